```python
import math
import jax, jax.numpy as jnp
from jax import lax
import numpy as np

D_MODEL = 1024
BATCH = 8
SEQ = 2048
DEPTH = 2
DEC_BATCH = 128
DEC_SEQ = 4
PAST_LEN = 16384
PAGE_SIZE = 128

N_MIXERS = 2
N_SSM_LAYERS = (DEPTH + 1) // 2
N_CONV_LAYERS = DEPTH // 2
SSM_GROUP = 16
SSM_GROUPS = D_MODEL // SSM_GROUP
SSM_STATE = 64
SSM_DT_MIN = 1e-3
SSM_DT_MAX = 1e-1
CONV_WIDTH = 3
MEM_TOKENS = 256
MEM_HEADS = 4
MEM_HEAD_DIM = D_MODEL // MEM_HEADS
PEER_HEADS = 8
PEER_KEYS = 128
PEER_EXPERTS = PEER_KEYS * PEER_KEYS
PEER_TOPK = 16
PEER_QUERY_DIM = 256
PEER_HALF = PEER_QUERY_DIM // 2
PEER_BLOCK = 256
RMS_EPS = 1e-6

kernel_name = 'hybrid_s5_shortconv_peer_memxattn_step'


def rmsnorm(x, g):
    xf = x.astype(jnp.float32)
    r = lax.rsqrt(jnp.mean(xf * xf, axis=-1, keepdims=True) + RMS_EPS)
    return (xf * r).astype(x.dtype) * g


def _cmul_combine(e1, e2):
    ar1, ai1, br1, bi1 = e1
    ar2, ai2, br2, bi2 = e2
    return (ar1 * ar2 - ai1 * ai2,
            ar1 * ai2 + ai1 * ar2,
            ar2 * br1 - ai2 * bi1 + br2,
            ar2 * bi1 + ai2 * br1 + bi2)


def s5_mixer(h, s0_re, s0_im, a_re, a_im, log_dt, b_re, b_im, c_re, c_im, d_skip, w_glu):
    f32 = jnp.float32
    bsz, s, _ = h.shape
    a_re = a_re.astype(f32)
    a_im = a_im.astype(f32)
    dt = jnp.exp(log_dt.astype(f32))[:, None]
    mag = jnp.exp(a_re * dt)
    ang = a_im * dt
    lb_re = mag * jnp.cos(ang)
    lb_im = mag * jnp.sin(ang)
    den = a_re * a_re + a_im * a_im
    f_re = ((lb_re - 1.0) * a_re + lb_im * a_im) / den
    f_im = (lb_im * a_re - (lb_re - 1.0) * a_im) / den
    b_re = b_re.astype(f32)
    b_im = b_im.astype(f32)
    bb_re = f_re[..., None] * b_re - f_im[..., None] * b_im
    bb_im = f_re[..., None] * b_im + f_im[..., None] * b_re
    u = h.astype(f32).reshape(bsz, s, SSM_GROUPS, SSM_GROUP)
    bu_re = jnp.einsum('bsgc,gpc->bsgp', u, bb_re)
    bu_im = jnp.einsum('bsgc,gpc->bsgp', u, bb_im)
    s0_re = s0_re.astype(f32)
    s0_im = s0_im.astype(f32)
    bu_re = bu_re.at[:, 0].add(lb_re * s0_re - lb_im * s0_im)
    bu_im = bu_im.at[:, 0].add(lb_re * s0_im + lb_im * s0_re)
    a_full_re = jnp.broadcast_to(lb_re, bu_re.shape)
    a_full_im = jnp.broadcast_to(lb_im, bu_im.shape)
    _, _, st_re, st_im = lax.associative_scan(
        _cmul_combine, (a_full_re, a_full_im, bu_re, bu_im), axis=1)
    c_re = c_re.astype(f32)
    c_im = c_im.astype(f32)
    y = (jnp.einsum('bsgp,gcp->bsgc', st_re, c_re)
         - jnp.einsum('bsgp,gcp->bsgc', st_im, c_im)).reshape(bsz, s, D_MODEL)
    y = y + d_skip.astype(f32) * h.astype(f32)
    z = jax.nn.gelu(y, approximate=False).astype(h.dtype)
    g_a, g_b = jnp.split(z @ w_glu, 2, axis=-1)
    return g_a * jax.nn.sigmoid(g_b), st_re[:, -1], st_im[:, -1]


def short_conv_mixer(h, buf, w_in, w_conv, w_out):
    s = h.shape[1]
    b_gate, c_gate, hv = jnp.split(h @ w_in, 3, axis=-1)
    v = c_gate * hv
    vp = jnp.concatenate([buf.astype(v.dtype), v], axis=1)
    conv = w_conv[0] * vp[:, 0:s]
    for k in range(1, CONV_WIDTH):
        conv = conv + w_conv[k] * vp[:, k:k + s]
    out = (b_gate * conv) @ w_out
    return out, vp[:, -(CONV_WIDTH - 1):]


def memory_kv(mem, w_k, w_v):
    bsz = mem.shape[0]
    k = jnp.einsum('bmd,ldf->lbmf', mem, w_k).reshape(DEPTH, bsz, MEM_TOKENS, MEM_HEADS, MEM_HEAD_DIM)
    v = jnp.einsum('bmd,ldf->lbmf', mem, w_v).reshape(DEPTH, bsz, MEM_TOKENS, MEM_HEADS, MEM_HEAD_DIM)
    return k, v


def memory_attention(h, mem_k, mem_v, w_q, w_o):
    bsz, s, _ = h.shape
    q = (h @ w_q).reshape(bsz, s, MEM_HEADS, MEM_HEAD_DIM)
    sc = jnp.einsum('bshd,bmhd->bhsm', q, mem_k).astype(jnp.float32) * (MEM_HEAD_DIM ** -0.5)
    p = jax.nn.softmax(sc, axis=-1).astype(h.dtype)
    o = jnp.einsum('bhsm,bmhd->bshd', p, mem_v).reshape(bsz, s, D_MODEL)
    return o @ w_o


def peer_ffn(h, w_query, key1, key2, u_tab, v_tab):
    bsz, s, d = h.shape
    t = bsz * s
    nb = -(-t // PEER_BLOCK)
    flat = jnp.pad(h.reshape(t, d), ((0, nb * PEER_BLOCK - t), (0, 0)))

    def block(hb):
        q = (hb @ w_query).reshape(PEER_BLOCK, PEER_HEADS, 2, PEER_HALF)
        s1 = jnp.einsum('thk,nk->thn', q[:, :, 0], key1)
        s2 = jnp.einsum('thk,nk->thn', q[:, :, 1], key2)
        v1, i1 = lax.top_k(s1, PEER_TOPK)
        v2, i2 = lax.top_k(s2, PEER_TOPK)
        cand = (v1[..., :, None] + v2[..., None, :]).reshape(PEER_BLOCK, PEER_HEADS, PEER_TOPK * PEER_TOPK)
        sc, ci = lax.top_k(cand, PEER_TOPK)
        e = (jnp.take_along_axis(i1, ci // PEER_TOPK, axis=-1) * PEER_KEYS
             + jnp.take_along_axis(i2, ci % PEER_TOPK, axis=-1))
        g = jax.nn.softmax(sc.astype(jnp.float32), axis=-1).astype(hb.dtype)
        act = jax.nn.gelu(jnp.einsum('thkd,td->thk', u_tab[e], hb), approximate=False)
        return jnp.einsum('thk,thkd->td', g * act, v_tab[e])

    out = lax.map(block, flat.reshape(nb, PEER_BLOCK, d))
    return out.reshape(nb * PEER_BLOCK, d)[:t].reshape(bsz, s, d)


def trunk(x, ssm_re0, ssm_im0, conv0, mem_k, mem_v, weights):
    (norm_mix, norm_mem, norm_ffn, norm_final,
     ssm_a_re, ssm_a_im, ssm_log_dt, ssm_b_re, ssm_b_im, ssm_c_re, ssm_c_im, ssm_d, ssm_w_glu,
     conv_w_in, conv_w, conv_w_out,
     mem_w_q, mem_w_o,
     peer_w_query, peer_key1, peer_key2, peer_u, peer_v) = weights
    ssm_re_out, ssm_im_out, conv_out = [], [], []
    for i in range(DEPTH):
        j = i // N_MIXERS
        h = rmsnorm(x, norm_mix[i])
        if i % N_MIXERS == 0:
            out, sr, si = s5_mixer(h, ssm_re0[j], ssm_im0[j], ssm_a_re[j], ssm_a_im[j], ssm_log_dt[j],
                                   ssm_b_re[j], ssm_b_im[j], ssm_c_re[j], ssm_c_im[j], ssm_d[j], ssm_w_glu[j])
            ssm_re_out.append(sr)
            ssm_im_out.append(si)
        else:
            out, cb = short_conv_mixer(h, conv0[j], conv_w_in[j], conv_w[j], conv_w_out[j])
            conv_out.append(cb)
        x = x + out
        x = x + memory_attention(rmsnorm(x, norm_mem[i]), mem_k[i], mem_v[i], mem_w_q[i], mem_w_o[i])
        x = x + peer_ffn(rmsnorm(x, norm_ffn[i]), peer_w_query[i], peer_key1[i], peer_key2[i],
                         peer_u[i], peer_v[i])
    y = rmsnorm(x, norm_final)
    return y, jnp.stack(ssm_re_out), jnp.stack(ssm_im_out), jnp.stack(conv_out)


def setup_inputs(seed: int = 0) -> dict:
    key = jax.random.key(seed)
    ks = jax.random.split(key, 40)
    f32 = jnp.float32
    nrm = lambda i, shape, scale: jax.random.normal(ks[i], shape, f32) * scale
    inp = {}
    inp['x_prompt'] = nrm(0, (BATCH, SEQ, D_MODEL), 1.0)
    inp['x_sample'] = nrm(1, (DEC_BATCH, DEC_SEQ, D_MODEL), 1.0)
    inp['mem_prompt'] = nrm(2, (BATCH, MEM_TOKENS, D_MODEL), 1.0)
    inp['state_ssm_re'] = nrm(3, (N_SSM_LAYERS, DEC_BATCH, SSM_GROUPS, SSM_STATE), 0.1)
    inp['state_ssm_im'] = nrm(4, (N_SSM_LAYERS, DEC_BATCH, SSM_GROUPS, SSM_STATE), 0.1)
    inp['state_conv'] = nrm(5, (N_CONV_LAYERS, DEC_BATCH, CONV_WIDTH - 1, D_MODEL), 1.0)
    inp['cache_mem_k'] = nrm(6, (DEPTH, DEC_BATCH, MEM_TOKENS, MEM_HEADS, MEM_HEAD_DIM), 1.0)
    inp['cache_mem_v'] = nrm(7, (DEPTH, DEC_BATCH, MEM_TOKENS, MEM_HEADS, MEM_HEAD_DIM), 1.0)
    inp['norm_mix'] = 1.0 + nrm(8, (DEPTH, D_MODEL), 0.01)
    inp['norm_mem'] = 1.0 + nrm(9, (DEPTH, D_MODEL), 0.01)
    inp['norm_ffn'] = 1.0 + nrm(10, (DEPTH, D_MODEL), 0.01)
    inp['norm_final'] = 1.0 + nrm(11, (D_MODEL,), 0.01)
    inp['ssm_a_re'] = -0.5 + nrm(12, (N_SSM_LAYERS, SSM_GROUPS, SSM_STATE), 0.01)
    inp['ssm_a_im'] = jnp.pi * jnp.arange(SSM_STATE, dtype=f32) + nrm(13, (N_SSM_LAYERS, SSM_GROUPS, SSM_STATE), 0.01)
    inp['ssm_log_dt'] = jax.random.uniform(ks[14], (N_SSM_LAYERS, SSM_GROUPS), f32,
                                           math.log(SSM_DT_MIN), math.log(SSM_DT_MAX))
    inp['ssm_b_re'] = nrm(15, (N_SSM_LAYERS, SSM_GROUPS, SSM_STATE, SSM_GROUP), (2 * SSM_GROUP) ** -0.5)
    inp['ssm_b_im'] = nrm(16, (N_SSM_LAYERS, SSM_GROUPS, SSM_STATE, SSM_GROUP), (2 * SSM_GROUP) ** -0.5)
    inp['ssm_c_re'] = nrm(17, (N_SSM_LAYERS, SSM_GROUPS, SSM_GROUP, SSM_STATE), SSM_STATE ** -0.5)
    inp['ssm_c_im'] = nrm(18, (N_SSM_LAYERS, SSM_GROUPS, SSM_GROUP, SSM_STATE), SSM_STATE ** -0.5)
    inp['ssm_d'] = nrm(19, (N_SSM_LAYERS, D_MODEL), 1.0)
    inp['ssm_w_glu'] = nrm(20, (N_SSM_LAYERS, D_MODEL, 2 * D_MODEL), D_MODEL ** -0.5)
    inp['conv_w_in'] = nrm(21, (N_CONV_LAYERS, D_MODEL, 3 * D_MODEL), D_MODEL ** -0.5)
    inp['conv_w'] = nrm(22, (N_CONV_LAYERS, CONV_WIDTH, D_MODEL), CONV_WIDTH ** -0.5)
    inp['conv_w_out'] = nrm(23, (N_CONV_LAYERS, D_MODEL, D_MODEL), D_MODEL ** -0.5)
    inp['mem_w_q'] = nrm(24, (DEPTH, D_MODEL, D_MODEL), D_MODEL ** -0.5)
    inp['mem_w_k'] = nrm(25, (DEPTH, D_MODEL, D_MODEL), D_MODEL ** -0.5)
    inp['mem_w_v'] = nrm(26, (DEPTH, D_MODEL, D_MODEL), D_MODEL ** -0.5)
    inp['mem_w_o'] = nrm(27, (DEPTH, D_MODEL, D_MODEL), D_MODEL ** -0.5)
    inp['peer_w_query'] = nrm(28, (DEPTH, D_MODEL, PEER_HEADS * PEER_QUERY_DIM), D_MODEL ** -0.5)
    inp['peer_key1'] = nrm(29, (DEPTH, PEER_KEYS, PEER_HALF), PEER_HALF ** -0.5)
    inp['peer_key2'] = nrm(30, (DEPTH, PEER_KEYS, PEER_HALF), PEER_HALF ** -0.5)
    inp['peer_u'] = nrm(31, (DEPTH, PEER_EXPERTS, D_MODEL), D_MODEL ** -0.5)
    inp['peer_v'] = nrm(32, (DEPTH, PEER_EXPERTS, D_MODEL), (PEER_HEADS * PEER_TOPK) ** -0.5)
    return inp


def reference(x_prompt, x_sample, mem_prompt, state_ssm_re, state_ssm_im, state_conv,
              cache_mem_k, cache_mem_v,
              norm_mix, norm_mem, norm_ffn, norm_final,
              ssm_a_re, ssm_a_im, ssm_log_dt, ssm_b_re, ssm_b_im, ssm_c_re, ssm_c_im, ssm_d, ssm_w_glu,
              conv_w_in, conv_w, conv_w_out,
              mem_w_q, mem_w_k, mem_w_v, mem_w_o,
              peer_w_query, peer_key1, peer_key2, peer_u, peer_v):
    weights = (norm_mix, norm_mem, norm_ffn, norm_final,
               ssm_a_re, ssm_a_im, ssm_log_dt, ssm_b_re, ssm_b_im, ssm_c_re, ssm_c_im, ssm_d, ssm_w_glu,
               conv_w_in, conv_w, conv_w_out,
               mem_w_q, mem_w_o,
               peer_w_query, peer_key1, peer_key2, peer_u, peer_v)
    bsz = x_prompt.shape[0]
    zero_re = jnp.zeros((N_SSM_LAYERS, bsz, SSM_GROUPS, SSM_STATE), jnp.float32)
    zero_im = jnp.zeros((N_SSM_LAYERS, bsz, SSM_GROUPS, SSM_STATE), jnp.float32)
    zero_conv = jnp.zeros((N_CONV_LAYERS, bsz, CONV_WIDTH - 1, D_MODEL), x_prompt.dtype)
    mem_k_prompt, mem_v_prompt = memory_kv(mem_prompt, mem_w_k, mem_w_v)
    y_prompt, ssm_re_prompt, ssm_im_prompt, conv_prompt = trunk(
        x_prompt, zero_re, zero_im, zero_conv, mem_k_prompt, mem_v_prompt, weights)
    y_sample, ssm_re_sample, ssm_im_sample, conv_sample = trunk(
        x_sample, state_ssm_re, state_ssm_im, state_conv, cache_mem_k, cache_mem_v, weights)
    return (y_prompt, y_sample, ssm_re_prompt, ssm_im_prompt, conv_prompt, mem_k_prompt, mem_v_prompt,
            ssm_re_sample, ssm_im_sample, conv_sample)
```

```python
import functools
import math

import jax
import jax.numpy as jnp
from jax import lax
from jax.experimental import pallas as pl
from jax.experimental.pallas import tpu as pltpu

F32 = jnp.float32
BF16 = jnp.bfloat16

D_MODEL = 1024
SSM_GROUP = 16
SSM_GROUPS = D_MODEL // SSM_GROUP
SSM_STATE = 64
SSM_PAIRS = SSM_GROUPS // 2
MEM_TOKENS = 256
MEM_HEADS = 4
MEM_HEAD_DIM = D_MODEL // MEM_HEADS
PEER_HEADS = 8
PEER_KEYS = 128
PEER_EXPERTS = PEER_KEYS * PEER_KEYS
PEER_TOPK = 16
RMS_EPS = 1e-6

LANES = 128
ROW_TILE = 512
S5_CHUNK = 16
EXPERT_TILE = 1024
VMEM_LIMIT = 48 * 1024 * 1024

_NT = (((1,), (1,)), ((), ()))


def _params(*sem):
    return pltpu.CompilerParams(dimension_semantics=sem, vmem_limit_bytes=VMEM_LIMIT)


def _rms(x, g):
    r = lax.rsqrt(jnp.mean(x * x, axis=-1, keepdims=True) + RMS_EPS)
    return (x * r) * g


def _gelu(x):
    return 0.5 * x * (1.0 + lax.erf(x * (1.0 / math.sqrt(2.0))))


def _sigmoid(x):
    return 1.0 / (1.0 + jnp.exp(-x))


def _norm_cast_kernel(x_ref, g_ref, o_ref):
    o_ref[...] = _rms(x_ref[...], g_ref[...]).astype(BF16)


def _norm_cast(x, g):
    t, d = x.shape
    tm = min(ROW_TILE, t)
    return pl.pallas_call(
        _norm_cast_kernel,
        grid=(t // tm,),
        in_specs=[pl.BlockSpec((tm, d), lambda i: (i, 0)), pl.BlockSpec((1, d), lambda i: (0, 0))],
        out_specs=pl.BlockSpec((tm, d), lambda i: (i, 0)),
        out_shape=jax.ShapeDtypeStruct((t, d), BF16),
        compiler_params=_params("parallel"),
        name="norm_cast",
    )(x, g.reshape(1, d))


def _mm_kernel(*refs, has_norm, has_res):
    x_ref, w_ref = refs[0], refs[1]
    pos = 2
    x = x_ref[...]
    if has_norm:
        x = _rms(x, refs[pos][...])
        pos += 1
    y = jnp.dot(x.astype(BF16), w_ref[...], preferred_element_type=F32)
    if has_res:
        y = y + refs[pos][...]
        pos += 1
    refs[pos][...] = y


def _mm(x, w, g=None, res=None, tm=ROW_TILE):
    t, k = x.shape
    n = w.shape[1]
    tm = min(tm, t)
    args = [x, w]
    specs = [pl.BlockSpec((tm, k), lambda i: (i, 0)), pl.BlockSpec((k, n), lambda i: (0, 0))]
    if g is not None:
        args.append(g.reshape(1, k))
        specs.append(pl.BlockSpec((1, k), lambda i: (0, 0)))
    if res is not None:
        args.append(res)
        specs.append(pl.BlockSpec((tm, n), lambda i: (i, 0)))
    return pl.pallas_call(
        functools.partial(_mm_kernel, has_norm=g is not None, has_res=res is not None),
        grid=(t // tm,),
        in_specs=specs,
        out_specs=pl.BlockSpec((tm, n), lambda i: (i, 0)),
        out_shape=jax.ShapeDtypeStruct((t, n), F32),
        compiler_params=_params("parallel"),
        name="mm_rows",
    )(*args)


def _s5_fold(a_re, a_im, log_dt, b_re, b_im, c_re, c_im, chunk):
    hi = lax.Precision.HIGHEST
    dt = jnp.exp(log_dt)[:, None]
    mag = jnp.exp(a_re * dt)
    ang = a_im * dt
    lb_re = mag * jnp.cos(ang)
    lb_im = mag * jnp.sin(ang)
    den = a_re * a_re + a_im * a_im
    f_re = ((lb_re - 1.0) * a_re + lb_im * a_im) / den
    f_im = (lb_im * a_re - (lb_re - 1.0) * a_im) / den
    bb_re = f_re[..., None] * b_re - f_im[..., None] * b_im
    bb_im = f_re[..., None] * b_im + f_im[..., None] * b_re
    pw_re, pw_im = [jnp.ones_like(lb_re)], [jnp.zeros_like(lb_im)]
    for _ in range(chunk):
        pr, pi = pw_re[-1], pw_im[-1]
        pw_re.append(pr * lb_re - pi * lb_im)
        pw_im.append(pr * lb_im + pi * lb_re)
    pw_re, pw_im = jnp.stack(pw_re), jnp.stack(pw_im)
    cl_re = c_re[None] * pw_re[:, :, None, :] - c_im[None] * pw_im[:, :, None, :]
    cl_im = c_re[None] * pw_im[:, :, None, :] + c_im[None] * pw_re[:, :, None, :]
    kern = (jnp.einsum('kgdp,gpc->kgdc', cl_re[:chunk], bb_re, precision=hi)
            - jnp.einsum('kgdp,gpc->kgdc', cl_im[:chunk], bb_im, precision=hi))
    step = jnp.arange(chunk)
    lag = step[None, :] - step[:, None]
    m = jnp.where((lag >= 0)[:, :, None, None, None], kern[jnp.clip(lag, 0, chunk - 1)], 0.0)
    m = m.transpose(2, 0, 4, 1, 3).reshape(SSM_GROUPS, chunk * SSM_GROUP, chunk * SSM_GROUP)
    rp_re, rp_im = pw_re[chunk - 1 - step], pw_im[chunk - 1 - step]
    p_re = rp_re[..., None] * bb_re[None] - rp_im[..., None] * bb_im[None]
    p_im = rp_re[..., None] * bb_im[None] + rp_im[..., None] * bb_re[None]
    p_re = p_re.transpose(1, 0, 3, 2).reshape(SSM_GROUPS, chunk * SSM_GROUP, SSM_STATE)
    p_im = p_im.transpose(1, 0, 3, 2).reshape(SSM_GROUPS, chunk * SSM_GROUP, SSM_STATE)
    n_re = cl_re[1:].transpose(1, 3, 0, 2).reshape(SSM_GROUPS, SSM_STATE, chunk * SSM_GROUP)
    n_im = (-cl_im[1:]).transpose(1, 3, 0, 2).reshape(SSM_GROUPS, SSM_STATE, chunk * SSM_GROUP)

    eye = jnp.eye(2, dtype=F32)

    def pair_diag(z):
        g, r, c = z.shape
        z = z.reshape(g // 2, 2, r, c)
        return jnp.einsum('pgrc,gh->pgrhc', z, eye).reshape(g // 2, 2 * r, 2 * c)

    lam_re = pw_re[chunk].reshape(SSM_PAIRS, 1, 2 * SSM_STATE)
    lam_im = pw_im[chunk].reshape(SSM_PAIRS, 1, 2 * SSM_STATE)
    return (pair_diag(m).astype(BF16), pair_diag(p_re).astype(BF16), pair_diag(p_im).astype(BF16),
            pair_diag(n_re).astype(BF16), pair_diag(n_im).astype(BF16), lam_re, lam_im)


def _s5_core_kernel(x_ref, m_ref, pre_ref, pim_ref, nre_ref, nim_ref, lre_ref, lim_ref,
                    s0re_ref, s0im_ref, y_ref, sre_ref, sim_ref, qre, qim, stre, stim,
                    *, nchunks, rows):
    x = x_ref[0]
    qre[...] = jnp.dot(x, pre_ref[0], preferred_element_type=F32)
    qim[...] = jnp.dot(x, pim_ref[0], preferred_element_type=F32)
    lr = jnp.broadcast_to(lre_ref[0], (rows, 2 * SSM_STATE))
    li = jnp.broadcast_to(lim_ref[0], (rows, 2 * SSM_STATE))

    def body(k, carry):
        sr, si = carry
        off = pl.multiple_of(k * rows, rows)
        stre[pl.ds(off, rows), :] = sr
        stim[pl.ds(off, rows), :] = si
        nr = lr * sr - li * si + qre[pl.ds(off, rows), :]
        ni = lr * si + li * sr + qim[pl.ds(off, rows), :]
        return nr, ni

    sr, si = lax.fori_loop(0, nchunks, body, (s0re_ref[0], s0im_ref[0]))
    sre_ref[0] = sr
    sim_ref[0] = si
    y = jnp.dot(x, m_ref[0], preferred_element_type=F32)
    y = y + jnp.dot(stre[...].astype(BF16), nre_ref[0], preferred_element_type=F32)
    y = y + jnp.dot(stim[...].astype(BF16), nim_ref[0], preferred_element_type=F32)
    y_ref[0] = y


def _s5_core(xp, folded, s0_re, s0_im, nchunks, rows):
    m, p_re, p_im, n_re, n_im, lam_re, lam_im = folded
    npair, nr, w = xp.shape
    sl = 2 * SSM_STATE
    blk = lambda a, b: pl.BlockSpec((1, a, b), lambda i: (i, 0, 0))
    return pl.pallas_call(
        functools.partial(_s5_core_kernel, nchunks=nchunks, rows=rows),
        grid=(npair,),
        in_specs=[blk(nr, w), blk(w, w), blk(w, sl), blk(w, sl), blk(sl, w), blk(sl, w),
                  blk(1, sl), blk(1, sl), blk(rows, sl), blk(rows, sl)],
        out_specs=[blk(nr, w), blk(rows, sl), blk(rows, sl)],
        out_shape=[jax.ShapeDtypeStruct((npair, nr, w), F32),
                   jax.ShapeDtypeStruct((npair, rows, sl), F32),
                   jax.ShapeDtypeStruct((npair, rows, sl), F32)],
        scratch_shapes=[pltpu.VMEM((nr, sl), F32)] * 4,
        compiler_params=_params("parallel"),
        name="s5_core",
    )(xp, m, p_re, p_im, n_re, n_im, lam_re, lam_im, s0_re, s0_im)


def _s5_out_kernel(x_ref, y_ref, g_ref, d_ref, w_ref, o_ref):
    x = x_ref[...]
    h = _rms(x, g_ref[...])
    z = _gelu(y_ref[...] + d_ref[...] * h)
    gg = jnp.dot(z.astype(BF16), w_ref[...], preferred_element_type=F32)
    o_ref[...] = x + gg[:, :D_MODEL] * _sigmoid(gg[:, D_MODEL:])


def _s5_out(x, y, g, d, w_glu):
    t = x.shape[0]
    tm = min(ROW_TILE, t)
    row = pl.BlockSpec((tm, D_MODEL), lambda i: (i, 0))
    vec = pl.BlockSpec((1, D_MODEL), lambda i: (0, 0))
    return pl.pallas_call(
        _s5_out_kernel,
        grid=(t // tm,),
        in_specs=[row, row, vec, vec, pl.BlockSpec((D_MODEL, 2 * D_MODEL), lambda i: (0, 0))],
        out_specs=row,
        out_shape=jax.ShapeDtypeStruct((t, D_MODEL), F32),
        compiler_params=_params("parallel"),
        name="s5_out",
    )(x, y, g.reshape(1, -1), d.reshape(1, -1), w_glu)


def _s5_mixer(x, s0_re, s0_im, g, folded, d_skip, w_glu, chunk):
    b, s, d = x.shape
    nk = s // chunk
    x2 = x.reshape(b * s, d)
    h = _norm_cast(x2, g)
    xp = (h.reshape(b, nk, chunk, SSM_PAIRS, 2, SSM_GROUP).transpose(3, 1, 0, 4, 2, 5)
          .reshape(SSM_PAIRS, nk * b, 2 * chunk * SSM_GROUP))
    to_pair = lambda z: z.reshape(b, SSM_PAIRS, 2 * SSM_STATE).transpose(1, 0, 2)
    y, sr, si = _s5_core(xp, folded, to_pair(s0_re), to_pair(s0_im), nk, b)
    y = (y.reshape(SSM_PAIRS, nk, b, 2, chunk, SSM_GROUP).transpose(2, 1, 4, 0, 3, 5)
         .reshape(b * s, d))
    from_pair = lambda z: z.transpose(1, 0, 2).reshape(b, SSM_GROUPS, SSM_STATE)
    out = _s5_out(x2, y, g, d_skip, w_glu)
    return out.reshape(b, s, d), from_pair(sr), from_pair(si)


CONV_HEAD = 8


def _conv_prompt_kernel(x_ref, g_ref, win_ref, wc_ref, wout_ref, o_ref, tail_ref, vbuf, *, tm):
    @pl.when(pl.program_id(1) == 0)
    def _():
        vbuf[0:CONV_HEAD, :] = jnp.zeros((CONV_HEAD, D_MODEL), F32)

    x = x_ref[0]
    h = _rms(x, g_ref[...])
    p = jnp.dot(h.astype(BF16), win_ref[...], preferred_element_type=F32)
    bg = p[:, :D_MODEL]
    v = p[:, D_MODEL:2 * D_MODEL] * p[:, 2 * D_MODEL:]
    vbuf[CONV_HEAD:CONV_HEAD + tm, :] = v
    v2 = vbuf[CONV_HEAD - 2:CONV_HEAD - 2 + tm, :]
    v1 = vbuf[CONV_HEAD - 1:CONV_HEAD - 1 + tm, :]
    conv = wc_ref[0:1, :] * v2 + wc_ref[1:2, :] * v1 + wc_ref[2:3, :] * v
    out = jnp.dot((bg * conv).astype(BF16), wout_ref[...], preferred_element_type=F32)
    o_ref[0] = x + out
    tail = vbuf[tm:tm + CONV_HEAD, :]
    vbuf[0:CONV_HEAD, :] = tail
    tail_ref[0] = tail


def _conv_prompt(x, g, w_in, w_conv, w_out):
    b, s, d = x.shape
    tm = min(ROW_TILE, s)
    return pl.pallas_call(
        functools.partial(_conv_prompt_kernel, tm=tm),
        grid=(b, s // tm),
        in_specs=[pl.BlockSpec((1, tm, d), lambda i, j: (i, j, 0)),
                  pl.BlockSpec((1, d), lambda i, j: (0, 0)),
                  pl.BlockSpec((d, 3 * d), lambda i, j: (0, 0)),
                  pl.BlockSpec((3, d), lambda i, j: (0, 0)),
                  pl.BlockSpec((d, d), lambda i, j: (0, 0))],
        out_specs=[pl.BlockSpec((1, tm, d), lambda i, j: (i, j, 0)),
                   pl.BlockSpec((1, CONV_HEAD, d), lambda i, j: (i, 0, 0))],
        out_shape=[jax.ShapeDtypeStruct((b, s, d), F32),
                   jax.ShapeDtypeStruct((b, CONV_HEAD, d), F32)],
        scratch_shapes=[pltpu.VMEM((CONV_HEAD + tm, d), F32)],
        compiler_params=_params("parallel", "arbitrary"),
        name="conv_prompt",
    )(x, g.reshape(1, d), w_in, w_conv, w_out)


def _conv_sample_kernel(x_ref, buf_ref, g_ref, win_ref, wc_ref, wout_ref, o_ref, nbuf_ref, *, steps, nb):
    x = x_ref[...]
    h = _rms(x, g_ref[...])
    p = jnp.dot(h.astype(BF16), win_ref[...], preferred_element_type=F32)
    bg = p[:, :D_MODEL]
    v = p[:, D_MODEL:2 * D_MODEL] * p[:, 2 * D_MODEL:]
    vp = [buf_ref[0:nb, :], buf_ref[nb:2 * nb, :]] + [v[t * nb:(t + 1) * nb, :] for t in range(steps)]
    conv = jnp.concatenate(
        [wc_ref[0:1, :] * vp[t] + wc_ref[1:2, :] * vp[t + 1] + wc_ref[2:3, :] * vp[t + 2]
         for t in range(steps)], axis=0)
    out = jnp.dot((bg * conv).astype(BF16), wout_ref[...], preferred_element_type=F32)
    o_ref[...] = x + out
    nbuf_ref[0:nb, :] = vp[-2]
    nbuf_ref[nb:2 * nb, :] = vp[-1]


def _conv_sample(x_tm, buf_tm, g, w_in, w_conv, w_out, steps, nb):
    d = D_MODEL
    full = lambda r, c: pl.BlockSpec((r, c), lambda i: (0, 0))
    return pl.pallas_call(
        functools.partial(_conv_sample_kernel, steps=steps, nb=nb),
        grid=(1,),
        in_specs=[full(steps * nb, d), full(2 * nb, d), full(1, d), full(d, 3 * d), full(3, d), full(d, d)],
        out_specs=[full(steps * nb, d), full(2 * nb, d)],
        out_shape=[jax.ShapeDtypeStruct((steps * nb, d), F32), jax.ShapeDtypeStruct((2 * nb, d), F32)],
        compiler_params=_params("arbitrary"),
        name="conv_sample",
    )(x_tm, buf_tm, g.reshape(1, d), w_in, w_conv, w_out)


def _softmax_rows(s):
    e = jnp.exp(s - jnp.max(s, axis=-1, keepdims=True))
    return e / jnp.sum(e, axis=-1, keepdims=True)


def _attn_heads(q, k, v):
    outs = []
    for hd in range(MEM_HEADS):
        sl = slice(hd * MEM_HEAD_DIM, (hd + 1) * MEM_HEAD_DIM)
        s = lax.dot_general(q[:, sl].astype(BF16), k[:, sl], _NT, preferred_element_type=F32)
        p = _softmax_rows(s * (MEM_HEAD_DIM ** -0.5))
        outs.append(jnp.dot(p.astype(BF16), v[:, sl], preferred_element_type=F32))
    return jnp.concatenate(outs, axis=-1)


def _attn_prompt_kernel(x_ref, g_ref, k_ref, v_ref, wq_ref, wo_ref, o_ref):
    x = x_ref[0]
    h = _rms(x, g_ref[...])
    q = jnp.dot(h.astype(BF16), wq_ref[...], preferred_element_type=F32)
    o = _attn_heads(q, k_ref[0].astype(BF16), v_ref[0].astype(BF16))
    o_ref[0] = x + jnp.dot(o.astype(BF16), wo_ref[...], preferred_element_type=F32)


def _attn_prompt(x, g, mem_k, mem_v, w_q, w_o):
    b, s, d = x.shape
    tm = min(ROW_TILE, s)
    mem = pl.BlockSpec((1, MEM_TOKENS, d), lambda i, j: (i, 0, 0))
    wgt = pl.BlockSpec((d, d), lambda i, j: (0, 0))
    row = pl.BlockSpec((1, tm, d), lambda i, j: (i, j, 0))
    return pl.pallas_call(
        _attn_prompt_kernel,
        grid=(b, s // tm),
        in_specs=[row, pl.BlockSpec((1, d), lambda i, j: (0, 0)), mem, mem, wgt, wgt],
        out_specs=row,
        out_shape=jax.ShapeDtypeStruct((b, s, d), F32),
        compiler_params=_params("parallel", "parallel"),
        name="attn_prompt",
    )(x, g.reshape(1, d), mem_k, mem_v, w_q, w_o)


SAMPLE_Q_ROWS = 8
SAMPLE_SEQ_BLOCK = 4


def _attn_sample_kernel(q_ref, k_ref, v_ref, o_ref):
    def body(i, carry):
        o_ref[i] = _attn_heads(q_ref[i], k_ref[i].astype(BF16), v_ref[i].astype(BF16))
        return carry

    lax.fori_loop(0, SAMPLE_SEQ_BLOCK, body, 0)


def _attn_sample(x, g, cache_k, cache_v, w_q, w_o):
    b, s, d = x.shape
    x2 = x.reshape(b * s, d)
    q = _mm(x2, w_q, g=g).reshape(b, s, d)
    q = jnp.pad(q, ((0, 0), (0, SAMPLE_Q_ROWS - s), (0, 0)))
    qblk = pl.BlockSpec((SAMPLE_SEQ_BLOCK, SAMPLE_Q_ROWS, d), lambda i: (i, 0, 0))
    mblk = pl.BlockSpec((SAMPLE_SEQ_BLOCK, MEM_TOKENS, d), lambda i: (i, 0, 0))
    o = pl.pallas_call(
        _attn_sample_kernel,
        grid=(b // SAMPLE_SEQ_BLOCK,),
        in_specs=[qblk, mblk, mblk],
        out_specs=qblk,
        out_shape=jax.ShapeDtypeStruct((b, SAMPLE_Q_ROWS, d), F32),
        compiler_params=_params("parallel"),
        name="attn_sample",
    )(q, cache_k, cache_v)
    o = o[:, :s].reshape(b * s, d)
    return _mm(o, w_o, res=x2).reshape(b, s, d)


def _top16(s):
    iota = lax.broadcasted_iota(jnp.int32, s.shape, 0)
    pos = jnp.full(s.shape, PEER_TOPK, jnp.int32)
    vals = []
    for i in range(PEER_TOPK):
        m = jnp.max(s, axis=0, keepdims=True)
        idx = jnp.min(jnp.where(s == m, iota, s.shape[0]), axis=0, keepdims=True)
        hit = iota == idx
        pos = jnp.where(hit, i, pos)
        s = jnp.where(hit, -jnp.inf, s)
        vals.append(m)
    return vals, pos


def _pair_select(vals1, vals2):
    lanes = vals1[0].shape[1]
    iota16 = lax.broadcasted_iota(jnp.int32, (PEER_TOPK, lanes), 0)
    v2 = jnp.zeros((PEER_TOPK, lanes), F32)
    for b in range(PEER_TOPK):
        v2 = jnp.where(iota16 == b, vals2[b], v2)
    iota8 = iota16[:8]
    cand = [vals1[0] + v2]
    flat = [iota16]
    for a in range(1, PEER_TOPK):
        cand.append(jnp.where(iota8 < PEER_TOPK // (a + 1), vals1[a] + v2[:8], -jnp.inf))
        flat.append(iota8 + a * PEER_TOPK)
    nsel = jnp.zeros((PEER_TOPK, lanes), jnp.int32)
    z = jnp.zeros((1, lanes), F32)
    top = vals1[0] + vals2[0]
    nflat = PEER_TOPK * PEER_TOPK
    for _ in range(PEER_TOPK):
        mx = jnp.maximum(cand[0][:8], cand[0][8:])
        for a in range(1, PEER_TOPK):
            mx = jnp.maximum(mx, cand[a])
        m = jnp.max(mx, axis=0, keepdims=True)
        w0 = jnp.where(cand[0] == m, flat[0], nflat)
        mi = jnp.minimum(w0[:8], w0[8:])
        for a in range(1, PEER_TOPK):
            mi = jnp.minimum(mi, jnp.where(cand[a] == m, flat[a], nflat))
        idx = jnp.min(mi, axis=0, keepdims=True)
        cand = [jnp.where(flat[a] == idx, -jnp.inf, cand[a]) for a in range(PEER_TOPK)]
        nsel = nsel + (iota16 == (idx >> 4)).astype(jnp.int32)
        z = z + jnp.exp(m - top)
    return nsel, z


def _peer_select_kernel(x_ref, g_ref, wq_ref, k1_ref, k2_ref,
                        pos2_ref, m2_ref, nsel_ref, m1_ref, q_scr, *, tm):
    h = _rms(x_ref[...], g_ref[...]).astype(BF16)
    q = jnp.dot(h, wq_ref[...], preferred_element_type=F32)
    for j in range(2 * PEER_HEADS):
        q_scr[j] = q[:, j * PEER_KEYS:(j + 1) * PEER_KEYS].astype(BF16)
    k1 = k1_ref[...]
    k2 = k2_ref[...]

    def head(hh, carry):
        s1f = lax.dot_general(k1, q_scr[2 * hh], _NT, preferred_element_type=F32)
        s2f = lax.dot_general(k2, q_scr[2 * hh + 1], _NT, preferred_element_type=F32)
        for c in range(tm // LANES):
            sl = slice(c * LANES, (c + 1) * LANES)
            s1, s2 = s1f[:, sl], s2f[:, sl]
            vals1, pos1 = _top16(s1)
            vals2, pos2 = _top16(s2)
            nsel, z = _pair_select(vals1, vals2)
            nfull = jnp.zeros(s1.shape, jnp.int32)
            for a in range(PEER_TOPK):
                nfull = jnp.where(pos1 == a, nsel[a:a + 1, :], nfull)
            pos2_ref[hh, :, sl] = pos2.astype(F32)
            m2_ref[hh, :, sl] = jnp.exp(s2 - vals2[0])
            nsel_ref[hh, :, sl] = nfull.astype(F32)
            m1_ref[hh, :, sl] = jnp.exp(s1 - vals1[0]) * (1.0 / z)
        return carry

    lax.fori_loop(0, PEER_HEADS, head, 0)


PEER_SELECT_TILE = 256


def _peer_select(x, g, w_query, key1, key2):
    t, d = x.shape
    tm = PEER_SELECT_TILE
    nq = w_query.shape[1]
    head = pl.BlockSpec((PEER_HEADS, PEER_KEYS, tm), lambda i: (0, 0, i))
    keyspec = pl.BlockSpec((PEER_KEYS, PEER_KEYS), lambda i: (0, 0))
    shp = jax.ShapeDtypeStruct((PEER_HEADS, PEER_KEYS, t), F32)
    return pl.pallas_call(
        functools.partial(_peer_select_kernel, tm=tm),
        grid=(t // tm,),
        in_specs=[pl.BlockSpec((tm, d), lambda i: (i, 0)), pl.BlockSpec((1, d), lambda i: (0, 0)),
                  pl.BlockSpec((d, nq), lambda i: (0, 0)), keyspec, keyspec],
        out_specs=[head] * 4,
        out_shape=[shp] * 4,
        scratch_shapes=[pltpu.VMEM((2 * PEER_HEADS, tm, PEER_KEYS), BF16)],
        compiler_params=_params("parallel"),
        name="peer_select",
    )(x, g.reshape(1, d), w_query, key1, key2)


def _peer_dense_kernel(x_ref, g_ref, gf_ref, u_ref, vt_ref, pos2_ref, m2_ref, nsel_ref, m1_ref,
                       o_ref, h_scr, acc_scr, a_scr, z_scr, *, tm, te, final_norm):
    j = pl.program_id(1)

    @pl.when(j == 0)
    def _():
        h_scr[...] = _rms(x_ref[...], g_ref[...]).astype(BF16)
        acc_scr[...] = jnp.zeros(acc_scr.shape, F32)

    a_scr[...] = lax.dot_general(u_ref[...], h_scr[...], _NT, preferred_element_type=F32)
    n1 = te // PEER_KEYS
    first = pl.multiple_of(j * n1, n1)
    for c in range(tm // LANES):
        sl = slice(c * LANES, (c + 1) * LANES)
        ns = [nsel_ref[hh, pl.ds(first, n1), sl] for hh in range(PEER_HEADS)]
        mm = [m1_ref[hh, pl.ds(first, n1), sl] for hh in range(PEER_HEADS)]
        for r in range(n1):
            rows = slice(r * PEER_KEYS, (r + 1) * PEER_KEYS)
            w = jnp.zeros((PEER_KEYS, LANES), F32)
            for hh in range(PEER_HEADS):
                w = w + jnp.where(pos2_ref[hh, :, sl] < ns[hh][r:r + 1, :],
                                  m2_ref[hh, :, sl] * mm[hh][r:r + 1, :], 0.0)
            z_scr[rows, sl] = (_gelu(a_scr[rows, sl]) * w).astype(BF16)
    acc_scr[...] += jnp.dot(vt_ref[...], z_scr[...], preferred_element_type=F32)

    @pl.when(j == pl.num_programs(1) - 1)
    def _():
        out = x_ref[...] + acc_scr[...].T
        if final_norm:
            out = _rms(out, gf_ref[...])
        o_ref[...] = out


def _peer_dense(x, g, g_final, u, vt, sel, final_norm):
    t, d = x.shape
    tm = min(ROW_TILE, t)
    te = EXPERT_TILE
    row = pl.BlockSpec((tm, d), lambda i, j: (i, 0))
    vec = pl.BlockSpec((1, d), lambda i, j: (0, 0))
    head = pl.BlockSpec((PEER_HEADS, PEER_KEYS, tm), lambda i, j: (0, 0, i))
    return pl.pallas_call(
        functools.partial(_peer_dense_kernel, tm=tm, te=te, final_norm=final_norm),
        grid=(t // tm, PEER_EXPERTS // te),
        in_specs=[row, vec, vec, pl.BlockSpec((te, d), lambda i, j: (j, 0)),
                  pl.BlockSpec((d, te), lambda i, j: (0, j)), head, head, head, head],
        out_specs=row,
        out_shape=jax.ShapeDtypeStruct((t, d), F32),
        scratch_shapes=[pltpu.VMEM((tm, d), BF16), pltpu.VMEM((d, tm), F32),
                        pltpu.VMEM((te, tm), F32), pltpu.VMEM((te, tm), BF16)],
        compiler_params=_params("parallel", "arbitrary"),
        name="peer_dense",
    )(x, g.reshape(1, d), g_final.reshape(1, d), u, vt, *sel)


def _peer(x, g, g_final, w_query, key1, key2, u, vt, final_norm):
    shp = x.shape
    x2 = x.reshape(-1, shp[-1])
    sel = _peer_select(x2, g, w_query, key1, key2)
    return _peer_dense(x2, g, g_final, u, vt, sel, final_norm).reshape(shp)


def _trunk(x, ssm0, conv0, mem_k, mem_v, w, sample):
    b, s, d = x.shape
    chunk = s if sample else S5_CHUNK
    if ssm0 is None:
        ssm0 = (jnp.zeros((b, SSM_GROUPS, SSM_STATE), F32),) * 2
    x, ssm_re, ssm_im = _s5_mixer(x, ssm0[0], ssm0[1], w['norm_mix'][0], w['s5_fold'][chunk],
                                  w['ssm_d'][0], w['ssm_w_glu'][0], chunk)
    attn = _attn_sample if sample else _attn_prompt
    for i in range(2):
        if i == 1:
            if sample:
                x_tm = x.transpose(1, 0, 2).reshape(s * b, d)
                buf_tm = conv0.transpose(1, 0, 2).reshape(2 * b, d)
                x_tm, nbuf = _conv_sample(x_tm, buf_tm, w['norm_mix'][1], w['conv_w_in'][0],
                                          w['conv_w'][0], w['conv_w_out'][0], s, b)
                x = x_tm.reshape(s, b, d).transpose(1, 0, 2)
                conv_out = nbuf.reshape(2, b, d).transpose(1, 0, 2)
            else:
                x, tail = _conv_prompt(x, w['norm_mix'][1], w['conv_w_in'][0], w['conv_w'][0],
                                       w['conv_w_out'][0])
                conv_out = tail[:, CONV_HEAD - 2:]
        x = attn(x, w['norm_mem'][i], mem_k[i], mem_v[i], w['mem_w_q'][i], w['mem_w_o'][i])
        x = _peer(x, w['norm_ffn'][i], w['norm_final'], w['peer_w_query'][i], w['peer_key1'][i],
                  w['peer_key2'][i], w['peer_u'][i], w['peer_vt'][i], final_norm=(i == 1))
    return x, ssm_re[None], ssm_im[None], conv_out[None]


def kernel(x_prompt, x_sample, mem_prompt, state_ssm_re, state_ssm_im, state_conv, cache_mem_k, cache_mem_v, norm_mix, norm_mem, norm_ffn, norm_final, ssm_a_re, ssm_a_im, ssm_log_dt, ssm_b_re, ssm_b_im, ssm_c_re, ssm_c_im, ssm_d, ssm_w_glu, conv_w_in, conv_w, conv_w_out, mem_w_q, mem_w_k, mem_w_v, mem_w_o, peer_w_query, peer_key1, peer_key2, peer_u, peer_v):
    bsz, seq, d = x_prompt.shape
    dec_b, dec_s, _ = x_sample.shape
    depth = mem_w_q.shape[0]
    fold = lambda chunk: _s5_fold(ssm_a_re[0], ssm_a_im[0], ssm_log_dt[0], ssm_b_re[0], ssm_b_im[0],
                                  ssm_c_re[0], ssm_c_im[0], chunk)
    w = dict(
        norm_mix=norm_mix, norm_mem=norm_mem, norm_ffn=norm_ffn, norm_final=norm_final,
        s5_fold={S5_CHUNK: fold(S5_CHUNK), dec_s: fold(dec_s)},
        ssm_d=ssm_d, ssm_w_glu=ssm_w_glu.astype(BF16),
        conv_w_in=conv_w_in.astype(BF16), conv_w=conv_w, conv_w_out=conv_w_out.astype(BF16),
        mem_w_q=mem_w_q.astype(BF16), mem_w_o=mem_w_o.astype(BF16),
        peer_w_query=peer_w_query.astype(BF16), peer_key1=peer_key1.astype(BF16),
        peer_key2=peer_key2.astype(BF16), peer_u=peer_u.astype(BF16),
        peer_vt=peer_v.astype(BF16).transpose(0, 2, 1),
    )
    w_kv = jnp.concatenate([mem_w_k[i] for i in range(depth)] + [mem_w_v[i] for i in range(depth)],
                           axis=1).astype(BF16)
    kv = _mm(mem_prompt.reshape(bsz * MEM_TOKENS, d), w_kv, tm=256)
    kv = kv.reshape(bsz, MEM_TOKENS, 2 * depth, d).transpose(2, 0, 1, 3)
    mem_k_p, mem_v_p = kv[:depth], kv[depth:]

    y_p, re_p, im_p, conv_p = _trunk(x_prompt, None, None, mem_k_p, mem_v_p, w, sample=False)
    cache_k = cache_mem_k.reshape(depth, dec_b, MEM_TOKENS, d)
    cache_v = cache_mem_v.reshape(depth, dec_b, MEM_TOKENS, d)
    y_s, re_s, im_s, conv_s = _trunk(x_sample, (state_ssm_re[0], state_ssm_im[0]), state_conv[0],
                                     cache_k, cache_v, w, sample=True)
    head_shape = (depth, bsz, MEM_TOKENS, MEM_HEADS, MEM_HEAD_DIM)
    return (y_p, y_s, re_p, im_p, conv_p, mem_k_p.reshape(head_shape), mem_v_p.reshape(head_shape),
            re_s, im_s, conv_s)
```

```python
import functools
import math

import jax
import jax.numpy as jnp
from jax import lax
from jax.experimental import pallas as pl
from jax.experimental.pallas import tpu as pltpu

F32 = jnp.float32
BF16 = jnp.bfloat16

D_MODEL = 1024
SSM_GROUP = 16
SSM_GROUPS = D_MODEL // SSM_GROUP
SSM_STATE = 64
SSM_BLOCK_GROUPS = 8
SSM_BLOCKS = SSM_GROUPS // SSM_BLOCK_GROUPS
MEM_TOKENS = 256
MEM_HEADS = 4
MEM_HEAD_DIM = D_MODEL // MEM_HEADS
PEER_HEADS = 8
PEER_KEYS = 128
PEER_EXPERTS = PEER_KEYS * PEER_KEYS
PEER_TOPK = 16
RMS_EPS = 1e-6

LANES = 128
SUBLANES = 8
ROW_TILE = 512
S5_CHUNK = 16
S5_COL_TILE = 512
EXPERT_TILE = 1024
VMEM_LIMIT = 48 * 1024 * 1024

_NT = (((1,), (1,)), ((), ()))


def _params(*sem):
    return pltpu.CompilerParams(dimension_semantics=sem, vmem_limit_bytes=VMEM_LIMIT)


def _rms(x, g):
    r = lax.rsqrt(jnp.mean(x * x, axis=-1, keepdims=True) + RMS_EPS)
    return (x * r) * g


def _gelu(x):
    return 0.5 * x * (1.0 + lax.erf(x * (1.0 / math.sqrt(2.0))))


def _sigmoid(x):
    return 1.0 / (1.0 + jnp.exp(-x))


def _norm_cast_kernel(x_ref, g_ref, o_ref):
    o_ref[...] = _rms(x_ref[...], g_ref[...]).astype(BF16)


def _norm_cast(x, g):
    t, d = x.shape
    tm = min(ROW_TILE, t)
    return pl.pallas_call(
        _norm_cast_kernel,
        grid=(t // tm,),
        in_specs=[pl.BlockSpec((tm, d), lambda i: (i, 0)), pl.BlockSpec((1, d), lambda i: (0, 0))],
        out_specs=pl.BlockSpec((tm, d), lambda i: (i, 0)),
        out_shape=jax.ShapeDtypeStruct((t, d), BF16),
        compiler_params=_params("parallel"),
        name="norm_cast",
    )(x, g.reshape(1, d))


def _mm_kernel(*refs, has_norm, has_res):
    x_ref, w_ref = refs[0], refs[1]
    pos = 2
    x = x_ref[...]
    if has_norm:
        x = _rms(x, refs[pos][...])
        pos += 1
    y = jnp.dot(x.astype(BF16), w_ref[...], preferred_element_type=F32)
    if has_res:
        y = y + refs[pos][...]
        pos += 1
    refs[pos][...] = y


def _mm(x, w, g=None, res=None, tm=ROW_TILE):
    t, k = x.shape
    n = w.shape[1]
    tm = min(tm, t)
    args = [x, w]
    specs = [pl.BlockSpec((tm, k), lambda i: (i, 0)), pl.BlockSpec((k, n), lambda i: (0, 0))]
    if g is not None:
        args.append(g.reshape(1, k))
        specs.append(pl.BlockSpec((1, k), lambda i: (0, 0)))
    if res is not None:
        args.append(res)
        specs.append(pl.BlockSpec((tm, n), lambda i: (i, 0)))
    return pl.pallas_call(
        functools.partial(_mm_kernel, has_norm=g is not None, has_res=res is not None),
        grid=(t // tm,),
        in_specs=specs,
        out_specs=pl.BlockSpec((tm, n), lambda i: (i, 0)),
        out_shape=jax.ShapeDtypeStruct((t, n), F32),
        compiler_params=_params("parallel"),
        name="mm_rows",
    )(*args)


def _s5_fold(a_re, a_im, log_dt, b_re, b_im, c_re, c_im, chunk):
    hi = lax.Precision.HIGHEST
    dt = jnp.exp(log_dt)[:, None]
    mag = jnp.exp(a_re * dt)
    ang = a_im * dt
    lb_re = mag * jnp.cos(ang)
    lb_im = mag * jnp.sin(ang)
    den = a_re * a_re + a_im * a_im
    f_re = ((lb_re - 1.0) * a_re + lb_im * a_im) / den
    f_im = (lb_im * a_re - (lb_re - 1.0) * a_im) / den
    bb_re = f_re[..., None] * b_re - f_im[..., None] * b_im
    bb_im = f_re[..., None] * b_im + f_im[..., None] * b_re
    pw_re, pw_im = [jnp.ones_like(lb_re)], [jnp.zeros_like(lb_im)]
    for _ in range(chunk):
        pr, pi = pw_re[-1], pw_im[-1]
        pw_re.append(pr * lb_re - pi * lb_im)
        pw_im.append(pr * lb_im + pi * lb_re)
    pw_re, pw_im = jnp.stack(pw_re), jnp.stack(pw_im)
    cl_re = c_re[None] * pw_re[:, :, None, :] - c_im[None] * pw_im[:, :, None, :]
    cl_im = c_re[None] * pw_im[:, :, None, :] + c_im[None] * pw_re[:, :, None, :]
    kern = (jnp.einsum('kgdp,gpc->kgdc', cl_re[:chunk], bb_re, precision=hi)
            - jnp.einsum('kgdp,gpc->kgdc', cl_im[:chunk], bb_im, precision=hi))
    step = jnp.arange(chunk)
    lag = step[None, :] - step[:, None]
    m = jnp.where((lag >= 0)[:, :, None, None, None], kern[jnp.clip(lag, 0, chunk - 1)], 0.0)
    m = m.transpose(2, 0, 4, 1, 3)
    rp_re, rp_im = pw_re[chunk - 1 - step], pw_im[chunk - 1 - step]
    p_re = rp_re[..., None] * bb_re[None] - rp_im[..., None] * bb_im[None]
    p_im = rp_re[..., None] * bb_im[None] + rp_im[..., None] * bb_re[None]
    p_re = p_re.transpose(1, 0, 3, 2)
    p_im = p_im.transpose(1, 0, 3, 2)
    n_re = cl_re[1:].transpose(1, 3, 0, 2)
    n_im = (-cl_im[1:]).transpose(1, 3, 0, 2)

    nb, q = SSM_BLOCKS, SSM_BLOCK_GROUPS
    eye = jnp.eye(q, dtype=BF16)
    w = chunk * q * SSM_GROUP
    sl = q * SSM_STATE
    m = m.astype(BF16).reshape(nb, q, chunk, SSM_GROUP, chunk, 1, SSM_GROUP)
    m = (m.transpose(0, 2, 1, 3, 4, 5, 6) * eye[None, None, :, None, None, :, None]).reshape(nb, w, w)

    def fold_p(z):
        z = z.astype(BF16).reshape(nb, q, chunk, SSM_GROUP, 1, SSM_STATE).transpose(0, 2, 1, 3, 4, 5)
        return (z * eye[None, None, :, None, :, None]).reshape(nb, w, sl)

    def fold_n(z):
        z = z.astype(BF16).reshape(nb, q, SSM_STATE, chunk, 1, SSM_GROUP)
        return (z * eye[None, :, None, None, :, None]).reshape(nb, sl, w)

    lam_re = pw_re[chunk].reshape(nb, 1, sl)
    lam_im = pw_im[chunk].reshape(nb, 1, sl)
    return m, fold_p(p_re), fold_p(p_im), fold_n(n_re), fold_n(n_im), lam_re, lam_im


def _s5_core_kernel(x_ref, m_ref, pre_ref, pim_ref, nre_ref, nim_ref, lre_ref, lim_ref,
                    s0re_ref, s0im_ref, y_ref, sre_ref, sim_ref, stre, stim, *, nchunks, rows):
    x = x_ref[0]

    @pl.when(pl.program_id(1) == 0)
    def _():
        stre[...] = jnp.dot(x, pre_ref[0], preferred_element_type=F32)
        stim[...] = jnp.dot(x, pim_ref[0], preferred_element_type=F32)
        width = stre.shape[1]
        lr = jnp.broadcast_to(lre_ref[0], (rows, width))
        li = jnp.broadcast_to(lim_ref[0], (rows, width))

        def body(k, carry):
            sr, si = carry
            off = pl.multiple_of(k * rows, rows)
            qr = stre[pl.ds(off, rows), :]
            qi = stim[pl.ds(off, rows), :]
            stre[pl.ds(off, rows), :] = sr
            stim[pl.ds(off, rows), :] = si
            return lr * sr - li * si + qr, lr * si + li * sr + qi

        sr, si = lax.fori_loop(0, nchunks, body, (s0re_ref[0], s0im_ref[0]))
        sre_ref[0] = sr
        sim_ref[0] = si

    y = jnp.dot(x, m_ref[0], preferred_element_type=F32)
    y = y + jnp.dot(stre[...].astype(BF16), nre_ref[0], preferred_element_type=F32)
    y = y + jnp.dot(stim[...].astype(BF16), nim_ref[0], preferred_element_type=F32)
    y_ref[0] = y


def _s5_core(xb, folded, s0_re, s0_im, nchunks, rows):
    m, p_re, p_im, n_re, n_im, lam_re, lam_im = folded
    nblk, nr, w = xb.shape
    sl = SSM_BLOCK_GROUPS * SSM_STATE
    wc = min(w, S5_COL_TILE)
    fixed = lambda a, b: pl.BlockSpec((1, a, b), lambda i, j: (i, 0, 0))
    cols = lambda a: pl.BlockSpec((1, a, wc), lambda i, j: (i, 0, j))
    return pl.pallas_call(
        functools.partial(_s5_core_kernel, nchunks=nchunks, rows=rows),
        grid=(nblk, w // wc),
        in_specs=[fixed(nr, w), cols(w), fixed(w, sl), fixed(w, sl), cols(sl), cols(sl),
                  fixed(1, sl), fixed(1, sl), fixed(rows, sl), fixed(rows, sl)],
        out_specs=[cols(nr), fixed(rows, sl), fixed(rows, sl)],
        out_shape=[jax.ShapeDtypeStruct((nblk, nr, w), F32),
                   jax.ShapeDtypeStruct((nblk, rows, sl), F32),
                   jax.ShapeDtypeStruct((nblk, rows, sl), F32)],
        scratch_shapes=[pltpu.VMEM((nr, sl), F32)] * 2,
        compiler_params=_params("parallel", "arbitrary"),
        name="s5_core",
    )(xb, m, p_re, p_im, n_re, n_im, lam_re, lam_im, s0_re, s0_im)


def _s5_out_kernel(x_ref, y_ref, g_ref, d_ref, w_ref, o_ref):
    x = x_ref[...]
    h = _rms(x, g_ref[...])
    z = _gelu(y_ref[...] + d_ref[...] * h)
    gg = jnp.dot(z.astype(BF16), w_ref[...], preferred_element_type=F32)
    o_ref[...] = x + gg[:, :D_MODEL] * _sigmoid(gg[:, D_MODEL:])


def _s5_out(x, y, g, d, w_glu):
    t = x.shape[0]
    tm = min(ROW_TILE, t)
    row = pl.BlockSpec((tm, D_MODEL), lambda i: (i, 0))
    vec = pl.BlockSpec((1, D_MODEL), lambda i: (0, 0))
    return pl.pallas_call(
        _s5_out_kernel,
        grid=(t // tm,),
        in_specs=[row, row, vec, vec, pl.BlockSpec((D_MODEL, 2 * D_MODEL), lambda i: (0, 0))],
        out_specs=row,
        out_shape=jax.ShapeDtypeStruct((t, D_MODEL), F32),
        compiler_params=_params("parallel"),
        name="s5_out",
    )(x, y, g.reshape(1, -1), d.reshape(1, -1), w_glu)


def _s5_mixer(x, s0_re, s0_im, g, folded, d_skip, w_glu, chunk):
    b, s, d = x.shape
    nk = s // chunk
    x2 = x.reshape(b * s, d)
    h = _norm_cast(x2, g)
    xb = (h.reshape(b, nk, chunk, SSM_BLOCKS, LANES).transpose(3, 1, 0, 2, 4)
          .reshape(SSM_BLOCKS, nk * b, chunk * LANES))
    sl = SSM_BLOCK_GROUPS * SSM_STATE
    to_block = lambda z: z.reshape(b, SSM_BLOCKS, sl).transpose(1, 0, 2)
    y, sr, si = _s5_core(xb, folded, to_block(s0_re), to_block(s0_im), nk, b)
    y = y.reshape(SSM_BLOCKS, nk, b, chunk, LANES).transpose(2, 1, 3, 0, 4).reshape(b * s, d)
    from_block = lambda z: z.transpose(1, 0, 2).reshape(b, SSM_GROUPS, SSM_STATE)
    out = _s5_out(x2, y, g, d_skip, w_glu)
    return out.reshape(b, s, d), from_block(sr), from_block(si)


CONV_HEAD = 8


def _conv_prompt_kernel(x_ref, g_ref, win_ref, wc_ref, wout_ref, o_ref, tail_ref, vbuf, *, tm):
    @pl.when(pl.program_id(1) == 0)
    def _():
        vbuf[0:CONV_HEAD, :] = jnp.zeros((CONV_HEAD, D_MODEL), F32)

    x = x_ref[0]
    h = _rms(x, g_ref[...])
    p = jnp.dot(h.astype(BF16), win_ref[...], preferred_element_type=F32)
    bg = p[:, :D_MODEL]
    v = p[:, D_MODEL:2 * D_MODEL] * p[:, 2 * D_MODEL:]
    vbuf[CONV_HEAD:CONV_HEAD + tm, :] = v
    v2 = vbuf[CONV_HEAD - 2:CONV_HEAD - 2 + tm, :]
    v1 = vbuf[CONV_HEAD - 1:CONV_HEAD - 1 + tm, :]
    conv = wc_ref[0:1, :] * v2 + wc_ref[1:2, :] * v1 + wc_ref[2:3, :] * v
    out = jnp.dot((bg * conv).astype(BF16), wout_ref[...], preferred_element_type=F32)
    o_ref[0] = x + out
    tail = vbuf[tm:tm + CONV_HEAD, :]
    vbuf[0:CONV_HEAD, :] = tail
    tail_ref[0] = tail


def _conv_prompt(x, g, w_in, w_conv, w_out):
    b, s, d = x.shape
    tm = min(ROW_TILE, s)
    return pl.pallas_call(
        functools.partial(_conv_prompt_kernel, tm=tm),
        grid=(b, s // tm),
        in_specs=[pl.BlockSpec((1, tm, d), lambda i, j: (i, j, 0)),
                  pl.BlockSpec((1, d), lambda i, j: (0, 0)),
                  pl.BlockSpec((d, 3 * d), lambda i, j: (0, 0)),
                  pl.BlockSpec((3, d), lambda i, j: (0, 0)),
                  pl.BlockSpec((d, d), lambda i, j: (0, 0))],
        out_specs=[pl.BlockSpec((1, tm, d), lambda i, j: (i, j, 0)),
                   pl.BlockSpec((1, CONV_HEAD, d), lambda i, j: (i, 0, 0))],
        out_shape=[jax.ShapeDtypeStruct((b, s, d), F32),
                   jax.ShapeDtypeStruct((b, CONV_HEAD, d), F32)],
        scratch_shapes=[pltpu.VMEM((CONV_HEAD + tm, d), F32)],
        compiler_params=_params("parallel", "arbitrary"),
        name="conv_prompt",
    )(x, g.reshape(1, d), w_in, w_conv, w_out)


def _conv_sample_kernel(x_ref, buf_ref, g_ref, win_ref, wc_ref, wout_ref, o_ref, nbuf_ref, *, steps, nb):
    x = x_ref[...]
    h = _rms(x, g_ref[...])
    p = jnp.dot(h.astype(BF16), win_ref[...], preferred_element_type=F32)
    bg = p[:, :D_MODEL]
    v = p[:, D_MODEL:2 * D_MODEL] * p[:, 2 * D_MODEL:]
    vp = [buf_ref[0:nb, :], buf_ref[nb:2 * nb, :]] + [v[t * nb:(t + 1) * nb, :] for t in range(steps)]
    conv = jnp.concatenate(
        [wc_ref[0:1, :] * vp[t] + wc_ref[1:2, :] * vp[t + 1] + wc_ref[2:3, :] * vp[t + 2]
         for t in range(steps)], axis=0)
    out = jnp.dot((bg * conv).astype(BF16), wout_ref[...], preferred_element_type=F32)
    o_ref[...] = x + out
    nbuf_ref[0:nb, :] = vp[-2]
    nbuf_ref[nb:2 * nb, :] = vp[-1]


def _conv_sample(x_tm, buf_tm, g, w_in, w_conv, w_out, steps, nb):
    d = D_MODEL
    full = lambda r, c: pl.BlockSpec((r, c), lambda i: (0, 0))
    return pl.pallas_call(
        functools.partial(_conv_sample_kernel, steps=steps, nb=nb),
        grid=(1,),
        in_specs=[full(steps * nb, d), full(2 * nb, d), full(1, d), full(d, 3 * d), full(3, d), full(d, d)],
        out_specs=[full(steps * nb, d), full(2 * nb, d)],
        out_shape=[jax.ShapeDtypeStruct((steps * nb, d), F32), jax.ShapeDtypeStruct((2 * nb, d), F32)],
        compiler_params=_params("arbitrary"),
        name="conv_sample",
    )(x_tm, buf_tm, g.reshape(1, d), w_in, w_conv, w_out)


def _softmax_rows(s):
    e = jnp.exp(s - jnp.max(s, axis=-1, keepdims=True))
    return e / jnp.sum(e, axis=-1, keepdims=True)


def _attn_heads(q, k, v):
    k = k.astype(BF16)
    v = v.astype(BF16)
    outs = []
    for hd in range(MEM_HEADS):
        sl = slice(hd * MEM_HEAD_DIM, (hd + 1) * MEM_HEAD_DIM)
        s = lax.dot_general(q[:, sl].astype(BF16), k[:, sl], _NT, preferred_element_type=F32)
        p = _softmax_rows(s * (MEM_HEAD_DIM ** -0.5))
        outs.append(jnp.dot(p.astype(BF16), v[:, sl], preferred_element_type=F32))
    return jnp.concatenate(outs, axis=-1)


def _attn_prompt_kernel(x_ref, g_ref, k_ref, v_ref, wq_ref, wo_ref, o_ref):
    x = x_ref[0]
    h = _rms(x, g_ref[...])
    q = jnp.dot(h.astype(BF16), wq_ref[...], preferred_element_type=F32)
    o = _attn_heads(q, k_ref[0], v_ref[0])
    o_ref[0] = x + jnp.dot(o.astype(BF16), wo_ref[...], preferred_element_type=F32)


def _attn_prompt(x, g, mem_k, mem_v, w_q, w_o):
    b, s, d = x.shape
    tm = min(ROW_TILE, s)
    mem = pl.BlockSpec((1, MEM_TOKENS, d), lambda i, j: (i, 0, 0))
    wgt = pl.BlockSpec((d, d), lambda i, j: (0, 0))
    row = pl.BlockSpec((1, tm, d), lambda i, j: (i, j, 0))
    return pl.pallas_call(
        _attn_prompt_kernel,
        grid=(b, s // tm),
        in_specs=[row, pl.BlockSpec((1, d), lambda i, j: (0, 0)), mem, mem, wgt, wgt],
        out_specs=row,
        out_shape=jax.ShapeDtypeStruct((b, s, d), F32),
        compiler_params=_params("parallel", "parallel"),
        name="attn_prompt",
    )(x, g.reshape(1, d), mem_k, mem_v, w_q, w_o)


SAMPLE_Q_ROWS = 8
SAMPLE_SEQ_BLOCK = 8


def _attn_sample_kernel(q_ref, k_ref, v_ref, o_ref):
    def body(i, carry):
        o_ref[i] = _attn_heads(q_ref[i], k_ref[i], v_ref[i])
        return carry

    lax.fori_loop(0, SAMPLE_SEQ_BLOCK, body, 0)


def _attn_sample(x, g, cache_k, cache_v, w_q, w_o):
    b, s, d = x.shape
    x2 = x.reshape(b * s, d)
    q = _mm(x2, w_q, g=g).reshape(b, s, d)
    q = jnp.pad(q, ((0, 0), (0, SAMPLE_Q_ROWS - s), (0, 0)))
    qblk = pl.BlockSpec((SAMPLE_SEQ_BLOCK, SAMPLE_Q_ROWS, d), lambda i: (i, 0, 0))
    mblk = pl.BlockSpec((SAMPLE_SEQ_BLOCK, MEM_TOKENS, d), lambda i: (i, 0, 0))
    o = pl.pallas_call(
        _attn_sample_kernel,
        grid=(b // SAMPLE_SEQ_BLOCK,),
        in_specs=[qblk, mblk, mblk],
        out_specs=qblk,
        out_shape=jax.ShapeDtypeStruct((b, SAMPLE_Q_ROWS, d), F32),
        compiler_params=_params("parallel"),
        name="attn_sample",
    )(q, cache_k, cache_v)
    o = o[:, :s].reshape(b * s, d)
    return _mm(o, w_o, res=x2).reshape(b, s, d)


def _top16(s):
    iota = lax.broadcasted_iota(jnp.int32, s.shape, 0)
    pos = jnp.full(s.shape, PEER_TOPK, jnp.int32)
    vals = []
    for i in range(PEER_TOPK):
        m = jnp.max(s, axis=0, keepdims=True)
        idx = jnp.min(jnp.where(s == m, iota, s.shape[0]), axis=0, keepdims=True)
        hit = iota == idx
        pos = jnp.where(hit, i, pos)
        s = jnp.where(hit, -jnp.inf, s)
        vals.append(m)
    return vals, pos


def _pair_select(vals1, vals2):
    lanes = vals1[0].shape[1]
    iota16 = lax.broadcasted_iota(jnp.int32, (PEER_TOPK, lanes), 0)
    v2 = jnp.zeros((PEER_TOPK, lanes), F32)
    for b in range(PEER_TOPK):
        v2 = jnp.where(iota16 == b, vals2[b], v2)
    iota8 = iota16[:8]
    cand = [vals1[0] + v2]
    flat = [iota16]
    for a in range(1, PEER_TOPK):
        cand.append(jnp.where(iota8 < PEER_TOPK // (a + 1), vals1[a] + v2[:8], -jnp.inf))
        flat.append(iota8 + a * PEER_TOPK)
    nsel = jnp.zeros((PEER_TOPK, lanes), jnp.int32)
    z = jnp.zeros((1, lanes), F32)
    top = vals1[0] + vals2[0]
    nflat = PEER_TOPK * PEER_TOPK
    for _ in range(PEER_TOPK):
        mx = jnp.maximum(cand[0][:8], cand[0][8:])
        for a in range(1, PEER_TOPK):
            mx = jnp.maximum(mx, cand[a])
        m = jnp.max(mx, axis=0, keepdims=True)
        w0 = jnp.where(cand[0] == m, flat[0], nflat)
        mi = jnp.minimum(w0[:8], w0[8:])
        for a in range(1, PEER_TOPK):
            mi = jnp.minimum(mi, jnp.where(cand[a] == m, flat[a], nflat))
        idx = jnp.min(mi, axis=0, keepdims=True)
        cand = [jnp.where(flat[a] == idx, -jnp.inf, cand[a]) for a in range(PEER_TOPK)]
        nsel = nsel + (iota16 == (idx >> 4)).astype(jnp.int32)
        z = z + jnp.exp(m - top)
    return nsel, z


def _peer_select_kernel(x_ref, g_ref, wq_ref, k1_ref, k2_ref,
                        pos2_ref, m2_ref, nsel_ref, m1_ref, q_scr, *, tm):
    h = _rms(x_ref[...], g_ref[...]).astype(BF16)
    q = jnp.dot(h, wq_ref[...], preferred_element_type=F32)
    for j in range(2 * PEER_HEADS):
        q_scr[j] = q[:, j * PEER_KEYS:(j + 1) * PEER_KEYS].astype(BF16)
    k1 = k1_ref[...]
    k2 = k2_ref[...]

    def head(hh, carry):
        s1f = lax.dot_general(k1, q_scr[2 * hh], _NT, preferred_element_type=F32)
        s2f = lax.dot_general(k2, q_scr[2 * hh + 1], _NT, preferred_element_type=F32)
        for c in range(tm // LANES):
            sl = slice(c * LANES, (c + 1) * LANES)
            s1, s2 = s1f[:, sl], s2f[:, sl]
            vals1, pos1 = _top16(s1)
            vals2, pos2 = _top16(s2)
            nsel, z = _pair_select(vals1, vals2)
            nfull = jnp.zeros(s1.shape, jnp.int32)
            for a in range(PEER_TOPK):
                nfull = jnp.where(pos1 == a, nsel[a:a + 1, :], nfull)
            pos2_ref[hh, :, sl] = pos2.astype(F32).astype(BF16)
            m2_ref[hh, :, sl] = jnp.exp(s2 - vals2[0]).astype(BF16)
            nsel_ref[hh, :, sl] = nfull.astype(F32)
            m1_ref[hh, :, sl] = jnp.exp(s1 - vals1[0]) * (1.0 / z)
        return carry

    lax.fori_loop(0, PEER_HEADS, head, 0)


PEER_SELECT_TILE = 256


def _peer_select(x, g, w_query, key1, key2):
    t, d = x.shape
    tm = PEER_SELECT_TILE
    nq = w_query.shape[1]
    head = pl.BlockSpec((PEER_HEADS, PEER_KEYS, tm), lambda i: (0, 0, i))
    keyspec = pl.BlockSpec((PEER_KEYS, PEER_KEYS), lambda i: (0, 0))
    shp = lambda dt: jax.ShapeDtypeStruct((PEER_HEADS, PEER_KEYS, t), dt)
    return pl.pallas_call(
        functools.partial(_peer_select_kernel, tm=tm),
        grid=(t // tm,),
        in_specs=[pl.BlockSpec((tm, d), lambda i: (i, 0)), pl.BlockSpec((1, d), lambda i: (0, 0)),
                  pl.BlockSpec((d, nq), lambda i: (0, 0)), keyspec, keyspec],
        out_specs=[head] * 4,
        out_shape=[shp(BF16), shp(BF16), shp(F32), shp(F32)],
        scratch_shapes=[pltpu.VMEM((2 * PEER_HEADS, tm, PEER_KEYS), BF16)],
        compiler_params=_params("parallel"),
        name="peer_select",
    )(x, g.reshape(1, d), w_query, key1, key2)


def _peer_dense_kernel(x_ref, g_ref, gf_ref, u_ref, vt_ref, pos2_ref, m2_ref, nsel_ref, m1_ref,
                       o_ref, h_scr, acc_scr, a_scr, z_scr, p2_scr, m2_scr, *, tm, te, final_norm):
    j = pl.program_id(1)

    @pl.when(j == 0)
    def _():
        h_scr[...] = _rms(x_ref[...], g_ref[...]).astype(BF16)
        acc_scr[...] = jnp.zeros(acc_scr.shape, F32)
        p2_scr[...] = pos2_ref[...]
        m2_scr[...] = m2_ref[...]

    a_scr[...] = lax.dot_general(u_ref[...], h_scr[...], _NT, preferred_element_type=F32)
    n1 = te // PEER_KEYS
    first = pl.multiple_of(j * n1, n1)
    for c in range(tm // LANES):
        sl = slice(c * LANES, (c + 1) * LANES)
        ns = [nsel_ref[hh, pl.ds(first, n1), sl].astype(BF16) for hh in range(PEER_HEADS)]
        mm = [m1_ref[hh, pl.ds(first, n1), sl].astype(BF16) for hh in range(PEER_HEADS)]
        for r in range(n1):
            rows = slice(r * PEER_KEYS, (r + 1) * PEER_KEYS)
            w = None
            for hh in range(PEER_HEADS):
                t = jnp.where(p2_scr[hh, :, sl] < ns[hh][r:r + 1, :],
                              m2_scr[hh, :, sl] * mm[hh][r:r + 1, :], 0)
                w = t if w is None else w + t
            z_scr[rows, sl] = _gelu(a_scr[rows, sl]).astype(BF16) * w
    acc_scr[...] += jnp.dot(vt_ref[...], z_scr[...], preferred_element_type=F32)

    @pl.when(j == pl.num_programs(1) - 1)
    def _():
        out = x_ref[...] + acc_scr[...].T
        if final_norm:
            out = _rms(out, gf_ref[...])
        o_ref[...] = out


def _peer_dense(x, g, g_final, u, vt, sel, final_norm):
    t, d = x.shape
    tm = min(ROW_TILE, t)
    te = EXPERT_TILE
    assert te // PEER_KEYS == SUBLANES
    row = pl.BlockSpec((tm, d), lambda i, j: (i, 0))
    vec = pl.BlockSpec((1, d), lambda i, j: (0, 0))
    head = pl.BlockSpec((PEER_HEADS, PEER_KEYS, tm), lambda i, j: (0, 0, i))
    return pl.pallas_call(
        functools.partial(_peer_dense_kernel, tm=tm, te=te, final_norm=final_norm),
        grid=(t // tm, PEER_EXPERTS // te),
        in_specs=[row, vec, vec, pl.BlockSpec((te, d), lambda i, j: (j, 0)),
                  pl.BlockSpec((d, te), lambda i, j: (0, j)), head, head, head, head],
        out_specs=row,
        out_shape=jax.ShapeDtypeStruct((t, d), F32),
        scratch_shapes=[pltpu.VMEM((tm, d), BF16), pltpu.VMEM((d, tm), F32),
                        pltpu.VMEM((te, tm), F32), pltpu.VMEM((te, tm), BF16),
                        pltpu.VMEM((PEER_HEADS, PEER_KEYS, tm), BF16),
                        pltpu.VMEM((PEER_HEADS, PEER_KEYS, tm), BF16)],
        compiler_params=_params("parallel", "arbitrary"),
        name="peer_dense",
    )(x, g.reshape(1, d), g_final.reshape(1, d), u, vt, *sel)


def _peer(x, g, g_final, w_query, key1, key2, u, vt, final_norm):
    shp = x.shape
    x2 = x.reshape(-1, shp[-1])
    sel = _peer_select(x2, g, w_query, key1, key2)
    return _peer_dense(x2, g, g_final, u, vt, sel, final_norm).reshape(shp)


def _trunk(x, ssm0, conv0, mem_k, mem_v, w, sample):
    b, s, d = x.shape
    chunk = s if sample else S5_CHUNK
    if ssm0 is None:
        ssm0 = (jnp.zeros((b, SSM_GROUPS, SSM_STATE), F32),) * 2
    x, ssm_re, ssm_im = _s5_mixer(x, ssm0[0], ssm0[1], w['norm_mix'][0], w['s5_fold'][chunk],
                                  w['ssm_d'][0], w['ssm_w_glu'][0], chunk)
    attn = _attn_sample if sample else _attn_prompt
    for i in range(2):
        if i == 1:
            if sample:
                x_tm = x.transpose(1, 0, 2).reshape(s * b, d)
                buf_tm = conv0.transpose(1, 0, 2).reshape(2 * b, d)
                x_tm, nbuf = _conv_sample(x_tm, buf_tm, w['norm_mix'][1], w['conv_w_in'][0],
                                          w['conv_w'][0], w['conv_w_out'][0], s, b)
                x = x_tm.reshape(s, b, d).transpose(1, 0, 2)
                conv_out = nbuf.reshape(2, b, d).transpose(1, 0, 2)
            else:
                x, tail = _conv_prompt(x, w['norm_mix'][1], w['conv_w_in'][0], w['conv_w'][0],
                                       w['conv_w_out'][0])
                conv_out = tail[:, CONV_HEAD - 2:]
        x = attn(x, w['norm_mem'][i], mem_k[i], mem_v[i], w['mem_w_q'][i], w['mem_w_o'][i])
        x = _peer(x, w['norm_ffn'][i], w['norm_final'], w['peer_w_query'][i], w['peer_key1'][i],
                  w['peer_key2'][i], w['peer_u'][i], w['peer_vt'][i], final_norm=(i == 1))
    return x, ssm_re[None], ssm_im[None], conv_out[None]


def kernel(x_prompt, x_sample, mem_prompt, state_ssm_re, state_ssm_im, state_conv, cache_mem_k, cache_mem_v, norm_mix, norm_mem, norm_ffn, norm_final, ssm_a_re, ssm_a_im, ssm_log_dt, ssm_b_re, ssm_b_im, ssm_c_re, ssm_c_im, ssm_d, ssm_w_glu, conv_w_in, conv_w, conv_w_out, mem_w_q, mem_w_k, mem_w_v, mem_w_o, peer_w_query, peer_key1, peer_key2, peer_u, peer_v):
    bsz, seq, d = x_prompt.shape
    dec_b, dec_s, _ = x_sample.shape
    depth = mem_w_q.shape[0]
    fold = lambda chunk: _s5_fold(ssm_a_re[0], ssm_a_im[0], ssm_log_dt[0], ssm_b_re[0], ssm_b_im[0],
                                  ssm_c_re[0], ssm_c_im[0], chunk)
    w = dict(
        norm_mix=norm_mix, norm_mem=norm_mem, norm_ffn=norm_ffn, norm_final=norm_final,
        s5_fold={S5_CHUNK: fold(S5_CHUNK), dec_s: fold(dec_s)},
        ssm_d=ssm_d, ssm_w_glu=ssm_w_glu.astype(BF16),
        conv_w_in=conv_w_in.astype(BF16), conv_w=conv_w, conv_w_out=conv_w_out.astype(BF16),
        mem_w_q=mem_w_q.astype(BF16), mem_w_o=mem_w_o.astype(BF16),
        peer_w_query=peer_w_query.astype(BF16), peer_key1=peer_key1.astype(BF16),
        peer_key2=peer_key2.astype(BF16), peer_u=peer_u.astype(BF16),
        peer_vt=peer_v.astype(BF16).transpose(0, 2, 1),
    )
    w_kv = jnp.concatenate([mem_w_k[i] for i in range(depth)] + [mem_w_v[i] for i in range(depth)],
                           axis=1).astype(BF16)
    kv = _mm(mem_prompt.reshape(bsz * MEM_TOKENS, d), w_kv, tm=256)
    kv = kv.reshape(bsz, MEM_TOKENS, 2 * depth, d).transpose(2, 0, 1, 3)
    mem_k_p, mem_v_p = kv[:depth], kv[depth:]

    y_p, re_p, im_p, conv_p = _trunk(x_prompt, None, None, mem_k_p, mem_v_p, w, sample=False)
    cache_k = cache_mem_k.astype(BF16).reshape(depth, dec_b, MEM_TOKENS, d)
    cache_v = cache_mem_v.astype(BF16).reshape(depth, dec_b, MEM_TOKENS, d)
    y_s, re_s, im_s, conv_s = _trunk(x_sample, (state_ssm_re[0], state_ssm_im[0]), state_conv[0],
                                     cache_k, cache_v, w, sample=True)
    head_shape = (depth, bsz, MEM_TOKENS, MEM_HEADS, MEM_HEAD_DIM)
    return (y_p, y_s, re_p, im_p, conv_p, mem_k_p.reshape(head_shape), mem_v_p.reshape(head_shape),
            re_s, im_s, conv_s)
```

```python
import functools
import math

import jax
import jax.numpy as jnp
from jax import lax
from jax.experimental import pallas as pl
from jax.experimental.pallas import tpu as pltpu

F32 = jnp.float32
BF16 = jnp.bfloat16

D_MODEL = 1024
SSM_GROUP = 16
SSM_GROUPS = D_MODEL // SSM_GROUP
SSM_STATE = 64
SSM_BLOCK_GROUPS = 8
SSM_BLOCKS = SSM_GROUPS // SSM_BLOCK_GROUPS
MEM_TOKENS = 256
MEM_HEADS = 4
MEM_HEAD_DIM = D_MODEL // MEM_HEADS
PEER_HEADS = 8
PEER_KEYS = 128
PEER_EXPERTS = PEER_KEYS * PEER_KEYS
PEER_TOPK = 16
RMS_EPS = 1e-6

LANES = 128
SUBLANES = 8
ROW_TILE = 512
S5_CHUNK = 16
S5_COL_TILE = 512
EXPERT_TILE = 1024
VMEM_LIMIT = 48 * 1024 * 1024

_NT = (((1,), (1,)), ((), ()))


def _params(*sem):
    return pltpu.CompilerParams(dimension_semantics=sem, vmem_limit_bytes=VMEM_LIMIT)


def _rms(x, g):
    r = lax.rsqrt(jnp.mean(x * x, axis=-1, keepdims=True) + RMS_EPS)
    return (x * r) * g


def _gelu(x):
    return 0.5 * x * (1.0 + lax.erf(x * (1.0 / math.sqrt(2.0))))


def _sigmoid(x):
    return 1.0 / (1.0 + jnp.exp(-x))


def _norm_cast_kernel(x_ref, g_ref, o_ref):
    o_ref[...] = _rms(x_ref[...], g_ref[...]).astype(BF16)


def _norm_cast(x, g):
    t, d = x.shape
    tm = min(ROW_TILE, t)
    return pl.pallas_call(
        _norm_cast_kernel,
        grid=(t // tm,),
        in_specs=[pl.BlockSpec((tm, d), lambda i: (i, 0)), pl.BlockSpec((1, d), lambda i: (0, 0))],
        out_specs=pl.BlockSpec((tm, d), lambda i: (i, 0)),
        out_shape=jax.ShapeDtypeStruct((t, d), BF16),
        compiler_params=_params("parallel"),
        name="norm_cast",
    )(x, g.reshape(1, d))


def _mm_kernel(*refs, has_norm, has_res):
    x_ref, w_ref = refs[0], refs[1]
    pos = 2
    x = x_ref[...]
    if has_norm:
        x = _rms(x, refs[pos][...])
        pos += 1
    y = jnp.dot(x.astype(BF16), w_ref[...], preferred_element_type=F32)
    if has_res:
        y = y + refs[pos][...]
        pos += 1
    refs[pos][...] = y


def _mm(x, w, g=None, res=None, tm=ROW_TILE):
    t, k = x.shape
    n = w.shape[1]
    tm = min(tm, t)
    args = [x, w]
    specs = [pl.BlockSpec((tm, k), lambda i: (i, 0)), pl.BlockSpec((k, n), lambda i: (0, 0))]
    if g is not None:
        args.append(g.reshape(1, k))
        specs.append(pl.BlockSpec((1, k), lambda i: (0, 0)))
    if res is not None:
        args.append(res)
        specs.append(pl.BlockSpec((tm, n), lambda i: (i, 0)))
    return pl.pallas_call(
        functools.partial(_mm_kernel, has_norm=g is not None, has_res=res is not None),
        grid=(t // tm,),
        in_specs=specs,
        out_specs=pl.BlockSpec((tm, n), lambda i: (i, 0)),
        out_shape=jax.ShapeDtypeStruct((t, n), F32),
        compiler_params=_params("parallel"),
        name="mm_rows",
    )(*args)


def _s5_fold(a_re, a_im, log_dt, b_re, b_im, c_re, c_im, chunk):
    hi = lax.Precision.HIGHEST
    dt = jnp.exp(log_dt)[:, None]
    mag = jnp.exp(a_re * dt)
    ang = a_im * dt
    lb_re = mag * jnp.cos(ang)
    lb_im = mag * jnp.sin(ang)
    den = a_re * a_re + a_im * a_im
    f_re = ((lb_re - 1.0) * a_re + lb_im * a_im) / den
    f_im = (lb_im * a_re - (lb_re - 1.0) * a_im) / den
    bb_re = f_re[..., None] * b_re - f_im[..., None] * b_im
    bb_im = f_re[..., None] * b_im + f_im[..., None] * b_re
    pw_re, pw_im = [jnp.ones_like(lb_re)], [jnp.zeros_like(lb_im)]
    for _ in range(chunk):
        pr, pi = pw_re[-1], pw_im[-1]
        pw_re.append(pr * lb_re - pi * lb_im)
        pw_im.append(pr * lb_im + pi * lb_re)
    pw_re, pw_im = jnp.stack(pw_re), jnp.stack(pw_im)
    cl_re = c_re[None] * pw_re[:, :, None, :] - c_im[None] * pw_im[:, :, None, :]
    cl_im = c_re[None] * pw_im[:, :, None, :] + c_im[None] * pw_re[:, :, None, :]
    kern = (jnp.einsum('kgdp,gpc->kgdc', cl_re[:chunk], bb_re, precision=hi)
            - jnp.einsum('kgdp,gpc->kgdc', cl_im[:chunk], bb_im, precision=hi))
    step = jnp.arange(chunk)
    rp_re, rp_im = pw_re[chunk - 1 - step], pw_im[chunk - 1 - step]
    p_re = rp_re[..., None] * bb_re[None] - rp_im[..., None] * bb_im[None]
    p_im = rp_re[..., None] * bb_im[None] + rp_im[..., None] * bb_re[None]
    p_re = p_re.transpose(1, 0, 3, 2)
    p_im = p_im.transpose(1, 0, 3, 2)
    n_re = cl_re[1:].transpose(1, 3, 0, 2)
    n_im = (-cl_im[1:]).transpose(1, 3, 0, 2)

    nb, q = SSM_BLOCKS, SSM_BLOCK_GROUPS
    eye = jnp.eye(q, dtype=BF16)
    w = chunk * q * SSM_GROUP
    sl = q * SSM_STATE
    kq = kern.transpose(1, 3, 0, 2).astype(BF16).reshape(nb, q, SSM_GROUP, chunk, 1, SSM_GROUP)
    kq = (kq * eye[None, :, None, None, :, None]).reshape(nb, q, SSM_GROUP, chunk, q * SSM_GROUP)
    kq = jnp.pad(kq, ((0, 0), (0, 0), (0, 0), (chunk - 1, 0), (0, 0)))
    m = jnp.stack([kq[:, :, :, chunk - 1 - t:2 * chunk - 1 - t] for t in range(chunk)], axis=1)
    m = m.reshape(nb, w, w)

    def fold_p(z):
        z = z.astype(BF16).reshape(nb, q, chunk, SSM_GROUP, 1, SSM_STATE).transpose(0, 2, 1, 3, 4, 5)
        return (z * eye[None, None, :, None, :, None]).reshape(nb, w, sl)

    def fold_n(z):
        z = z.astype(BF16).reshape(nb, q, SSM_STATE, chunk, 1, SSM_GROUP)
        return (z * eye[None, :, None, None, :, None]).reshape(nb, sl, w)

    lam_re = pw_re[chunk].reshape(nb, 1, sl)
    lam_im = pw_im[chunk].reshape(nb, 1, sl)
    return m, fold_p(p_re), fold_p(p_im), fold_n(n_re), fold_n(n_im), lam_re, lam_im


def _s5_core_kernel(x_ref, m_ref, pre_ref, pim_ref, nre_ref, nim_ref, lre_ref, lim_ref,
                    s0re_ref, s0im_ref, y_ref, sre_ref, sim_ref, stre, stim, *, nchunks, rows):
    x = x_ref[0]

    @pl.when(pl.program_id(1) == 0)
    def _():
        stre[...] = jnp.dot(x, pre_ref[0], preferred_element_type=F32)
        stim[...] = jnp.dot(x, pim_ref[0], preferred_element_type=F32)
        width = stre.shape[1]
        lr = jnp.broadcast_to(lre_ref[0], (rows, width))
        li = jnp.broadcast_to(lim_ref[0], (rows, width))

        def body(k, carry):
            sr, si = carry
            off = pl.multiple_of(k * rows, rows)
            qr = stre[pl.ds(off, rows), :]
            qi = stim[pl.ds(off, rows), :]
            stre[pl.ds(off, rows), :] = sr
            stim[pl.ds(off, rows), :] = si
            return lr * sr - li * si + qr, lr * si + li * sr + qi

        sr, si = lax.fori_loop(0, nchunks, body, (s0re_ref[0], s0im_ref[0]))
        sre_ref[0] = sr
        sim_ref[0] = si

    y = jnp.dot(x, m_ref[0], preferred_element_type=F32)
    y = y + jnp.dot(stre[...].astype(BF16), nre_ref[0], preferred_element_type=F32)
    y = y + jnp.dot(stim[...].astype(BF16), nim_ref[0], preferred_element_type=F32)
    y_ref[0] = y


def _s5_core(xb, folded, s0_re, s0_im, nchunks, rows):
    m, p_re, p_im, n_re, n_im, lam_re, lam_im = folded
    nblk, nr, w = xb.shape
    sl = SSM_BLOCK_GROUPS * SSM_STATE
    wc = min(w, S5_COL_TILE)
    fixed = lambda a, b: pl.BlockSpec((1, a, b), lambda i, j: (i, 0, 0))
    cols = lambda a: pl.BlockSpec((1, a, wc), lambda i, j: (i, 0, j))
    return pl.pallas_call(
        functools.partial(_s5_core_kernel, nchunks=nchunks, rows=rows),
        grid=(nblk, w // wc),
        in_specs=[fixed(nr, w), cols(w), fixed(w, sl), fixed(w, sl), cols(sl), cols(sl),
                  fixed(1, sl), fixed(1, sl), fixed(rows, sl), fixed(rows, sl)],
        out_specs=[cols(nr), fixed(rows, sl), fixed(rows, sl)],
        out_shape=[jax.ShapeDtypeStruct((nblk, nr, w), F32),
                   jax.ShapeDtypeStruct((nblk, rows, sl), F32),
                   jax.ShapeDtypeStruct((nblk, rows, sl), F32)],
        scratch_shapes=[pltpu.VMEM((nr, sl), F32)] * 2,
        compiler_params=_params("parallel", "arbitrary"),
        name="s5_core",
    )(xb, m, p_re, p_im, n_re, n_im, lam_re, lam_im, s0_re, s0_im)


def _s5_out_kernel(x_ref, y_ref, g_ref, d_ref, w_ref, o_ref):
    x = x_ref[...]
    h = _rms(x, g_ref[...])
    z = _gelu(y_ref[...] + d_ref[...] * h)
    gg = jnp.dot(z.astype(BF16), w_ref[...], preferred_element_type=F32)
    o_ref[...] = x + gg[:, :D_MODEL] * _sigmoid(gg[:, D_MODEL:])


def _s5_out(x, y, g, d, w_glu):
    t = x.shape[0]
    tm = min(ROW_TILE, t)
    row = pl.BlockSpec((tm, D_MODEL), lambda i: (i, 0))
    vec = pl.BlockSpec((1, D_MODEL), lambda i: (0, 0))
    return pl.pallas_call(
        _s5_out_kernel,
        grid=(t // tm,),
        in_specs=[row, row, vec, vec, pl.BlockSpec((D_MODEL, 2 * D_MODEL), lambda i: (0, 0))],
        out_specs=row,
        out_shape=jax.ShapeDtypeStruct((t, D_MODEL), F32),
        compiler_params=_params("parallel"),
        name="s5_out",
    )(x, y, g.reshape(1, -1), d.reshape(1, -1), w_glu)


def _s5_mixer(x, s0_re, s0_im, g, folded, d_skip, w_glu, chunk):
    b, s, d = x.shape
    nk = s // chunk
    x2 = x.reshape(b * s, d)
    h = _norm_cast(x2, g)
    xb = (h.reshape(b, nk, chunk, SSM_BLOCKS, LANES).transpose(3, 1, 0, 2, 4)
          .reshape(SSM_BLOCKS, nk * b, chunk * LANES))
    sl = SSM_BLOCK_GROUPS * SSM_STATE
    to_block = lambda z: z.reshape(b, SSM_BLOCKS, sl).transpose(1, 0, 2)
    y, sr, si = _s5_core(xb, folded, to_block(s0_re), to_block(s0_im), nk, b)
    y = y.reshape(SSM_BLOCKS, nk, b, chunk, LANES).transpose(2, 1, 3, 0, 4).reshape(b * s, d)
    from_block = lambda z: z.transpose(1, 0, 2).reshape(b, SSM_GROUPS, SSM_STATE)
    out = _s5_out(x2, y, g, d_skip, w_glu)
    return out.reshape(b, s, d), from_block(sr), from_block(si)


CONV_HEAD = 8


def _conv_prompt_kernel(x_ref, g_ref, win_ref, wc_ref, wout_ref, o_ref, tail_ref, vbuf, *, tm):
    @pl.when(pl.program_id(1) == 0)
    def _():
        vbuf[0:CONV_HEAD, :] = jnp.zeros((CONV_HEAD, D_MODEL), F32)

    x = x_ref[0]
    h = _rms(x, g_ref[...])
    p = jnp.dot(h.astype(BF16), win_ref[...], preferred_element_type=F32)
    bg = p[:, :D_MODEL]
    v = p[:, D_MODEL:2 * D_MODEL] * p[:, 2 * D_MODEL:]
    vbuf[CONV_HEAD:CONV_HEAD + tm, :] = v
    v2 = vbuf[CONV_HEAD - 2:CONV_HEAD - 2 + tm, :]
    v1 = vbuf[CONV_HEAD - 1:CONV_HEAD - 1 + tm, :]
    conv = wc_ref[0:1, :] * v2 + wc_ref[1:2, :] * v1 + wc_ref[2:3, :] * v
    out = jnp.dot((bg * conv).astype(BF16), wout_ref[...], preferred_element_type=F32)
    o_ref[0] = x + out
    tail = vbuf[tm:tm + CONV_HEAD, :]
    vbuf[0:CONV_HEAD, :] = tail
    tail_ref[0] = tail


def _conv_prompt(x, g, w_in, w_conv, w_out):
    b, s, d = x.shape
    tm = min(ROW_TILE, s)
    return pl.pallas_call(
        functools.partial(_conv_prompt_kernel, tm=tm),
        grid=(b, s // tm),
        in_specs=[pl.BlockSpec((1, tm, d), lambda i, j: (i, j, 0)),
                  pl.BlockSpec((1, d), lambda i, j: (0, 0)),
                  pl.BlockSpec((d, 3 * d), lambda i, j: (0, 0)),
                  pl.BlockSpec((3, d), lambda i, j: (0, 0)),
                  pl.BlockSpec((d, d), lambda i, j: (0, 0))],
        out_specs=[pl.BlockSpec((1, tm, d), lambda i, j: (i, j, 0)),
                   pl.BlockSpec((1, CONV_HEAD, d), lambda i, j: (i, 0, 0))],
        out_shape=[jax.ShapeDtypeStruct((b, s, d), F32),
                   jax.ShapeDtypeStruct((b, CONV_HEAD, d), F32)],
        scratch_shapes=[pltpu.VMEM((CONV_HEAD + tm, d), F32)],
        compiler_params=_params("parallel", "arbitrary"),
        name="conv_prompt",
    )(x, g.reshape(1, d), w_in, w_conv, w_out)


def _conv_sample_kernel(x_ref, buf_ref, g_ref, win_ref, wc_ref, wout_ref, o_ref, nbuf_ref, *, steps, nb):
    x = x_ref[...]
    h = _rms(x, g_ref[...])
    p = jnp.dot(h.astype(BF16), win_ref[...], preferred_element_type=F32)
    bg = p[:, :D_MODEL]
    v = p[:, D_MODEL:2 * D_MODEL] * p[:, 2 * D_MODEL:]
    vp = [buf_ref[0:nb, :], buf_ref[nb:2 * nb, :]] + [v[t * nb:(t + 1) * nb, :] for t in range(steps)]
    conv = jnp.concatenate(
        [wc_ref[0:1, :] * vp[t] + wc_ref[1:2, :] * vp[t + 1] + wc_ref[2:3, :] * vp[t + 2]
         for t in range(steps)], axis=0)
    out = jnp.dot((bg * conv).astype(BF16), wout_ref[...], preferred_element_type=F32)
    o_ref[...] = x + out
    nbuf_ref[0:nb, :] = vp[-2]
    nbuf_ref[nb:2 * nb, :] = vp[-1]


def _conv_sample(x_tm, buf_tm, g, w_in, w_conv, w_out, steps, nb):
    d = D_MODEL
    full = lambda r, c: pl.BlockSpec((r, c), lambda i: (0, 0))
    return pl.pallas_call(
        functools.partial(_conv_sample_kernel, steps=steps, nb=nb),
        grid=(1,),
        in_specs=[full(steps * nb, d), full(2 * nb, d), full(1, d), full(d, 3 * d), full(3, d), full(d, d)],
        out_specs=[full(steps * nb, d), full(2 * nb, d)],
        out_shape=[jax.ShapeDtypeStruct((steps * nb, d), F32), jax.ShapeDtypeStruct((2 * nb, d), F32)],
        compiler_params=_params("arbitrary"),
        name="conv_sample",
    )(x_tm, buf_tm, g.reshape(1, d), w_in, w_conv, w_out)


def _softmax_rows(s):
    e = jnp.exp(s - jnp.max(s, axis=-1, keepdims=True))
    return e / jnp.sum(e, axis=-1, keepdims=True)


def _attn_heads(q, k, v):
    k = k.astype(BF16)
    v = v.astype(BF16)
    outs = []
    for hd in range(MEM_HEADS):
        sl = slice(hd * MEM_HEAD_DIM, (hd + 1) * MEM_HEAD_DIM)
        s = lax.dot_general(q[:, sl].astype(BF16), k[:, sl], _NT, preferred_element_type=F32)
        p = _softmax_rows(s * (MEM_HEAD_DIM ** -0.5))
        outs.append(jnp.dot(p.astype(BF16), v[:, sl], preferred_element_type=F32))
    return jnp.concatenate(outs, axis=-1)


def _attn_prompt_kernel(x_ref, g_ref, k_ref, v_ref, wq_ref, wo_ref, o_ref):
    x = x_ref[0]
    h = _rms(x, g_ref[...])
    q = jnp.dot(h.astype(BF16), wq_ref[...], preferred_element_type=F32)
    o = _attn_heads(q, k_ref[0], v_ref[0])
    o_ref[0] = x + jnp.dot(o.astype(BF16), wo_ref[...], preferred_element_type=F32)


def _attn_prompt(x, g, mem_k, mem_v, layer, w_q, w_o):
    b, s, d = x.shape
    tm = min(ROW_TILE, s)
    mem = pl.BlockSpec((None, 1, MEM_TOKENS, d), lambda i, j: (layer, i, 0, 0))
    wgt = pl.BlockSpec((d, d), lambda i, j: (0, 0))
    row = pl.BlockSpec((1, tm, d), lambda i, j: (i, j, 0))
    return pl.pallas_call(
        _attn_prompt_kernel,
        grid=(b, s // tm),
        in_specs=[row, pl.BlockSpec((1, d), lambda i, j: (0, 0)), mem, mem, wgt, wgt],
        out_specs=row,
        out_shape=jax.ShapeDtypeStruct((b, s, d), F32),
        compiler_params=_params("parallel", "parallel"),
        name="attn_prompt",
    )(x, g.reshape(1, d), mem_k, mem_v, w_q, w_o)


SAMPLE_Q_ROWS = 8
SAMPLE_SEQ_BLOCK = 8


def _attn_sample_kernel(q_ref, k_ref, v_ref, o_ref):
    def body(i, carry):
        o_ref[i] = _attn_heads(q_ref[i], k_ref[i], v_ref[i])
        return carry

    lax.fori_loop(0, SAMPLE_SEQ_BLOCK, body, 0)


def _attn_sample(x, g, cache_k, cache_v, layer, w_q, w_o):
    b, s, d = x.shape
    x2 = x.reshape(b * s, d)
    q = _mm(x2, w_q, g=g).reshape(b, s, d)
    q = jnp.pad(q, ((0, 0), (0, SAMPLE_Q_ROWS - s), (0, 0)))
    qblk = pl.BlockSpec((SAMPLE_SEQ_BLOCK, SAMPLE_Q_ROWS, d), lambda i: (i, 0, 0))
    mblk = pl.BlockSpec((None, SAMPLE_SEQ_BLOCK, MEM_TOKENS, d), lambda i: (layer, i, 0, 0))
    o = pl.pallas_call(
        _attn_sample_kernel,
        grid=(b // SAMPLE_SEQ_BLOCK,),
        in_specs=[qblk, mblk, mblk],
        out_specs=qblk,
        out_shape=jax.ShapeDtypeStruct((b, SAMPLE_Q_ROWS, d), F32),
        compiler_params=_params("parallel"),
        name="attn_sample",
    )(q, cache_k, cache_v)
    o = o[:, :s].reshape(b * s, d)
    return _mm(o, w_o, res=x2).reshape(b, s, d)


def _top16(s):
    iota = lax.broadcasted_iota(jnp.int32, s.shape, 0)
    pos = jnp.full(s.shape, PEER_TOPK, jnp.int32)
    vals = []
    for i in range(PEER_TOPK):
        m = jnp.max(s, axis=0, keepdims=True)
        idx = jnp.min(jnp.where(s == m, iota, s.shape[0]), axis=0, keepdims=True)
        hit = iota == idx
        pos = jnp.where(hit, i, pos)
        s = jnp.where(hit, -jnp.inf, s)
        vals.append(m)
    return vals, pos


def _pair_select(vals1, vals2):
    lanes = vals1[0].shape[1]
    iota16 = lax.broadcasted_iota(jnp.int32, (PEER_TOPK, lanes), 0)
    v2 = jnp.zeros((PEER_TOPK, lanes), F32)
    for b in range(PEER_TOPK):
        v2 = jnp.where(iota16 == b, vals2[b], v2)
    iota8 = iota16[:8]
    cand = [vals1[0] + v2]
    flat = [iota16]
    for a in range(1, PEER_TOPK):
        cand.append(jnp.where(iota8 < PEER_TOPK // (a + 1), vals1[a] + v2[:8], -jnp.inf))
        flat.append(iota8 + a * PEER_TOPK)
    nsel = jnp.zeros((PEER_TOPK, lanes), jnp.int32)
    z = jnp.zeros((1, lanes), F32)
    top = vals1[0] + vals2[0]
    nflat = PEER_TOPK * PEER_TOPK
    for _ in range(PEER_TOPK):
        mx = jnp.maximum(cand[0][:8], cand[0][8:])
        for a in range(1, PEER_TOPK):
            mx = jnp.maximum(mx, cand[a])
        m = jnp.max(mx, axis=0, keepdims=True)
        w0 = jnp.where(cand[0] == m, flat[0], nflat)
        mi = jnp.minimum(w0[:8], w0[8:])
        for a in range(1, PEER_TOPK):
            mi = jnp.minimum(mi, jnp.where(cand[a] == m, flat[a], nflat))
        idx = jnp.min(mi, axis=0, keepdims=True)
        cand = [jnp.where(flat[a] == idx, -jnp.inf, cand[a]) for a in range(PEER_TOPK)]
        nsel = nsel + (iota16 == (idx >> 4)).astype(jnp.int32)
        z = z + jnp.exp(m - top)
    return nsel, z


def _peer_select_kernel(x_ref, g_ref, wq_ref, k1_ref, k2_ref,
                        pos2_ref, m2_ref, nsel_ref, m1_ref, q_scr, *, tm):
    h = _rms(x_ref[...], g_ref[...]).astype(BF16)
    q = jnp.dot(h, wq_ref[...], preferred_element_type=F32)
    for j in range(2 * PEER_HEADS):
        q_scr[j] = q[:, j * PEER_KEYS:(j + 1) * PEER_KEYS].astype(BF16)
    k1 = k1_ref[...]
    k2 = k2_ref[...]

    def head(hh, carry):
        s1f = lax.dot_general(k1, q_scr[2 * hh], _NT, preferred_element_type=F32)
        s2f = lax.dot_general(k2, q_scr[2 * hh + 1], _NT, preferred_element_type=F32)
        for c in range(tm // LANES):
            sl = slice(c * LANES, (c + 1) * LANES)
            s1, s2 = s1f[:, sl], s2f[:, sl]
            vals1, pos1 = _top16(s1)
            vals2, pos2 = _top16(s2)
            nsel, z = _pair_select(vals1, vals2)
            nfull = jnp.zeros(s1.shape, jnp.int32)
            for a in range(PEER_TOPK):
                nfull = jnp.where(pos1 == a, nsel[a:a + 1, :], nfull)
            pos2_ref[hh, :, sl] = pos2.astype(F32).astype(BF16)
            m2_ref[hh, :, sl] = jnp.exp(s2 - vals2[0]).astype(BF16)
            nsel_ref[hh, :, sl] = nfull.astype(F32)
            m1_ref[hh, :, sl] = jnp.exp(s1 - vals1[0]) * (1.0 / z)
        return carry

    lax.fori_loop(0, PEER_HEADS, head, 0)


PEER_SELECT_TILE = 256


def _peer_select(x, g, w_query, key1, key2):
    t, d = x.shape
    tm = PEER_SELECT_TILE
    nq = w_query.shape[1]
    head = pl.BlockSpec((PEER_HEADS, PEER_KEYS, tm), lambda i: (0, 0, i))
    keyspec = pl.BlockSpec((PEER_KEYS, PEER_KEYS), lambda i: (0, 0))
    shp = lambda dt: jax.ShapeDtypeStruct((PEER_HEADS, PEER_KEYS, t), dt)
    return pl.pallas_call(
        functools.partial(_peer_select_kernel, tm=tm),
        grid=(t // tm,),
        in_specs=[pl.BlockSpec((tm, d), lambda i: (i, 0)), pl.BlockSpec((1, d), lambda i: (0, 0)),
                  pl.BlockSpec((d, nq), lambda i: (0, 0)), keyspec, keyspec],
        out_specs=[head] * 4,
        out_shape=[shp(BF16), shp(BF16), shp(F32), shp(F32)],
        scratch_shapes=[pltpu.VMEM((2 * PEER_HEADS, tm, PEER_KEYS), BF16)],
        compiler_params=_params("parallel"),
        name="peer_select",
    )(x, g.reshape(1, d), w_query, key1, key2)


def _peer_dense_kernel(x_ref, g_ref, gf_ref, u_ref, vt_ref, pos2_ref, m2_ref, nsel_ref, m1_ref,
                       o_ref, h_scr, acc_scr, a_scr, z_scr, p2_scr, m2_scr, *, tm, te, final_norm):
    j = pl.program_id(1)

    @pl.when(j == 0)
    def _():
        h_scr[...] = _rms(x_ref[...], g_ref[...]).astype(BF16)
        acc_scr[...] = jnp.zeros(acc_scr.shape, F32)
        p2_scr[...] = pos2_ref[...]
        m2_scr[...] = m2_ref[...]

    a_scr[...] = lax.dot_general(u_ref[...], h_scr[...], _NT, preferred_element_type=F32)
    n1 = te // PEER_KEYS
    first = pl.multiple_of(j * n1, n1)
    for c in range(tm // LANES):
        sl = slice(c * LANES, (c + 1) * LANES)
        ns = [nsel_ref[hh, pl.ds(first, n1), sl].astype(BF16) for hh in range(PEER_HEADS)]
        mm = [m1_ref[hh, pl.ds(first, n1), sl].astype(BF16) for hh in range(PEER_HEADS)]
        for r in range(n1):
            rows = slice(r * PEER_KEYS, (r + 1) * PEER_KEYS)
            w = None
            for hh in range(PEER_HEADS):
                t = jnp.where(p2_scr[hh, :, sl] < ns[hh][r:r + 1, :],
                              m2_scr[hh, :, sl] * mm[hh][r:r + 1, :], 0)
                w = t if w is None else w + t
            z_scr[rows, sl] = _gelu(a_scr[rows, sl]).astype(BF16) * w
    acc_scr[...] += jnp.dot(vt_ref[...], z_scr[...], preferred_element_type=F32)

    @pl.when(j == pl.num_programs(1) - 1)
    def _():
        out = x_ref[...] + acc_scr[...].T
        if final_norm:
            out = _rms(out, gf_ref[...])
        o_ref[...] = out


def _peer_dense(x, g, g_final, u, vt, sel, final_norm):
    t, d = x.shape
    tm = min(ROW_TILE, t)
    te = EXPERT_TILE
    assert te // PEER_KEYS == SUBLANES
    row = pl.BlockSpec((tm, d), lambda i, j: (i, 0))
    vec = pl.BlockSpec((1, d), lambda i, j: (0, 0))
    head = pl.BlockSpec((PEER_HEADS, PEER_KEYS, tm), lambda i, j: (0, 0, i))
    return pl.pallas_call(
        functools.partial(_peer_dense_kernel, tm=tm, te=te, final_norm=final_norm),
        grid=(t // tm, PEER_EXPERTS // te),
        in_specs=[row, vec, vec, pl.BlockSpec((te, d), lambda i, j: (j, 0)),
                  pl.BlockSpec((d, te), lambda i, j: (0, j)), head, head, head, head],
        out_specs=row,
        out_shape=jax.ShapeDtypeStruct((t, d), F32),
        scratch_shapes=[pltpu.VMEM((tm, d), BF16), pltpu.VMEM((d, tm), F32),
                        pltpu.VMEM((te, tm), F32), pltpu.VMEM((te, tm), BF16),
                        pltpu.VMEM((PEER_HEADS, PEER_KEYS, tm), BF16),
                        pltpu.VMEM((PEER_HEADS, PEER_KEYS, tm), BF16)],
        compiler_params=_params("parallel", "arbitrary"),
        name="peer_dense",
    )(x, g.reshape(1, d), g_final.reshape(1, d), u, vt, *sel)


def _peer(x, g, g_final, w_query, key1, key2, u, vt, final_norm):
    shp = x.shape
    x2 = x.reshape(-1, shp[-1])
    sel = _peer_select(x2, g, w_query, key1, key2)
    return _peer_dense(x2, g, g_final, u, vt, sel, final_norm).reshape(shp)


def _trunk(x, ssm0, conv0, mem_k, mem_v, w, sample):
    b, s, d = x.shape
    chunk = s if sample else S5_CHUNK
    if ssm0 is None:
        ssm0 = (jnp.zeros((b, SSM_GROUPS, SSM_STATE), F32),) * 2
    x, ssm_re, ssm_im = _s5_mixer(x, ssm0[0], ssm0[1], w['norm_mix'][0], w['s5_fold'][chunk],
                                  w['ssm_d'][0], w['ssm_w_glu'][0], chunk)
    attn = _attn_sample if sample else _attn_prompt
    for i in range(2):
        if i == 1:
            if sample:
                x_tm = x.transpose(1, 0, 2).reshape(s * b, d)
                buf_tm = conv0.transpose(1, 0, 2).reshape(2 * b, d)
                x_tm, nbuf = _conv_sample(x_tm, buf_tm, w['norm_mix'][1], w['conv_w_in'][0],
                                          w['conv_w'][0], w['conv_w_out'][0], s, b)
                x = x_tm.reshape(s, b, d).transpose(1, 0, 2)
                conv_out = nbuf.reshape(2, b, d).transpose(1, 0, 2)
            else:
                x, tail = _conv_prompt(x, w['norm_mix'][1], w['conv_w_in'][0], w['conv_w'][0],
                                       w['conv_w_out'][0])
                conv_out = tail[:, CONV_HEAD - 2:]
        x = attn(x, w['norm_mem'][i], mem_k, mem_v, i, w['mem_w_q'][i], w['mem_w_o'][i])
        x = _peer(x, w['norm_ffn'][i], w['norm_final'], w['peer_w_query'][i], w['peer_key1'][i],
                  w['peer_key2'][i], w['peer_u'][i], w['peer_vt'][i], final_norm=(i == 1))
    return x, ssm_re[None], ssm_im[None], conv_out[None]


def kernel(x_prompt, x_sample, mem_prompt, state_ssm_re, state_ssm_im, state_conv, cache_mem_k, cache_mem_v, norm_mix, norm_mem, norm_ffn, norm_final, ssm_a_re, ssm_a_im, ssm_log_dt, ssm_b_re, ssm_b_im, ssm_c_re, ssm_c_im, ssm_d, ssm_w_glu, conv_w_in, conv_w, conv_w_out, mem_w_q, mem_w_k, mem_w_v, mem_w_o, peer_w_query, peer_key1, peer_key2, peer_u, peer_v):
    bsz, seq, d = x_prompt.shape
    dec_b, dec_s, _ = x_sample.shape
    depth = mem_w_q.shape[0]
    fold = lambda chunk: _s5_fold(ssm_a_re[0], ssm_a_im[0], ssm_log_dt[0], ssm_b_re[0], ssm_b_im[0],
                                  ssm_c_re[0], ssm_c_im[0], chunk)
    w = dict(
        norm_mix=norm_mix, norm_mem=norm_mem, norm_ffn=norm_ffn, norm_final=norm_final,
        s5_fold={S5_CHUNK: fold(S5_CHUNK), dec_s: fold(dec_s)},
        ssm_d=ssm_d, ssm_w_glu=ssm_w_glu.astype(BF16),
        conv_w_in=conv_w_in.astype(BF16), conv_w=conv_w, conv_w_out=conv_w_out.astype(BF16),
        mem_w_q=mem_w_q.astype(BF16), mem_w_o=mem_w_o.astype(BF16),
        peer_w_query=peer_w_query.astype(BF16), peer_key1=peer_key1.astype(BF16),
        peer_key2=peer_key2.astype(BF16), peer_u=peer_u.astype(BF16),
        peer_vt=peer_v.astype(BF16).transpose(0, 2, 1),
    )
    w_kv = jnp.concatenate([mem_w_k[i] for i in range(depth)] + [mem_w_v[i] for i in range(depth)],
                           axis=1).astype(BF16)
    kv = _mm(mem_prompt.reshape(bsz * MEM_TOKENS, d), w_kv, tm=256)
    kv = kv.reshape(bsz, MEM_TOKENS, 2 * depth, d).transpose(2, 0, 1, 3)
    mem_k_p, mem_v_p = kv[:depth], kv[depth:]

    y_p, re_p, im_p, conv_p = _trunk(x_prompt, None, None, mem_k_p, mem_v_p, w, sample=False)
    cache_k = cache_mem_k.astype(BF16).reshape(depth, dec_b, MEM_TOKENS, d)
    cache_v = cache_mem_v.astype(BF16).reshape(depth, dec_b, MEM_TOKENS, d)
    y_s, re_s, im_s, conv_s = _trunk(x_sample, (state_ssm_re[0], state_ssm_im[0]), state_conv[0],
                                     cache_k, cache_v, w, sample=True)
    head_shape = (depth, bsz, MEM_TOKENS, MEM_HEADS, MEM_HEAD_DIM)
    return (y_p, y_s, re_p, im_p, conv_p, mem_k_p.reshape(head_shape), mem_v_p.reshape(head_shape),
            re_s, im_s, conv_s)
```

```python
import functools
import math

import jax
import jax.numpy as jnp
from jax import lax
from jax.experimental import pallas as pl
from jax.experimental.pallas import tpu as pltpu

F32 = jnp.float32
BF16 = jnp.bfloat16

D_MODEL = 1024
SSM_GROUP = 16
SSM_GROUPS = D_MODEL // SSM_GROUP
SSM_STATE = 64
SSM_BLOCK_GROUPS = 8
SSM_BLOCKS = SSM_GROUPS // SSM_BLOCK_GROUPS
MEM_TOKENS = 256
MEM_HEADS = 4
MEM_HEAD_DIM = D_MODEL // MEM_HEADS
PEER_HEADS = 8
PEER_KEYS = 128
PEER_EXPERTS = PEER_KEYS * PEER_KEYS
PEER_TOPK = 16
RMS_EPS = 1e-6

LANES = 128
SUBLANES = 8
ROW_TILE = 512
S5_CHUNK = 16
S5_COL_TILE = 512
EXPERT_TILE = 1024
VMEM_LIMIT = 48 * 1024 * 1024

_NT = (((1,), (1,)), ((), ()))


def _params(*sem):
    return pltpu.CompilerParams(dimension_semantics=sem, vmem_limit_bytes=VMEM_LIMIT)


def _rms(x, g):
    r = lax.rsqrt(jnp.mean(x * x, axis=-1, keepdims=True) + RMS_EPS)
    return (x * r) * g


def _gelu(x):
    return 0.5 * x * (1.0 + lax.erf(x * (1.0 / math.sqrt(2.0))))


def _sigmoid(x):
    return 1.0 / (1.0 + jnp.exp(-x))


def _norm_cast_kernel(x_ref, g_ref, o_ref):
    o_ref[...] = _rms(x_ref[...], g_ref[...]).astype(BF16)


def _norm_cast(x, g):
    t, d = x.shape
    tm = min(ROW_TILE, t)
    return pl.pallas_call(
        _norm_cast_kernel,
        grid=(t // tm,),
        in_specs=[pl.BlockSpec((tm, d), lambda i: (i, 0)), pl.BlockSpec((1, d), lambda i: (0, 0))],
        out_specs=pl.BlockSpec((tm, d), lambda i: (i, 0)),
        out_shape=jax.ShapeDtypeStruct((t, d), BF16),
        compiler_params=_params("parallel"),
        name="norm_cast",
    )(x, g.reshape(1, d))


def _mm_kernel(*refs, has_norm, has_res):
    x_ref, w_ref = refs[0], refs[1]
    pos = 2
    x = x_ref[...]
    if has_norm:
        x = _rms(x, refs[pos][...])
        pos += 1
    y = jnp.dot(x.astype(BF16), w_ref[...], preferred_element_type=F32)
    if has_res:
        y = y + refs[pos][...]
        pos += 1
    refs[pos][...] = y


def _mm(x, w, g=None, res=None, tm=ROW_TILE):
    t, k = x.shape
    n = w.shape[1]
    tm = min(tm, t)
    args = [x, w]
    specs = [pl.BlockSpec((tm, k), lambda i: (i, 0)), pl.BlockSpec((k, n), lambda i: (0, 0))]
    if g is not None:
        args.append(g.reshape(1, k))
        specs.append(pl.BlockSpec((1, k), lambda i: (0, 0)))
    if res is not None:
        args.append(res)
        specs.append(pl.BlockSpec((tm, n), lambda i: (i, 0)))
    return pl.pallas_call(
        functools.partial(_mm_kernel, has_norm=g is not None, has_res=res is not None),
        grid=(t // tm,),
        in_specs=specs,
        out_specs=pl.BlockSpec((tm, n), lambda i: (i, 0)),
        out_shape=jax.ShapeDtypeStruct((t, n), F32),
        compiler_params=_params("parallel"),
        name="mm_rows",
    )(*args)


def _s5_fold(a_re, a_im, log_dt, b_re, b_im, c_re, c_im, chunk):
    hi = lax.Precision.HIGHEST
    dt = jnp.exp(log_dt)[:, None]
    mag = jnp.exp(a_re * dt)
    ang = a_im * dt
    lb_re = mag * jnp.cos(ang)
    lb_im = mag * jnp.sin(ang)
    den = a_re * a_re + a_im * a_im
    f_re = ((lb_re - 1.0) * a_re + lb_im * a_im) / den
    f_im = (lb_im * a_re - (lb_re - 1.0) * a_im) / den
    bb_re = f_re[..., None] * b_re - f_im[..., None] * b_im
    bb_im = f_re[..., None] * b_im + f_im[..., None] * b_re
    pw_re, pw_im = [jnp.ones_like(lb_re)], [jnp.zeros_like(lb_im)]
    for _ in range(chunk):
        pr, pi = pw_re[-1], pw_im[-1]
        pw_re.append(pr * lb_re - pi * lb_im)
        pw_im.append(pr * lb_im + pi * lb_re)
    pw_re, pw_im = jnp.stack(pw_re), jnp.stack(pw_im)
    cl_re = c_re[None] * pw_re[:, :, None, :] - c_im[None] * pw_im[:, :, None, :]
    cl_im = c_re[None] * pw_im[:, :, None, :] + c_im[None] * pw_re[:, :, None, :]
    kern = (jnp.einsum('kgdp,gpc->kgdc', cl_re[:chunk], bb_re, precision=hi)
            - jnp.einsum('kgdp,gpc->kgdc', cl_im[:chunk], bb_im, precision=hi))
    step = jnp.arange(chunk)
    rp_re, rp_im = pw_re[chunk - 1 - step], pw_im[chunk - 1 - step]
    p_re = rp_re[..., None] * bb_re[None] - rp_im[..., None] * bb_im[None]
    p_im = rp_re[..., None] * bb_im[None] + rp_im[..., None] * bb_re[None]
    p_re = p_re.transpose(1, 0, 3, 2)
    p_im = p_im.transpose(1, 0, 3, 2)
    n_re = cl_re[1:].transpose(1, 3, 0, 2)
    n_im = (-cl_im[1:]).transpose(1, 3, 0, 2)

    nb, q = SSM_BLOCKS, SSM_BLOCK_GROUPS
    eye = jnp.eye(q, dtype=BF16)
    w = chunk * q * SSM_GROUP
    sl = q * SSM_STATE
    kq = kern.transpose(1, 3, 0, 2).astype(BF16).reshape(nb, q, SSM_GROUP, chunk, 1, SSM_GROUP)
    kq = (kq * eye[None, :, None, None, :, None]).reshape(nb, q, SSM_GROUP, chunk, q * SSM_GROUP)
    kq = jnp.pad(kq, ((0, 0), (0, 0), (0, 0), (chunk - 1, 0), (0, 0)))
    m = jnp.stack([kq[:, :, :, chunk - 1 - t:2 * chunk - 1 - t] for t in range(chunk)], axis=1)
    m = m.reshape(nb, w, w)

    def fold_p(z):
        z = z.astype(BF16).reshape(nb, q, chunk, SSM_GROUP, 1, SSM_STATE).transpose(0, 2, 1, 3, 4, 5)
        return (z * eye[None, None, :, None, :, None]).reshape(nb, w, sl)

    def fold_n(z):
        z = z.astype(BF16).reshape(nb, q, SSM_STATE, chunk, 1, SSM_GROUP)
        return (z * eye[None, :, None, None, :, None]).reshape(nb, sl, w)

    lam_re = pw_re[chunk].reshape(nb, 1, sl)
    lam_im = pw_im[chunk].reshape(nb, 1, sl)
    return m, fold_p(p_re), fold_p(p_im), fold_n(n_re), fold_n(n_im), lam_re, lam_im


def _s5_core_kernel(x_ref, m_ref, pre_ref, pim_ref, nre_ref, nim_ref, lre_ref, lim_ref,
                    s0re_ref, s0im_ref, y_ref, sre_ref, sim_ref, stre, stim, *, nchunks, rows):
    x = x_ref[0]

    @pl.when(pl.program_id(1) == 0)
    def _():
        stre[...] = jnp.dot(x, pre_ref[0], preferred_element_type=F32)
        stim[...] = jnp.dot(x, pim_ref[0], preferred_element_type=F32)
        width = stre.shape[1]
        lr = jnp.broadcast_to(lre_ref[0], (rows, width))
        li = jnp.broadcast_to(lim_ref[0], (rows, width))

        def body(k, carry):
            sr, si = carry
            off = pl.multiple_of(k * rows, rows)
            qr = stre[pl.ds(off, rows), :]
            qi = stim[pl.ds(off, rows), :]
            stre[pl.ds(off, rows), :] = sr
            stim[pl.ds(off, rows), :] = si
            return lr * sr - li * si + qr, lr * si + li * sr + qi

        sr, si = lax.fori_loop(0, nchunks, body, (s0re_ref[0], s0im_ref[0]))
        sre_ref[0] = sr
        sim_ref[0] = si

    y = jnp.dot(x, m_ref[0], preferred_element_type=F32)
    y = y + jnp.dot(stre[...].astype(BF16), nre_ref[0], preferred_element_type=F32)
    y = y + jnp.dot(stim[...].astype(BF16), nim_ref[0], preferred_element_type=F32)
    y_ref[0] = y


def _s5_core(xb, folded, s0_re, s0_im, nchunks, rows):
    m, p_re, p_im, n_re, n_im, lam_re, lam_im = folded
    nblk, nr, w = xb.shape
    sl = SSM_BLOCK_GROUPS * SSM_STATE
    wc = min(w, S5_COL_TILE)
    fixed = lambda a, b: pl.BlockSpec((1, a, b), lambda i, j: (i, 0, 0))
    cols = lambda a: pl.BlockSpec((1, a, wc), lambda i, j: (i, 0, j))
    return pl.pallas_call(
        functools.partial(_s5_core_kernel, nchunks=nchunks, rows=rows),
        grid=(nblk, w // wc),
        in_specs=[fixed(nr, w), cols(w), fixed(w, sl), fixed(w, sl), cols(sl), cols(sl),
                  fixed(1, sl), fixed(1, sl), fixed(rows, sl), fixed(rows, sl)],
        out_specs=[cols(nr), fixed(rows, sl), fixed(rows, sl)],
        out_shape=[jax.ShapeDtypeStruct((nblk, nr, w), F32),
                   jax.ShapeDtypeStruct((nblk, rows, sl), F32),
                   jax.ShapeDtypeStruct((nblk, rows, sl), F32)],
        scratch_shapes=[pltpu.VMEM((nr, sl), F32)] * 2,
        compiler_params=_params("parallel", "arbitrary"),
        name="s5_core",
    )(xb, m, p_re, p_im, n_re, n_im, lam_re, lam_im, s0_re, s0_im)


def _s5_out_kernel(x_ref, y_ref, g_ref, d_ref, w_ref, o_ref):
    x = x_ref[...]
    h = _rms(x, g_ref[...])
    z = _gelu(y_ref[...] + d_ref[...] * h)
    gg = jnp.dot(z.astype(BF16), w_ref[...], preferred_element_type=F32)
    o_ref[...] = x + gg[:, :D_MODEL] * _sigmoid(gg[:, D_MODEL:])


def _s5_out(x, y, g, d, w_glu):
    t = x.shape[0]
    tm = min(ROW_TILE, t)
    row = pl.BlockSpec((tm, D_MODEL), lambda i: (i, 0))
    vec = pl.BlockSpec((1, D_MODEL), lambda i: (0, 0))
    return pl.pallas_call(
        _s5_out_kernel,
        grid=(t // tm,),
        in_specs=[row, row, vec, vec, pl.BlockSpec((D_MODEL, 2 * D_MODEL), lambda i: (0, 0))],
        out_specs=row,
        out_shape=jax.ShapeDtypeStruct((t, D_MODEL), F32),
        compiler_params=_params("parallel"),
        name="s5_out",
    )(x, y, g.reshape(1, -1), d.reshape(1, -1), w_glu)


def _s5_mixer(x, s0_re, s0_im, g, folded, d_skip, w_glu, chunk):
    b, s, d = x.shape
    nk = s // chunk
    x2 = x.reshape(b * s, d)
    h = _norm_cast(x2, g)
    xb = (h.reshape(b, nk, chunk, SSM_BLOCKS, LANES).transpose(3, 1, 0, 2, 4)
          .reshape(SSM_BLOCKS, nk * b, chunk * LANES))
    sl = SSM_BLOCK_GROUPS * SSM_STATE
    to_block = lambda z: z.reshape(b, SSM_BLOCKS, sl).transpose(1, 0, 2)
    y, sr, si = _s5_core(xb, folded, to_block(s0_re), to_block(s0_im), nk, b)
    y = y.reshape(SSM_BLOCKS, nk, b, chunk, LANES).transpose(2, 1, 3, 0, 4).reshape(b * s, d)
    from_block = lambda z: z.transpose(1, 0, 2).reshape(b, SSM_GROUPS, SSM_STATE)
    out = _s5_out(x2, y, g, d_skip, w_glu)
    return out.reshape(b, s, d), from_block(sr), from_block(si)


CONV_HEAD = 8


def _conv_prompt_kernel(x_ref, g_ref, win_ref, wc_ref, wout_ref, o_ref, tail_ref, vbuf, *, tm):
    @pl.when(pl.program_id(1) == 0)
    def _():
        vbuf[0:CONV_HEAD, :] = jnp.zeros((CONV_HEAD, D_MODEL), F32)

    x = x_ref[0]
    h = _rms(x, g_ref[...])
    p = jnp.dot(h.astype(BF16), win_ref[...], preferred_element_type=F32)
    bg = p[:, :D_MODEL]
    v = p[:, D_MODEL:2 * D_MODEL] * p[:, 2 * D_MODEL:]
    vbuf[CONV_HEAD:CONV_HEAD + tm, :] = v
    v2 = vbuf[CONV_HEAD - 2:CONV_HEAD - 2 + tm, :]
    v1 = vbuf[CONV_HEAD - 1:CONV_HEAD - 1 + tm, :]
    conv = wc_ref[0:1, :] * v2 + wc_ref[1:2, :] * v1 + wc_ref[2:3, :] * v
    out = jnp.dot((bg * conv).astype(BF16), wout_ref[...], preferred_element_type=F32)
    o_ref[0] = x + out
    tail = vbuf[tm:tm + CONV_HEAD, :]
    vbuf[0:CONV_HEAD, :] = tail
    tail_ref[0] = tail


def _conv_prompt(x, g, w_in, w_conv, w_out):
    b, s, d = x.shape
    tm = min(ROW_TILE, s)
    return pl.pallas_call(
        functools.partial(_conv_prompt_kernel, tm=tm),
        grid=(b, s // tm),
        in_specs=[pl.BlockSpec((1, tm, d), lambda i, j: (i, j, 0)),
                  pl.BlockSpec((1, d), lambda i, j: (0, 0)),
                  pl.BlockSpec((d, 3 * d), lambda i, j: (0, 0)),
                  pl.BlockSpec((3, d), lambda i, j: (0, 0)),
                  pl.BlockSpec((d, d), lambda i, j: (0, 0))],
        out_specs=[pl.BlockSpec((1, tm, d), lambda i, j: (i, j, 0)),
                   pl.BlockSpec((1, CONV_HEAD, d), lambda i, j: (i, 0, 0))],
        out_shape=[jax.ShapeDtypeStruct((b, s, d), F32),
                   jax.ShapeDtypeStruct((b, CONV_HEAD, d), F32)],
        scratch_shapes=[pltpu.VMEM((CONV_HEAD + tm, d), F32)],
        compiler_params=_params("parallel", "arbitrary"),
        name="conv_prompt",
    )(x, g.reshape(1, d), w_in, w_conv, w_out)


def _conv_sample_kernel(x_ref, buf_ref, g_ref, win_ref, wc_ref, wout_ref, o_ref, nbuf_ref, *, steps, nb):
    x = x_ref[...]
    h = _rms(x, g_ref[...])
    p = jnp.dot(h.astype(BF16), win_ref[...], preferred_element_type=F32)
    bg = p[:, :D_MODEL]
    v = p[:, D_MODEL:2 * D_MODEL] * p[:, 2 * D_MODEL:]
    vp = [buf_ref[0:nb, :], buf_ref[nb:2 * nb, :]] + [v[t * nb:(t + 1) * nb, :] for t in range(steps)]
    conv = jnp.concatenate(
        [wc_ref[0:1, :] * vp[t] + wc_ref[1:2, :] * vp[t + 1] + wc_ref[2:3, :] * vp[t + 2]
         for t in range(steps)], axis=0)
    out = jnp.dot((bg * conv).astype(BF16), wout_ref[...], preferred_element_type=F32)
    o_ref[...] = x + out
    nbuf_ref[0:nb, :] = vp[-2]
    nbuf_ref[nb:2 * nb, :] = vp[-1]


def _conv_sample(x_tm, buf_tm, g, w_in, w_conv, w_out, steps, nb):
    d = D_MODEL
    full = lambda r, c: pl.BlockSpec((r, c), lambda i: (0, 0))
    return pl.pallas_call(
        functools.partial(_conv_sample_kernel, steps=steps, nb=nb),
        grid=(1,),
        in_specs=[full(steps * nb, d), full(2 * nb, d), full(1, d), full(d, 3 * d), full(3, d), full(d, d)],
        out_specs=[full(steps * nb, d), full(2 * nb, d)],
        out_shape=[jax.ShapeDtypeStruct((steps * nb, d), F32), jax.ShapeDtypeStruct((2 * nb, d), F32)],
        compiler_params=_params("arbitrary"),
        name="conv_sample",
    )(x_tm, buf_tm, g.reshape(1, d), w_in, w_conv, w_out)


def _softmax_rows(s):
    e = jnp.exp(s - jnp.max(s, axis=-1, keepdims=True))
    return e / jnp.sum(e, axis=-1, keepdims=True)


def _attn_heads(q, k, v):
    k = k.astype(BF16)
    v = v.astype(BF16)
    outs = []
    for hd in range(MEM_HEADS):
        sl = slice(hd * MEM_HEAD_DIM, (hd + 1) * MEM_HEAD_DIM)
        s = lax.dot_general(q[:, sl].astype(BF16), k[:, sl], _NT, preferred_element_type=F32)
        p = _softmax_rows(s * (MEM_HEAD_DIM ** -0.5))
        outs.append(jnp.dot(p.astype(BF16), v[:, sl], preferred_element_type=F32))
    return jnp.concatenate(outs, axis=-1)


def _attn_prompt_kernel(x_ref, g_ref, k_ref, v_ref, wq_ref, wo_ref, o_ref):
    x = x_ref[0]
    h = _rms(x, g_ref[...])
    q = jnp.dot(h.astype(BF16), wq_ref[...], preferred_element_type=F32)
    o = _attn_heads(q, k_ref[0], v_ref[0])
    o_ref[0] = x + jnp.dot(o.astype(BF16), wo_ref[...], preferred_element_type=F32)


def _attn_prompt(x, g, mem_k, mem_v, layer, w_q, w_o):
    b, s, d = x.shape
    tm = min(ROW_TILE, s)
    mem = pl.BlockSpec((None, 1, MEM_TOKENS, d), lambda i, j: (layer, i, 0, 0))
    wgt = pl.BlockSpec((d, d), lambda i, j: (0, 0))
    row = pl.BlockSpec((1, tm, d), lambda i, j: (i, j, 0))
    return pl.pallas_call(
        _attn_prompt_kernel,
        grid=(b, s // tm),
        in_specs=[row, pl.BlockSpec((1, d), lambda i, j: (0, 0)), mem, mem, wgt, wgt],
        out_specs=row,
        out_shape=jax.ShapeDtypeStruct((b, s, d), F32),
        compiler_params=_params("parallel", "parallel"),
        name="attn_prompt",
    )(x, g.reshape(1, d), mem_k, mem_v, w_q, w_o)


SAMPLE_Q_ROWS = 8
SAMPLE_SEQ_BLOCK = 8


def _attn_sample_kernel(q_ref, k_ref, v_ref, o_ref):
    def body(i, carry):
        o_ref[i] = _attn_heads(q_ref[i], k_ref[i], v_ref[i])
        return carry

    lax.fori_loop(0, SAMPLE_SEQ_BLOCK, body, 0)


def _attn_sample(x, g, cache_k, cache_v, layer, w_q, w_o):
    b, s, d = x.shape
    x2 = x.reshape(b * s, d)
    q = _mm(x2, w_q, g=g).reshape(b, s, d)
    q = jnp.pad(q, ((0, 0), (0, SAMPLE_Q_ROWS - s), (0, 0)))
    qblk = pl.BlockSpec((SAMPLE_SEQ_BLOCK, SAMPLE_Q_ROWS, d), lambda i: (i, 0, 0))
    mblk = pl.BlockSpec((None, SAMPLE_SEQ_BLOCK, MEM_TOKENS, d), lambda i: (layer, i, 0, 0))
    o = pl.pallas_call(
        _attn_sample_kernel,
        grid=(b // SAMPLE_SEQ_BLOCK,),
        in_specs=[qblk, mblk, mblk],
        out_specs=qblk,
        out_shape=jax.ShapeDtypeStruct((b, SAMPLE_Q_ROWS, d), F32),
        compiler_params=_params("parallel"),
        name="attn_sample",
    )(q, cache_k, cache_v)
    o = o[:, :s].reshape(b * s, d)
    return _mm(o, w_o, res=x2).reshape(b, s, d)


def _top16(s):
    iota = lax.broadcasted_iota(jnp.int32, s.shape, 0)
    pos = jnp.full(s.shape, PEER_TOPK, jnp.int32)
    vals = []
    for i in range(PEER_TOPK):
        m = jnp.max(s, axis=0, keepdims=True)
        idx = jnp.min(jnp.where(s == m, iota, s.shape[0]), axis=0, keepdims=True)
        hit = iota == idx
        pos = jnp.where(hit, i, pos)
        s = jnp.where(hit, -jnp.inf, s)
        vals.append(m)
    return vals, pos


def _pair_select(vals1, vals2):
    lanes = vals1[0].shape[1]
    iota16 = lax.broadcasted_iota(jnp.int32, (PEER_TOPK, lanes), 0)
    v2 = jnp.zeros((PEER_TOPK, lanes), F32)
    for b in range(PEER_TOPK):
        v2 = jnp.where(iota16 == b, vals2[b], v2)
    iota8 = iota16[:8]
    cand = [vals1[0] + v2]
    flat = [iota16]
    for a in range(1, PEER_TOPK):
        cand.append(jnp.where(iota8 < PEER_TOPK // (a + 1), vals1[a] + v2[:8], -jnp.inf))
        flat.append(iota8 + a * PEER_TOPK)
    nsel = jnp.zeros((PEER_TOPK, lanes), jnp.int32)
    z = jnp.zeros((1, lanes), F32)
    top = vals1[0] + vals2[0]
    nflat = PEER_TOPK * PEER_TOPK
    for _ in range(PEER_TOPK):
        mx = jnp.maximum(cand[0][:8], cand[0][8:])
        for a in range(1, PEER_TOPK):
            mx = jnp.maximum(mx, cand[a])
        m = jnp.max(mx, axis=0, keepdims=True)
        w0 = jnp.where(cand[0] == m, flat[0], nflat)
        mi = jnp.minimum(w0[:8], w0[8:])
        for a in range(1, PEER_TOPK):
            mi = jnp.minimum(mi, jnp.where(cand[a] == m, flat[a], nflat))
        idx = jnp.min(mi, axis=0, keepdims=True)
        cand = [jnp.where(flat[a] == idx, -jnp.inf, cand[a]) for a in range(PEER_TOPK)]
        nsel = nsel + (iota16 == (idx >> 4)).astype(jnp.int32)
        z = z + jnp.exp(m - top)
    return nsel, z


def _sort_network(n):
    pairs = []

    def merge(lo, cnt, r):
        m = r * 2
        if m < cnt:
            merge(lo, cnt, m)
            merge(lo + r, cnt, m)
            for i in range(lo + r, lo + cnt - r, m):
                pairs.append((i, i + r))
        else:
            pairs.append((lo, lo + r))

    def sort(lo, cnt):
        if cnt > 1:
            m = cnt // 2
            sort(lo, m)
            sort(lo + m, m)
            merge(lo, cnt, 1)

    sort(0, n)
    return tuple(pairs)


_SORT16 = _sort_network(PEER_TOPK)


def _exchange(v, i, j):
    v[i], v[j] = jnp.maximum(v[i], v[j]), jnp.minimum(v[i], v[j])


def _allreduce_rows(x, op):
    for shift in (4, 2, 1):
        x = op(x, pltpu.roll(x, shift, 0))
    return x


def _sorted_top16(s):
    v = [s[SUBLANES * j:SUBLANES * (j + 1), :] for j in range(PEER_TOPK)]
    for i, j in _SORT16:
        _exchange(v, i, j)
    for shift in (4, 2, 1):
        r = [pltpu.roll(x, shift, 0) for x in v]
        v = [jnp.maximum(v[i], r[PEER_TOPK - 1 - i]) for i in range(PEER_TOPK)]
        for stride in (8, 4, 2, 1):
            for i in range(PEER_TOPK):
                if i & stride == 0:
                    _exchange(v, i, i + stride)
    return v


def _rank_bits(x, v):
    b3 = v[7] > x
    b2 = jnp.where(b3, v[11], v[3]) > x
    b1 = jnp.where(b3, jnp.where(b2, v[13], v[9]), jnp.where(b2, v[5], v[1])) > x
    t = jnp.where(b3,
                  jnp.where(b2, jnp.where(b1, v[14], v[12]), jnp.where(b1, v[10], v[8])),
                  jnp.where(b2, jnp.where(b1, v[6], v[4]), jnp.where(b1, v[2], v[0])))
    return (b3, b2, b1, t > x), v[PEER_TOPK - 1] > x


def _select16(bits, vals):
    b3, b2, b1, b0 = bits
    lvl = [jnp.where(b0, vals[2 * i + 1], vals[2 * i]) for i in range(8)]
    lvl = [jnp.where(b1, lvl[2 * i + 1], lvl[2 * i]) for i in range(4)]
    lvl = [jnp.where(b2, lvl[2 * i + 1], lvl[2 * i]) for i in range(2)]
    return jnp.where(b3, lvl[1], lvl[0])


def _pair_counts(v1, v2):
    sub = lax.broadcasted_iota(jnp.int32, v1[0].shape, 0)
    ninf = -jnp.inf

    def column(v, off):
        col = v[off]
        for b in range(1, SUBLANES):
            col = jnp.where(sub == b, v[off + b], col)
        return col

    v2c0, v2c1 = column(v2, 0), column(v2, SUBLANES)
    v1c0, v1c1 = column(v1, 0), column(v1, SUBLANES)
    rows = [[v1[0] + v2c0, v1[0] + v2c1], [v1[1] + v2c0],
            [jnp.where(sub < 5, v1[2] + v2c0, ninf)], [jnp.where(sub < 4, v1[3] + v2c0, ninf)]]
    cols = [jnp.where(sub >= 4, v1c0 + v2[0], ninf), v1c1 + v2[0],
            jnp.where(sub >= 4, v1c0 + v2[1], ninf), jnp.where(sub == 4, v1c0 + v2[2], ninf)]
    cur = [c for r in rows for c in r] + cols
    top = v1[0] + v2[0]
    z = jnp.zeros_like(top)
    for _ in range(PEER_TOPK):
        m = cur[0]
        for c in cur[1:]:
            m = jnp.maximum(m, c)
        m = _allreduce_rows(m, jnp.maximum)
        z = z + jnp.exp(m - top)
        cur = [jnp.where(c == m, ninf, c) for c in cur]
    tau = m
    count = lambda c: jnp.where(c >= tau, 1.0, 0.0)
    nsel = []
    for r in rows:
        f = count(r[0])
        for c in r[1:]:
            f = f + count(c)
        nsel.append(_allreduce_rows(f, jnp.add))
    for a in range(4, PEER_TOPK):
        f = count(v1[a] + v2[0])
        for b in range(1, PEER_TOPK // (a + 1)):
            f = f + count(v1[a] + v2[b])
        nsel.append(f)
    total = nsel[0]
    for f in nsel[1:]:
        total = total + f
    return nsel, z, total != float(PEER_TOPK)


def _key_ranks(s, v, lookup=None):
    outs = []
    ranked = None
    for j in range(PEER_TOPK):
        x = s[SUBLANES * j:SUBLANES * (j + 1), :]
        bits, low = _rank_bits(x, v)
        if lookup is None:
            b3, b2, b1, b0 = bits
            val = (jnp.where(b3, 8.0, 0.0) + jnp.where(b2, 4.0, 0.0)
                   + jnp.where(b1, 2.0, 0.0) + jnp.where(b0, 1.0, 0.0) + jnp.where(low, 1.0, 0.0))
        else:
            val = jnp.where(low, 0.0, _select16(bits, lookup))
        outs.append(val)
        r = jnp.where(low, 0.0, 1.0)
        ranked = r if ranked is None else ranked + r
    tie = _allreduce_rows(ranked, jnp.add) != float(PEER_TOPK)
    for a in range(PEER_TOPK - 1):
        tie = tie | (v[a] == v[a + 1])
    return jnp.concatenate(outs, axis=0), tie


def _peer_select_kernel(x_ref, g_ref, wq_ref, k1_ref, k2_ref,
                        pos2_ref, m2_ref, nsel_ref, m1_ref, q_scr, s1_scr, s2_scr, *, tm):
    h = _rms(x_ref[...], g_ref[...]).astype(BF16)
    q = jnp.dot(h, wq_ref[...], preferred_element_type=F32)
    for j in range(2 * PEER_HEADS):
        q_scr[j] = q[:, j * PEER_KEYS:(j + 1) * PEER_KEYS].astype(BF16)
    k1 = k1_ref[...]
    k2 = k2_ref[...]

    def head(hh, carry):
        s1_scr[...] = lax.dot_general(k1, q_scr[2 * hh], _NT, preferred_element_type=F32)
        s2_scr[...] = lax.dot_general(k2, q_scr[2 * hh + 1], _NT, preferred_element_type=F32)
        tie = None
        for c in range(tm // LANES):
            sl = slice(c * LANES, (c + 1) * LANES)
            s1, s2 = s1_scr[:, sl], s2_scr[:, sl]
            v1 = _sorted_top16(s1)
            v2 = _sorted_top16(s2)
            nsel, z, t0 = _pair_counts(v1, v2)
            pos2, t2 = _key_ranks(s2, v2)
            nfull, t1 = _key_ranks(s1, v1, lookup=nsel)
            t = t0 | t1 | t2
            tie = t if tie is None else tie | t
            pos2_ref[hh, :, sl] = pos2.astype(BF16)
            m2_ref[hh, :, sl] = jnp.exp(s2 - v2[0][0:1, :]).astype(BF16)
            nsel_ref[hh, :, sl] = nfull
            m1_ref[hh, :, sl] = jnp.exp(s1 - v1[0][0:1, :]) * (1.0 / z[0:1, :])

        @pl.when(jnp.max(jnp.where(tie, 1.0, 0.0)) > 0.0)
        def _():
            for c in range(tm // LANES):
                sl = slice(c * LANES, (c + 1) * LANES)
                s1, s2 = s1_scr[:, sl], s2_scr[:, sl]
                vals1, pos1 = _top16(s1)
                vals2, pos2 = _top16(s2)
                nsel, z = _pair_select(vals1, vals2)
                nfull = jnp.zeros(s1.shape, jnp.int32)
                for a in range(PEER_TOPK):
                    nfull = jnp.where(pos1 == a, nsel[a:a + 1, :], nfull)
                pos2_ref[hh, :, sl] = pos2.astype(F32).astype(BF16)
                m2_ref[hh, :, sl] = jnp.exp(s2 - vals2[0]).astype(BF16)
                nsel_ref[hh, :, sl] = nfull.astype(F32)
                m1_ref[hh, :, sl] = jnp.exp(s1 - vals1[0]) * (1.0 / z)

        return carry

    lax.fori_loop(0, PEER_HEADS, head, 0)


PEER_SELECT_TILE = 256


def _peer_select(x, g, w_query, key1, key2):
    t, d = x.shape
    tm = PEER_SELECT_TILE
    nq = w_query.shape[1]
    head = pl.BlockSpec((PEER_HEADS, PEER_KEYS, tm), lambda i: (0, 0, i))
    keyspec = pl.BlockSpec((PEER_KEYS, PEER_KEYS), lambda i: (0, 0))
    shp = lambda dt: jax.ShapeDtypeStruct((PEER_HEADS, PEER_KEYS, t), dt)
    return pl.pallas_call(
        functools.partial(_peer_select_kernel, tm=tm),
        grid=(t // tm,),
        in_specs=[pl.BlockSpec((tm, d), lambda i: (i, 0)), pl.BlockSpec((1, d), lambda i: (0, 0)),
                  pl.BlockSpec((d, nq), lambda i: (0, 0)), keyspec, keyspec],
        out_specs=[head] * 4,
        out_shape=[shp(BF16), shp(BF16), shp(F32), shp(F32)],
        scratch_shapes=[pltpu.VMEM((2 * PEER_HEADS, tm, PEER_KEYS), BF16),
                        pltpu.VMEM((PEER_KEYS, tm), F32), pltpu.VMEM((PEER_KEYS, tm), F32)],
        compiler_params=_params("parallel"),
        name="peer_select",
    )(x, g.reshape(1, d), w_query, key1, key2)


def _peer_dense_kernel(x_ref, g_ref, gf_ref, u_ref, vt_ref, pos2_ref, m2_ref, nsel_ref, m1_ref,
                       o_ref, h_scr, acc_scr, a_scr, z_scr, p2_scr, m2_scr, *, tm, te, final_norm):
    j = pl.program_id(1)

    @pl.when(j == 0)
    def _():
        h_scr[...] = _rms(x_ref[...], g_ref[...]).astype(BF16)
        acc_scr[...] = jnp.zeros(acc_scr.shape, F32)
        p2_scr[...] = pos2_ref[...]
        m2_scr[...] = m2_ref[...]

    a_scr[...] = lax.dot_general(u_ref[...], h_scr[...], _NT, preferred_element_type=F32)
    n1 = te // PEER_KEYS
    first = pl.multiple_of(j * n1, n1)
    for c in range(tm // LANES):
        sl = slice(c * LANES, (c + 1) * LANES)
        ns = [nsel_ref[hh, pl.ds(first, n1), sl].astype(BF16) for hh in range(PEER_HEADS)]
        mm = [m1_ref[hh, pl.ds(first, n1), sl].astype(BF16) for hh in range(PEER_HEADS)]
        for r in range(n1):
            rows = slice(r * PEER_KEYS, (r + 1) * PEER_KEYS)
            w = None
            for hh in range(PEER_HEADS):
                t = jnp.where(p2_scr[hh, :, sl] < ns[hh][r:r + 1, :],
                              m2_scr[hh, :, sl] * mm[hh][r:r + 1, :], 0)
                w = t if w is None else w + t
            z_scr[rows, sl] = _gelu(a_scr[rows, sl]).astype(BF16) * w
    acc_scr[...] += jnp.dot(vt_ref[...], z_scr[...], preferred_element_type=F32)

    @pl.when(j == pl.num_programs(1) - 1)
    def _():
        out = x_ref[...] + acc_scr[...].T
        if final_norm:
            out = _rms(out, gf_ref[...])
        o_ref[...] = out


def _peer_dense(x, g, g_final, u, vt, sel, final_norm):
    t, d = x.shape
    tm = min(ROW_TILE, t)
    te = EXPERT_TILE
    assert te // PEER_KEYS == SUBLANES
    row = pl.BlockSpec((tm, d), lambda i, j: (i, 0))
    vec = pl.BlockSpec((1, d), lambda i, j: (0, 0))
    head = pl.BlockSpec((PEER_HEADS, PEER_KEYS, tm), lambda i, j: (0, 0, i))
    return pl.pallas_call(
        functools.partial(_peer_dense_kernel, tm=tm, te=te, final_norm=final_norm),
        grid=(t // tm, PEER_EXPERTS // te),
        in_specs=[row, vec, vec, pl.BlockSpec((te, d), lambda i, j: (j, 0)),
                  pl.BlockSpec((d, te), lambda i, j: (0, j)), head, head, head, head],
        out_specs=row,
        out_shape=jax.ShapeDtypeStruct((t, d), F32),
        scratch_shapes=[pltpu.VMEM((tm, d), BF16), pltpu.VMEM((d, tm), F32),
                        pltpu.VMEM((te, tm), F32), pltpu.VMEM((te, tm), BF16),
                        pltpu.VMEM((PEER_HEADS, PEER_KEYS, tm), BF16),
                        pltpu.VMEM((PEER_HEADS, PEER_KEYS, tm), BF16)],
        compiler_params=_params("parallel", "arbitrary"),
        name="peer_dense",
    )(x, g.reshape(1, d), g_final.reshape(1, d), u, vt, *sel)


def _peer(x, g, g_final, w_query, key1, key2, u, vt, final_norm):
    shp = x.shape
    x2 = x.reshape(-1, shp[-1])
    sel = _peer_select(x2, g, w_query, key1, key2)
    return _peer_dense(x2, g, g_final, u, vt, sel, final_norm).reshape(shp)


def _trunk(x, ssm0, conv0, mem_k, mem_v, w, sample):
    b, s, d = x.shape
    chunk = s if sample else S5_CHUNK
    if ssm0 is None:
        ssm0 = (jnp.zeros((b, SSM_GROUPS, SSM_STATE), F32),) * 2
    x, ssm_re, ssm_im = _s5_mixer(x, ssm0[0], ssm0[1], w['norm_mix'][0], w['s5_fold'][chunk],
                                  w['ssm_d'][0], w['ssm_w_glu'][0], chunk)
    attn = _attn_sample if sample else _attn_prompt
    for i in range(2):
        if i == 1:
            if sample:
                x_tm = x.transpose(1, 0, 2).reshape(s * b, d)
                buf_tm = conv0.transpose(1, 0, 2).reshape(2 * b, d)
                x_tm, nbuf = _conv_sample(x_tm, buf_tm, w['norm_mix'][1], w['conv_w_in'][0],
                                          w['conv_w'][0], w['conv_w_out'][0], s, b)
                x = x_tm.reshape(s, b, d).transpose(1, 0, 2)
                conv_out = nbuf.reshape(2, b, d).transpose(1, 0, 2)
            else:
                x, tail = _conv_prompt(x, w['norm_mix'][1], w['conv_w_in'][0], w['conv_w'][0],
                                       w['conv_w_out'][0])
                conv_out = tail[:, CONV_HEAD - 2:]
        x = attn(x, w['norm_mem'][i], mem_k, mem_v, i, w['mem_w_q'][i], w['mem_w_o'][i])
        x = _peer(x, w['norm_ffn'][i], w['norm_final'], w['peer_w_query'][i], w['peer_key1'][i],
                  w['peer_key2'][i], w['peer_u'][i], w['peer_vt'][i], final_norm=(i == 1))
    return x, ssm_re[None], ssm_im[None], conv_out[None]


def kernel(x_prompt, x_sample, mem_prompt, state_ssm_re, state_ssm_im, state_conv, cache_mem_k, cache_mem_v, norm_mix, norm_mem, norm_ffn, norm_final, ssm_a_re, ssm_a_im, ssm_log_dt, ssm_b_re, ssm_b_im, ssm_c_re, ssm_c_im, ssm_d, ssm_w_glu, conv_w_in, conv_w, conv_w_out, mem_w_q, mem_w_k, mem_w_v, mem_w_o, peer_w_query, peer_key1, peer_key2, peer_u, peer_v):
    bsz, seq, d = x_prompt.shape
    dec_b, dec_s, _ = x_sample.shape
    depth = mem_w_q.shape[0]
    fold = lambda chunk: _s5_fold(ssm_a_re[0], ssm_a_im[0], ssm_log_dt[0], ssm_b_re[0], ssm_b_im[0],
                                  ssm_c_re[0], ssm_c_im[0], chunk)
    w = dict(
        norm_mix=norm_mix, norm_mem=norm_mem, norm_ffn=norm_ffn, norm_final=norm_final,
        s5_fold={S5_CHUNK: fold(S5_CHUNK), dec_s: fold(dec_s)},
        ssm_d=ssm_d, ssm_w_glu=ssm_w_glu.astype(BF16),
        conv_w_in=conv_w_in.astype(BF16), conv_w=conv_w, conv_w_out=conv_w_out.astype(BF16),
        mem_w_q=mem_w_q.astype(BF16), mem_w_o=mem_w_o.astype(BF16),
        peer_w_query=peer_w_query.astype(BF16), peer_key1=peer_key1.astype(BF16),
        peer_key2=peer_key2.astype(BF16), peer_u=peer_u.astype(BF16),
        peer_vt=peer_v.astype(BF16).transpose(0, 2, 1),
    )
    w_kv = jnp.concatenate([mem_w_k[i] for i in range(depth)] + [mem_w_v[i] for i in range(depth)],
                           axis=1).astype(BF16)
    kv = _mm(mem_prompt.reshape(bsz * MEM_TOKENS, d), w_kv, tm=256)
    kv = kv.reshape(bsz, MEM_TOKENS, 2 * depth, d).transpose(2, 0, 1, 3)
    mem_k_p, mem_v_p = kv[:depth], kv[depth:]

    y_p, re_p, im_p, conv_p = _trunk(x_prompt, None, None, mem_k_p, mem_v_p, w, sample=False)
    cache_k = cache_mem_k.astype(BF16).reshape(depth, dec_b, MEM_TOKENS, d)
    cache_v = cache_mem_v.astype(BF16).reshape(depth, dec_b, MEM_TOKENS, d)
    y_s, re_s, im_s, conv_s = _trunk(x_sample, (state_ssm_re[0], state_ssm_im[0]), state_conv[0],
                                     cache_k, cache_v, w, sample=True)
    head_shape = (depth, bsz, MEM_TOKENS, MEM_HEADS, MEM_HEAD_DIM)
    return (y_p, y_s, re_p, im_p, conv_p, mem_k_p.reshape(head_shape), mem_v_p.reshape(head_shape),
            re_s, im_s, conv_s)
```

```python
import functools
import math

import jax
import jax.numpy as jnp
from jax import lax
from jax.experimental import pallas as pl
from jax.experimental.pallas import tpu as pltpu

F32 = jnp.float32
BF16 = jnp.bfloat16

D_MODEL = 1024
SSM_GROUP = 16
SSM_GROUPS = D_MODEL // SSM_GROUP
SSM_STATE = 64
SSM_BLOCK_GROUPS = 8
SSM_BLOCKS = SSM_GROUPS // SSM_BLOCK_GROUPS
MEM_TOKENS = 256
MEM_HEADS = 4
MEM_HEAD_DIM = D_MODEL // MEM_HEADS
PEER_HEADS = 8
PEER_KEYS = 128
PEER_EXPERTS = PEER_KEYS * PEER_KEYS
PEER_TOPK = 16
RMS_EPS = 1e-6

LANES = 128
SUBLANES = 8
ROW_TILE = 512
S5_CHUNK = 8
S5_COL_TILE = 512
EXPERT_TILE = 1024
VMEM_LIMIT = 48 * 1024 * 1024

_NT = (((1,), (1,)), ((), ()))


def _params(*sem):
    return pltpu.CompilerParams(dimension_semantics=sem, vmem_limit_bytes=VMEM_LIMIT)


def _rms(x, g):
    r = lax.rsqrt(jnp.mean(x * x, axis=-1, keepdims=True) + RMS_EPS)
    return (x * r) * g


def _gelu(x):
    return 0.5 * x * (1.0 + lax.erf(x * (1.0 / math.sqrt(2.0))))


def _sigmoid(x):
    return 1.0 / (1.0 + jnp.exp(-x))


def _norm_cast_kernel(x_ref, g_ref, o_ref):
    o_ref[...] = _rms(x_ref[...], g_ref[...]).astype(BF16)


def _norm_cast(x, g):
    t, d = x.shape
    tm = min(ROW_TILE, t)
    return pl.pallas_call(
        _norm_cast_kernel,
        grid=(t // tm,),
        in_specs=[pl.BlockSpec((tm, d), lambda i: (i, 0)), pl.BlockSpec((1, d), lambda i: (0, 0))],
        out_specs=pl.BlockSpec((tm, d), lambda i: (i, 0)),
        out_shape=jax.ShapeDtypeStruct((t, d), BF16),
        compiler_params=_params("parallel"),
        name="norm_cast",
    )(x, g.reshape(1, d))


def _mm_kernel(*refs, has_norm, has_res):
    x_ref, w_ref = refs[0], refs[1]
    pos = 2
    x = x_ref[...]
    if has_norm:
        x = _rms(x, refs[pos][...])
        pos += 1
    y = jnp.dot(x.astype(BF16), w_ref[...], preferred_element_type=F32)
    if has_res:
        y = y + refs[pos][...]
        pos += 1
    refs[pos][...] = y


def _mm(x, w, g=None, res=None, tm=ROW_TILE):
    t, k = x.shape
    n = w.shape[1]
    tm = min(tm, t)
    args = [x, w]
    specs = [pl.BlockSpec((tm, k), lambda i: (i, 0)), pl.BlockSpec((k, n), lambda i: (0, 0))]
    if g is not None:
        args.append(g.reshape(1, k))
        specs.append(pl.BlockSpec((1, k), lambda i: (0, 0)))
    if res is not None:
        args.append(res)
        specs.append(pl.BlockSpec((tm, n), lambda i: (i, 0)))
    return pl.pallas_call(
        functools.partial(_mm_kernel, has_norm=g is not None, has_res=res is not None),
        grid=(t // tm,),
        in_specs=specs,
        out_specs=pl.BlockSpec((tm, n), lambda i: (i, 0)),
        out_shape=jax.ShapeDtypeStruct((t, n), F32),
        compiler_params=_params("parallel"),
        name="mm_rows",
    )(*args)


def _s5_fold(a_re, a_im, log_dt, b_re, b_im, c_re, c_im, chunk):
    hi = lax.Precision.HIGHEST
    dt = jnp.exp(log_dt)[:, None]
    mag = jnp.exp(a_re * dt)
    ang = a_im * dt
    lb_re = mag * jnp.cos(ang)
    lb_im = mag * jnp.sin(ang)
    den = a_re * a_re + a_im * a_im
    f_re = ((lb_re - 1.0) * a_re + lb_im * a_im) / den
    f_im = (lb_im * a_re - (lb_re - 1.0) * a_im) / den
    bb_re = f_re[..., None] * b_re - f_im[..., None] * b_im
    bb_im = f_re[..., None] * b_im + f_im[..., None] * b_re
    pw_re, pw_im = [jnp.ones_like(lb_re)], [jnp.zeros_like(lb_im)]
    for _ in range(chunk):
        pr, pi = pw_re[-1], pw_im[-1]
        pw_re.append(pr * lb_re - pi * lb_im)
        pw_im.append(pr * lb_im + pi * lb_re)
    pw_re, pw_im = jnp.stack(pw_re), jnp.stack(pw_im)
    cl_re = c_re[None] * pw_re[:, :, None, :] - c_im[None] * pw_im[:, :, None, :]
    cl_im = c_re[None] * pw_im[:, :, None, :] + c_im[None] * pw_re[:, :, None, :]
    kern = (jnp.einsum('kgdp,gpc->kgdc', cl_re[:chunk], bb_re, precision=hi)
            - jnp.einsum('kgdp,gpc->kgdc', cl_im[:chunk], bb_im, precision=hi))
    step = jnp.arange(chunk)
    rp_re, rp_im = pw_re[chunk - 1 - step], pw_im[chunk - 1 - step]
    p_re = rp_re[..., None] * bb_re[None] - rp_im[..., None] * bb_im[None]
    p_im = rp_re[..., None] * bb_im[None] + rp_im[..., None] * bb_re[None]
    p_re = p_re.transpose(1, 0, 3, 2)
    p_im = p_im.transpose(1, 0, 3, 2)
    n_re = cl_re[1:].transpose(1, 3, 0, 2)
    n_im = (-cl_im[1:]).transpose(1, 3, 0, 2)

    nb, q = SSM_BLOCKS, SSM_BLOCK_GROUPS
    eye = jnp.eye(q, dtype=BF16)
    w = chunk * q * SSM_GROUP
    sl = q * SSM_STATE
    kq = kern.transpose(1, 3, 0, 2).astype(BF16).reshape(nb, q, SSM_GROUP, chunk, 1, SSM_GROUP)
    kq = (kq * eye[None, :, None, None, :, None]).reshape(nb, q, SSM_GROUP, chunk, q * SSM_GROUP)
    kq = jnp.pad(kq, ((0, 0), (0, 0), (0, 0), (chunk - 1, 0), (0, 0)))
    m = jnp.stack([kq[:, :, :, chunk - 1 - t:2 * chunk - 1 - t] for t in range(chunk)], axis=1)
    m = m.reshape(nb, w, w)

    def fold_p(z):
        z = z.astype(BF16).reshape(nb, q, chunk, SSM_GROUP, 1, SSM_STATE).transpose(0, 2, 1, 3, 4, 5)
        return (z * eye[None, None, :, None, :, None]).reshape(nb, w, sl)

    def fold_n(z):
        z = z.astype(BF16).reshape(nb, q, SSM_STATE, chunk, 1, SSM_GROUP)
        return (z * eye[None, :, None, None, :, None]).reshape(nb, sl, w)

    lam_re = pw_re[chunk].reshape(nb, 1, sl)
    lam_im = pw_im[chunk].reshape(nb, 1, sl)
    return m, fold_p(p_re), fold_p(p_im), fold_n(n_re), fold_n(n_im), lam_re, lam_im


def _s5_core_kernel(x_ref, m_ref, pre_ref, pim_ref, nre_ref, nim_ref, lre_ref, lim_ref,
                    s0re_ref, s0im_ref, y_ref, sre_ref, sim_ref, stre, stim, *, nchunks, rows):
    x = x_ref[0]

    @pl.when(pl.program_id(1) == 0)
    def _():
        stre[...] = jnp.dot(x, pre_ref[0], preferred_element_type=F32)
        stim[...] = jnp.dot(x, pim_ref[0], preferred_element_type=F32)
        width = stre.shape[1]
        lr = jnp.broadcast_to(lre_ref[0], (rows, width))
        li = jnp.broadcast_to(lim_ref[0], (rows, width))

        def body(k, carry):
            sr, si = carry
            off = pl.multiple_of(k * rows, rows)
            qr = stre[pl.ds(off, rows), :]
            qi = stim[pl.ds(off, rows), :]
            stre[pl.ds(off, rows), :] = sr
            stim[pl.ds(off, rows), :] = si
            return lr * sr - li * si + qr, lr * si + li * sr + qi

        sr, si = lax.fori_loop(0, nchunks, body, (s0re_ref[0], s0im_ref[0]))
        sre_ref[0] = sr
        sim_ref[0] = si

    y = jnp.dot(x, m_ref[0], preferred_element_type=F32)
    y = y + jnp.dot(stre[...].astype(BF16), nre_ref[0], preferred_element_type=F32)
    y = y + jnp.dot(stim[...].astype(BF16), nim_ref[0], preferred_element_type=F32)
    y_ref[0] = y


def _s5_core(xb, folded, s0_re, s0_im, nchunks, rows):
    m, p_re, p_im, n_re, n_im, lam_re, lam_im = folded
    nblk, nr, w = xb.shape
    sl = SSM_BLOCK_GROUPS * SSM_STATE
    wc = min(w, S5_COL_TILE)
    fixed = lambda a, b: pl.BlockSpec((1, a, b), lambda i, j: (i, 0, 0))
    cols = lambda a: pl.BlockSpec((1, a, wc), lambda i, j: (i, 0, j))
    return pl.pallas_call(
        functools.partial(_s5_core_kernel, nchunks=nchunks, rows=rows),
        grid=(nblk, w // wc),
        in_specs=[fixed(nr, w), cols(w), fixed(w, sl), fixed(w, sl), cols(sl), cols(sl),
                  fixed(1, sl), fixed(1, sl), fixed(rows, sl), fixed(rows, sl)],
        out_specs=[cols(nr), fixed(rows, sl), fixed(rows, sl)],
        out_shape=[jax.ShapeDtypeStruct((nblk, nr, w), F32),
                   jax.ShapeDtypeStruct((nblk, rows, sl), F32),
                   jax.ShapeDtypeStruct((nblk, rows, sl), F32)],
        scratch_shapes=[pltpu.VMEM((nr, sl), F32)] * 2,
        compiler_params=_params("parallel", "arbitrary"),
        name="s5_core",
    )(xb, m, p_re, p_im, n_re, n_im, lam_re, lam_im, s0_re, s0_im)


def _s5_out_kernel(x_ref, y_ref, g_ref, d_ref, w_ref, o_ref):
    x = x_ref[...]
    h = _rms(x, g_ref[...])
    z = _gelu(y_ref[...] + d_ref[...] * h)
    gg = jnp.dot(z.astype(BF16), w_ref[...], preferred_element_type=F32)
    o_ref[...] = x + gg[:, :D_MODEL] * _sigmoid(gg[:, D_MODEL:])


def _s5_out(x, y, g, d, w_glu):
    t = x.shape[0]
    tm = min(ROW_TILE, t)
    row = pl.BlockSpec((tm, D_MODEL), lambda i: (i, 0))
    vec = pl.BlockSpec((1, D_MODEL), lambda i: (0, 0))
    return pl.pallas_call(
        _s5_out_kernel,
        grid=(t // tm,),
        in_specs=[row, row, vec, vec, pl.BlockSpec((D_MODEL, 2 * D_MODEL), lambda i: (0, 0))],
        out_specs=row,
        out_shape=jax.ShapeDtypeStruct((t, D_MODEL), F32),
        compiler_params=_params("parallel"),
        name="s5_out",
    )(x, y, g.reshape(1, -1), d.reshape(1, -1), w_glu)


def _s5_mixer(x, s0_re, s0_im, g, folded, d_skip, w_glu, chunk):
    b, s, d = x.shape
    nk = s // chunk
    x2 = x.reshape(b * s, d)
    h = _norm_cast(x2, g)
    xb = (h.reshape(b, nk, chunk, SSM_BLOCKS, LANES).transpose(3, 1, 0, 2, 4)
          .reshape(SSM_BLOCKS, nk * b, chunk * LANES))
    sl = SSM_BLOCK_GROUPS * SSM_STATE
    to_block = lambda z: z.reshape(b, SSM_BLOCKS, sl).transpose(1, 0, 2)
    y, sr, si = _s5_core(xb, folded, to_block(s0_re), to_block(s0_im), nk, b)
    y = y.reshape(SSM_BLOCKS, nk, b, chunk, LANES).transpose(2, 1, 3, 0, 4).reshape(b * s, d)
    from_block = lambda z: z.transpose(1, 0, 2).reshape(b, SSM_GROUPS, SSM_STATE)
    out = _s5_out(x2, y, g, d_skip, w_glu)
    return out.reshape(b, s, d), from_block(sr), from_block(si)


CONV_HEAD = 8


def _conv_prompt_kernel(x_ref, g_ref, win_ref, wc_ref, wout_ref, o_ref, tail_ref, vbuf, *, tm):
    @pl.when(pl.program_id(1) == 0)
    def _():
        vbuf[0:CONV_HEAD, :] = jnp.zeros((CONV_HEAD, D_MODEL), F32)

    x = x_ref[0]
    h = _rms(x, g_ref[...])
    p = jnp.dot(h.astype(BF16), win_ref[...], preferred_element_type=F32)
    bg = p[:, :D_MODEL]
    v = p[:, D_MODEL:2 * D_MODEL] * p[:, 2 * D_MODEL:]
    vbuf[CONV_HEAD:CONV_HEAD + tm, :] = v
    v2 = vbuf[CONV_HEAD - 2:CONV_HEAD - 2 + tm, :]
    v1 = vbuf[CONV_HEAD - 1:CONV_HEAD - 1 + tm, :]
    conv = wc_ref[0:1, :] * v2 + wc_ref[1:2, :] * v1 + wc_ref[2:3, :] * v
    out = jnp.dot((bg * conv).astype(BF16), wout_ref[...], preferred_element_type=F32)
    o_ref[0] = x + out
    tail = vbuf[tm:tm + CONV_HEAD, :]
    vbuf[0:CONV_HEAD, :] = tail
    tail_ref[0] = tail


def _conv_prompt(x, g, w_in, w_conv, w_out):
    b, s, d = x.shape
    tm = min(ROW_TILE, s)
    return pl.pallas_call(
        functools.partial(_conv_prompt_kernel, tm=tm),
        grid=(b, s // tm),
        in_specs=[pl.BlockSpec((1, tm, d), lambda i, j: (i, j, 0)),
                  pl.BlockSpec((1, d), lambda i, j: (0, 0)),
                  pl.BlockSpec((d, 3 * d), lambda i, j: (0, 0)),
                  pl.BlockSpec((3, d), lambda i, j: (0, 0)),
                  pl.BlockSpec((d, d), lambda i, j: (0, 0))],
        out_specs=[pl.BlockSpec((1, tm, d), lambda i, j: (i, j, 0)),
                   pl.BlockSpec((1, CONV_HEAD, d), lambda i, j: (i, 0, 0))],
        out_shape=[jax.ShapeDtypeStruct((b, s, d), F32),
                   jax.ShapeDtypeStruct((b, CONV_HEAD, d), F32)],
        scratch_shapes=[pltpu.VMEM((CONV_HEAD + tm, d), F32)],
        compiler_params=_params("parallel", "arbitrary"),
        name="conv_prompt",
    )(x, g.reshape(1, d), w_in, w_conv, w_out)


def _conv_sample_kernel(x_ref, buf_ref, g_ref, win_ref, wc_ref, wout_ref, o_ref, nbuf_ref, *, steps, nb):
    x = x_ref[...]
    h = _rms(x, g_ref[...])
    p = jnp.dot(h.astype(BF16), win_ref[...], preferred_element_type=F32)
    bg = p[:, :D_MODEL]
    v = p[:, D_MODEL:2 * D_MODEL] * p[:, 2 * D_MODEL:]
    vp = [buf_ref[0:nb, :], buf_ref[nb:2 * nb, :]] + [v[t * nb:(t + 1) * nb, :] for t in range(steps)]
    conv = jnp.concatenate(
        [wc_ref[0:1, :] * vp[t] + wc_ref[1:2, :] * vp[t + 1] + wc_ref[2:3, :] * vp[t + 2]
         for t in range(steps)], axis=0)
    out = jnp.dot((bg * conv).astype(BF16), wout_ref[...], preferred_element_type=F32)
    o_ref[...] = x + out
    nbuf_ref[0:nb, :] = vp[-2]
    nbuf_ref[nb:2 * nb, :] = vp[-1]


def _conv_sample(x_tm, buf_tm, g, w_in, w_conv, w_out, steps, nb):
    d = D_MODEL
    full = lambda r, c: pl.BlockSpec((r, c), lambda i: (0, 0))
    return pl.pallas_call(
        functools.partial(_conv_sample_kernel, steps=steps, nb=nb),
        grid=(1,),
        in_specs=[full(steps * nb, d), full(2 * nb, d), full(1, d), full(d, 3 * d), full(3, d), full(d, d)],
        out_specs=[full(steps * nb, d), full(2 * nb, d)],
        out_shape=[jax.ShapeDtypeStruct((steps * nb, d), F32), jax.ShapeDtypeStruct((2 * nb, d), F32)],
        compiler_params=_params("arbitrary"),
        name="conv_sample",
    )(x_tm, buf_tm, g.reshape(1, d), w_in, w_conv, w_out)


def _softmax_rows(s):
    e = jnp.exp(s - jnp.max(s, axis=-1, keepdims=True))
    return e / jnp.sum(e, axis=-1, keepdims=True)


def _attn_heads(q, k, v):
    k = k.astype(BF16)
    v = v.astype(BF16)
    outs = []
    for hd in range(MEM_HEADS):
        sl = slice(hd * MEM_HEAD_DIM, (hd + 1) * MEM_HEAD_DIM)
        s = lax.dot_general(q[:, sl].astype(BF16), k[:, sl], _NT, preferred_element_type=F32)
        p = _softmax_rows(s * (MEM_HEAD_DIM ** -0.5))
        outs.append(jnp.dot(p.astype(BF16), v[:, sl], preferred_element_type=F32))
    return jnp.concatenate(outs, axis=-1)


def _attn_prompt_kernel(x_ref, g_ref, k_ref, v_ref, wq_ref, wo_ref, o_ref):
    x = x_ref[0]
    h = _rms(x, g_ref[...])
    q = jnp.dot(h.astype(BF16), wq_ref[...], preferred_element_type=F32)
    o = _attn_heads(q, k_ref[0], v_ref[0])
    o_ref[0] = x + jnp.dot(o.astype(BF16), wo_ref[...], preferred_element_type=F32)


def _attn_prompt(x, g, mem_k, mem_v, layer, w_q, w_o):
    b, s, d = x.shape
    tm = min(ROW_TILE, s)
    mem = pl.BlockSpec((None, 1, MEM_TOKENS, d), lambda i, j: (layer, i, 0, 0))
    wgt = pl.BlockSpec((d, d), lambda i, j: (0, 0))
    row = pl.BlockSpec((1, tm, d), lambda i, j: (i, j, 0))
    return pl.pallas_call(
        _attn_prompt_kernel,
        grid=(b, s // tm),
        in_specs=[row, pl.BlockSpec((1, d), lambda i, j: (0, 0)), mem, mem, wgt, wgt],
        out_specs=row,
        out_shape=jax.ShapeDtypeStruct((b, s, d), F32),
        compiler_params=_params("parallel", "parallel"),
        name="attn_prompt",
    )(x, g.reshape(1, d), mem_k, mem_v, w_q, w_o)


SAMPLE_Q_ROWS = 8
SAMPLE_SEQ_BLOCK = 4


def _attn_sample_kernel(q_ref, k_ref, v_ref, o_ref):
    def body(i, carry):
        o_ref[i] = _attn_heads(q_ref[i], k_ref[i], v_ref[i])
        return carry

    lax.fori_loop(0, SAMPLE_SEQ_BLOCK, body, 0)


def _attn_sample(x, g, cache_k, cache_v, layer, w_q, w_o):
    b, s, d = x.shape
    x2 = x.reshape(b * s, d)
    q = _mm(x2, w_q, g=g).reshape(b, s, d)
    q = jnp.pad(q, ((0, 0), (0, SAMPLE_Q_ROWS - s), (0, 0)))
    qblk = pl.BlockSpec((SAMPLE_SEQ_BLOCK, SAMPLE_Q_ROWS, d), lambda i: (i, 0, 0))
    mblk = pl.BlockSpec((None, SAMPLE_SEQ_BLOCK, MEM_TOKENS, d), lambda i: (layer, i, 0, 0))
    o = pl.pallas_call(
        _attn_sample_kernel,
        grid=(b // SAMPLE_SEQ_BLOCK,),
        in_specs=[qblk, mblk, mblk],
        out_specs=qblk,
        out_shape=jax.ShapeDtypeStruct((b, SAMPLE_Q_ROWS, d), F32),
        compiler_params=_params("parallel"),
        name="attn_sample",
    )(q, cache_k, cache_v)
    o = o[:, :s].reshape(b * s, d)
    return _mm(o, w_o, res=x2).reshape(b, s, d)


def _top16(s):
    iota = lax.broadcasted_iota(jnp.int32, s.shape, 0)
    pos = jnp.full(s.shape, PEER_TOPK, jnp.int32)
    vals = []
    for i in range(PEER_TOPK):
        m = jnp.max(s, axis=0, keepdims=True)
        idx = jnp.min(jnp.where(s == m, iota, s.shape[0]), axis=0, keepdims=True)
        hit = iota == idx
        pos = jnp.where(hit, i, pos)
        s = jnp.where(hit, -jnp.inf, s)
        vals.append(m)
    return vals, pos


def _pair_select(vals1, vals2):
    lanes = vals1[0].shape[1]
    iota16 = lax.broadcasted_iota(jnp.int32, (PEER_TOPK, lanes), 0)
    v2 = jnp.zeros((PEER_TOPK, lanes), F32)
    for b in range(PEER_TOPK):
        v2 = jnp.where(iota16 == b, vals2[b], v2)
    iota8 = iota16[:8]
    cand = [vals1[0] + v2]
    flat = [iota16]
    for a in range(1, PEER_TOPK):
        cand.append(jnp.where(iota8 < PEER_TOPK // (a + 1), vals1[a] + v2[:8], -jnp.inf))
        flat.append(iota8 + a * PEER_TOPK)
    nsel = jnp.zeros((PEER_TOPK, lanes), jnp.int32)
    z = jnp.zeros((1, lanes), F32)
    top = vals1[0] + vals2[0]
    nflat = PEER_TOPK * PEER_TOPK
    for _ in range(PEER_TOPK):
        mx = jnp.maximum(cand[0][:8], cand[0][8:])
        for a in range(1, PEER_TOPK):
            mx = jnp.maximum(mx, cand[a])
        m = jnp.max(mx, axis=0, keepdims=True)
        w0 = jnp.where(cand[0] == m, flat[0], nflat)
        mi = jnp.minimum(w0[:8], w0[8:])
        for a in range(1, PEER_TOPK):
            mi = jnp.minimum(mi, jnp.where(cand[a] == m, flat[a], nflat))
        idx = jnp.min(mi, axis=0, keepdims=True)
        cand = [jnp.where(flat[a] == idx, -jnp.inf, cand[a]) for a in range(PEER_TOPK)]
        nsel = nsel + (iota16 == (idx >> 4)).astype(jnp.int32)
        z = z + jnp.exp(m - top)
    return nsel, z


def _sort_network(n):
    pairs = []

    def merge(lo, cnt, r):
        m = r * 2
        if m < cnt:
            merge(lo, cnt, m)
            merge(lo + r, cnt, m)
            for i in range(lo + r, lo + cnt - r, m):
                pairs.append((i, i + r))
        else:
            pairs.append((lo, lo + r))

    def sort(lo, cnt):
        if cnt > 1:
            m = cnt // 2
            sort(lo, m)
            sort(lo + m, m)
            merge(lo, cnt, 1)

    sort(0, n)
    return tuple(pairs)


_SORT16 = _sort_network(PEER_TOPK)


def _exchange(v, i, j):
    v[i], v[j] = jnp.maximum(v[i], v[j]), jnp.minimum(v[i], v[j])


def _allreduce_rows(x, op):
    for shift in (4, 2, 1):
        x = op(x, pltpu.roll(x, shift, 0))
    return x


def _sorted_top16(s):
    v = [s[SUBLANES * j:SUBLANES * (j + 1), :] for j in range(PEER_TOPK)]
    for i, j in _SORT16:
        _exchange(v, i, j)
    for shift in (4, 2, 1):
        r = [pltpu.roll(x, shift, 0) for x in v]
        v = [jnp.maximum(v[i], r[PEER_TOPK - 1 - i]) for i in range(PEER_TOPK)]
        for stride in (8, 4, 2, 1):
            for i in range(PEER_TOPK):
                if i & stride == 0:
                    _exchange(v, i, i + stride)
    return v


def _rank_bits(x, v):
    b3 = v[7] > x
    b2 = jnp.where(b3, v[11], v[3]) > x
    b1 = jnp.where(b3, jnp.where(b2, v[13], v[9]), jnp.where(b2, v[5], v[1])) > x
    t = jnp.where(b3,
                  jnp.where(b2, jnp.where(b1, v[14], v[12]), jnp.where(b1, v[10], v[8])),
                  jnp.where(b2, jnp.where(b1, v[6], v[4]), jnp.where(b1, v[2], v[0])))
    return (b3, b2, b1, t > x), v[PEER_TOPK - 1] > x


def _select16(bits, vals):
    b3, b2, b1, b0 = bits
    lvl = [jnp.where(b0, vals[2 * i + 1], vals[2 * i]) for i in range(8)]
    lvl = [jnp.where(b1, lvl[2 * i + 1], lvl[2 * i]) for i in range(4)]
    lvl = [jnp.where(b2, lvl[2 * i + 1], lvl[2 * i]) for i in range(2)]
    return jnp.where(b3, lvl[1], lvl[0])


def _pair_counts(v1, v2):
    sub = lax.broadcasted_iota(jnp.int32, v1[0].shape, 0)
    ninf = -jnp.inf

    def column(v, off):
        col = v[off]
        for b in range(1, SUBLANES):
            col = jnp.where(sub == b, v[off + b], col)
        return col

    v2c0, v2c1 = column(v2, 0), column(v2, SUBLANES)
    v1c0, v1c1 = column(v1, 0), column(v1, SUBLANES)
    rows = [[v1[0] + v2c0, v1[0] + v2c1], [v1[1] + v2c0],
            [jnp.where(sub < 5, v1[2] + v2c0, ninf)], [jnp.where(sub < 4, v1[3] + v2c0, ninf)]]
    cols = [jnp.where(sub >= 4, v1c0 + v2[0], ninf), v1c1 + v2[0],
            jnp.where(sub >= 4, v1c0 + v2[1], ninf), jnp.where(sub == 4, v1c0 + v2[2], ninf)]
    cur = [c for r in rows for c in r] + cols
    top = v1[0] + v2[0]
    z = jnp.zeros_like(top)
    for _ in range(PEER_TOPK):
        m = cur[0]
        for c in cur[1:]:
            m = jnp.maximum(m, c)
        m = _allreduce_rows(m, jnp.maximum)
        z = z + jnp.exp(m - top)
        cur = [jnp.where(c == m, ninf, c) for c in cur]
    tau = m
    count = lambda c: jnp.where(c >= tau, 1.0, 0.0)
    nsel = []
    for r in rows:
        f = count(r[0])
        for c in r[1:]:
            f = f + count(c)
        nsel.append(_allreduce_rows(f, jnp.add))
    for a in range(4, PEER_TOPK):
        f = count(v1[a] + v2[0])
        for b in range(1, PEER_TOPK // (a + 1)):
            f = f + count(v1[a] + v2[b])
        nsel.append(f)
    total = nsel[0]
    for f in nsel[1:]:
        total = total + f
    return nsel, z, total != float(PEER_TOPK)


def _key_ranks(s, v, lookup=None):
    outs = []
    ranked = None
    for j in range(PEER_TOPK):
        x = s[SUBLANES * j:SUBLANES * (j + 1), :]
        bits, low = _rank_bits(x, v)
        if lookup is None:
            b3, b2, b1, b0 = bits
            val = (jnp.where(b3, 8.0, 0.0) + jnp.where(b2, 4.0, 0.0)
                   + jnp.where(b1, 2.0, 0.0) + jnp.where(b0, 1.0, 0.0) + jnp.where(low, 1.0, 0.0))
        else:
            val = jnp.where(low, 0.0, _select16(bits, lookup))
        outs.append(val)
        r = jnp.where(low, 0.0, 1.0)
        ranked = r if ranked is None else ranked + r
    tie = _allreduce_rows(ranked, jnp.add) != float(PEER_TOPK)
    for a in range(PEER_TOPK - 1):
        tie = tie | (v[a] == v[a + 1])
    return jnp.concatenate(outs, axis=0), tie


def _peer_select_kernel(x_ref, g_ref, wq_ref, k1_ref, k2_ref,
                        pos2_ref, m2_ref, nsel_ref, m1_ref, q_scr, s1_scr, s2_scr, *, tm):
    h = _rms(x_ref[...], g_ref[...]).astype(BF16)
    q = jnp.dot(h, wq_ref[...], preferred_element_type=F32)
    for j in range(2 * PEER_HEADS):
        q_scr[j] = q[:, j * PEER_KEYS:(j + 1) * PEER_KEYS].astype(BF16)
    k1 = k1_ref[...]
    k2 = k2_ref[...]

    def head(hh, carry):
        s1_scr[...] = lax.dot_general(k1, q_scr[2 * hh], _NT, preferred_element_type=F32)
        s2_scr[...] = lax.dot_general(k2, q_scr[2 * hh + 1], _NT, preferred_element_type=F32)
        tie = None
        for c in range(tm // LANES):
            sl = slice(c * LANES, (c + 1) * LANES)
            s1, s2 = s1_scr[:, sl], s2_scr[:, sl]
            v1 = _sorted_top16(s1)
            v2 = _sorted_top16(s2)
            nsel, z, t0 = _pair_counts(v1, v2)
            pos2, t2 = _key_ranks(s2, v2)
            nfull, t1 = _key_ranks(s1, v1, lookup=nsel)
            t = t0 | t1 | t2
            tie = t if tie is None else tie | t
            pos2_ref[hh, :, sl] = pos2.astype(BF16)
            m2_ref[hh, :, sl] = jnp.exp(s2 - v2[0][0:1, :]).astype(BF16)
            nsel_ref[hh, :, sl] = nfull
            m1_ref[hh, :, sl] = jnp.exp(s1 - v1[0][0:1, :]) * (1.0 / z[0:1, :])

        @pl.when(jnp.max(jnp.where(tie, 1.0, 0.0)) > 0.0)
        def _():
            for c in range(tm // LANES):
                sl = slice(c * LANES, (c + 1) * LANES)
                s1, s2 = s1_scr[:, sl], s2_scr[:, sl]
                vals1, pos1 = _top16(s1)
                vals2, pos2 = _top16(s2)
                nsel, z = _pair_select(vals1, vals2)
                nfull = jnp.zeros(s1.shape, jnp.int32)
                for a in range(PEER_TOPK):
                    nfull = jnp.where(pos1 == a, nsel[a:a + 1, :], nfull)
                pos2_ref[hh, :, sl] = pos2.astype(F32).astype(BF16)
                m2_ref[hh, :, sl] = jnp.exp(s2 - vals2[0]).astype(BF16)
                nsel_ref[hh, :, sl] = nfull.astype(F32)
                m1_ref[hh, :, sl] = jnp.exp(s1 - vals1[0]) * (1.0 / z)

        return carry

    lax.fori_loop(0, PEER_HEADS, head, 0)


PEER_SELECT_TILE = 256


def _peer_select(x, g, w_query, key1, key2):
    t, d = x.shape
    tm = PEER_SELECT_TILE
    nq = w_query.shape[1]
    head = pl.BlockSpec((PEER_HEADS, PEER_KEYS, tm), lambda i: (0, 0, i))
    keyspec = pl.BlockSpec((PEER_KEYS, PEER_KEYS), lambda i: (0, 0))
    shp = lambda dt: jax.ShapeDtypeStruct((PEER_HEADS, PEER_KEYS, t), dt)
    return pl.pallas_call(
        functools.partial(_peer_select_kernel, tm=tm),
        grid=(t // tm,),
        in_specs=[pl.BlockSpec((tm, d), lambda i: (i, 0)), pl.BlockSpec((1, d), lambda i: (0, 0)),
                  pl.BlockSpec((d, nq), lambda i: (0, 0)), keyspec, keyspec],
        out_specs=[head] * 4,
        out_shape=[shp(BF16), shp(BF16), shp(F32), shp(F32)],
        scratch_shapes=[pltpu.VMEM((2 * PEER_HEADS, tm, PEER_KEYS), BF16),
                        pltpu.VMEM((PEER_KEYS, tm), F32), pltpu.VMEM((PEER_KEYS, tm), F32)],
        compiler_params=_params("parallel"),
        name="peer_select",
    )(x, g.reshape(1, d), w_query, key1, key2)


def _peer_dense_kernel(x_ref, g_ref, gf_ref, u_ref, vt_ref, pos2_ref, m2_ref, nsel_ref, m1_ref,
                       o_ref, h_scr, acc_scr, a_scr, z_scr, p2_scr, m2_scr, *, tm, te, final_norm):
    j = pl.program_id(1)

    @pl.when(j == 0)
    def _():
        h_scr[...] = _rms(x_ref[...], g_ref[...]).astype(BF16)
        acc_scr[...] = jnp.zeros(acc_scr.shape, F32)
        p2_scr[...] = pos2_ref[...]
        m2_scr[...] = m2_ref[...]

    a_scr[...] = lax.dot_general(u_ref[...], h_scr[...], _NT, preferred_element_type=F32)
    n1 = te // PEER_KEYS
    first = pl.multiple_of(j * n1, n1)
    for c in range(tm // LANES):
        sl = slice(c * LANES, (c + 1) * LANES)
        ns = [nsel_ref[hh, pl.ds(first, n1), sl].astype(BF16) for hh in range(PEER_HEADS)]
        mm = [m1_ref[hh, pl.ds(first, n1), sl].astype(BF16) for hh in range(PEER_HEADS)]
        for r in range(n1):
            rows = slice(r * PEER_KEYS, (r + 1) * PEER_KEYS)
            w = None
            for hh in range(PEER_HEADS):
                t = jnp.where(p2_scr[hh, :, sl] < ns[hh][r:r + 1, :],
                              m2_scr[hh, :, sl] * mm[hh][r:r + 1, :], 0)
                w = t if w is None else w + t
            z_scr[rows, sl] = _gelu(a_scr[rows, sl]).astype(BF16) * w
    acc_scr[...] += jnp.dot(vt_ref[...], z_scr[...], preferred_element_type=F32)

    @pl.when(j == pl.num_programs(1) - 1)
    def _():
        out = x_ref[...] + acc_scr[...].T
        if final_norm:
            out = _rms(out, gf_ref[...])
        o_ref[...] = out


def _peer_dense(x, g, g_final, u, vt, sel, final_norm):
    t, d = x.shape
    tm = min(ROW_TILE, t)
    te = EXPERT_TILE
    assert te // PEER_KEYS == SUBLANES
    row = pl.BlockSpec((tm, d), lambda i, j: (i, 0))
    vec = pl.BlockSpec((1, d), lambda i, j: (0, 0))
    head = pl.BlockSpec((PEER_HEADS, PEER_KEYS, tm), lambda i, j: (0, 0, i))
    return pl.pallas_call(
        functools.partial(_peer_dense_kernel, tm=tm, te=te, final_norm=final_norm),
        grid=(t // tm, PEER_EXPERTS // te),
        in_specs=[row, vec, vec, pl.BlockSpec((te, d), lambda i, j: (j, 0)),
                  pl.BlockSpec((d, te), lambda i, j: (0, j)), head, head, head, head],
        out_specs=row,
        out_shape=jax.ShapeDtypeStruct((t, d), F32),
        scratch_shapes=[pltpu.VMEM((tm, d), BF16), pltpu.VMEM((d, tm), F32),
                        pltpu.VMEM((te, tm), F32), pltpu.VMEM((te, tm), BF16),
                        pltpu.VMEM((PEER_HEADS, PEER_KEYS, tm), BF16),
                        pltpu.VMEM((PEER_HEADS, PEER_KEYS, tm), BF16)],
        compiler_params=_params("parallel", "arbitrary"),
        name="peer_dense",
    )(x, g.reshape(1, d), g_final.reshape(1, d), u, vt, *sel)


def _peer(x, g, g_final, w_query, key1, key2, u, vt, final_norm):
    shp = x.shape
    x2 = x.reshape(-1, shp[-1])
    sel = _peer_select(x2, g, w_query, key1, key2)
    return _peer_dense(x2, g, g_final, u, vt, sel, final_norm).reshape(shp)


def _trunk(x, ssm0, conv0, mem_k, mem_v, w, sample):
    b, s, d = x.shape
    chunk = s if sample else S5_CHUNK
    if ssm0 is None:
        ssm0 = (jnp.zeros((b, SSM_GROUPS, SSM_STATE), F32),) * 2
    x, ssm_re, ssm_im = _s5_mixer(x, ssm0[0], ssm0[1], w['norm_mix'][0], w['s5_fold'][chunk],
                                  w['ssm_d'][0], w['ssm_w_glu'][0], chunk)
    attn = _attn_sample if sample else _attn_prompt
    for i in range(2):
        if i == 1:
            if sample:
                x_tm = x.transpose(1, 0, 2).reshape(s * b, d)
                buf_tm = conv0.transpose(1, 0, 2).reshape(2 * b, d)
                x_tm, nbuf = _conv_sample(x_tm, buf_tm, w['norm_mix'][1], w['conv_w_in'][0],
                                          w['conv_w'][0], w['conv_w_out'][0], s, b)
                x = x_tm.reshape(s, b, d).transpose(1, 0, 2)
                conv_out = nbuf.reshape(2, b, d).transpose(1, 0, 2)
            else:
                x, tail = _conv_prompt(x, w['norm_mix'][1], w['conv_w_in'][0], w['conv_w'][0],
                                       w['conv_w_out'][0])
                conv_out = tail[:, CONV_HEAD - 2:]
        x = attn(x, w['norm_mem'][i], mem_k, mem_v, i, w['mem_w_q'][i], w['mem_w_o'][i])
        x = _peer(x, w['norm_ffn'][i], w['norm_final'], w['peer_w_query'][i], w['peer_key1'][i],
                  w['peer_key2'][i], w['peer_u'][i], w['peer_vt'][i], final_norm=(i == 1))
    return x, ssm_re[None], ssm_im[None], conv_out[None]


def kernel(x_prompt, x_sample, mem_prompt, state_ssm_re, state_ssm_im, state_conv, cache_mem_k, cache_mem_v, norm_mix, norm_mem, norm_ffn, norm_final, ssm_a_re, ssm_a_im, ssm_log_dt, ssm_b_re, ssm_b_im, ssm_c_re, ssm_c_im, ssm_d, ssm_w_glu, conv_w_in, conv_w, conv_w_out, mem_w_q, mem_w_k, mem_w_v, mem_w_o, peer_w_query, peer_key1, peer_key2, peer_u, peer_v):
    bsz, seq, d = x_prompt.shape
    dec_b, dec_s, _ = x_sample.shape
    depth = mem_w_q.shape[0]
    fold = lambda chunk: _s5_fold(ssm_a_re[0], ssm_a_im[0], ssm_log_dt[0], ssm_b_re[0], ssm_b_im[0],
                                  ssm_c_re[0], ssm_c_im[0], chunk)
    w = dict(
        norm_mix=norm_mix, norm_mem=norm_mem, norm_ffn=norm_ffn, norm_final=norm_final,
        s5_fold={S5_CHUNK: fold(S5_CHUNK), dec_s: fold(dec_s)},
        ssm_d=ssm_d, ssm_w_glu=ssm_w_glu.astype(BF16),
        conv_w_in=conv_w_in.astype(BF16), conv_w=conv_w, conv_w_out=conv_w_out.astype(BF16),
        mem_w_q=mem_w_q.astype(BF16), mem_w_o=mem_w_o.astype(BF16),
        peer_w_query=peer_w_query.astype(BF16), peer_key1=peer_key1.astype(BF16),
        peer_key2=peer_key2.astype(BF16), peer_u=peer_u.astype(BF16),
        peer_vt=peer_v.astype(BF16).transpose(0, 2, 1),
    )
    w_kv = jnp.concatenate([mem_w_k[i] for i in range(depth)] + [mem_w_v[i] for i in range(depth)],
                           axis=1).astype(BF16)
    kv = _mm(mem_prompt.reshape(bsz * MEM_TOKENS, d), w_kv, tm=256)
    kv = kv.reshape(bsz, MEM_TOKENS, 2 * depth, d).transpose(2, 0, 1, 3)
    mem_k_p, mem_v_p = kv[:depth], kv[depth:]

    y_p, re_p, im_p, conv_p = _trunk(x_prompt, None, None, mem_k_p, mem_v_p, w, sample=False)
    cache_k = cache_mem_k.reshape(depth, dec_b, MEM_TOKENS, d)
    cache_v = cache_mem_v.reshape(depth, dec_b, MEM_TOKENS, d)
    y_s, re_s, im_s, conv_s = _trunk(x_sample, (state_ssm_re[0], state_ssm_im[0]), state_conv[0],
                                     cache_k, cache_v, w, sample=True)
    head_shape = (depth, bsz, MEM_TOKENS, MEM_HEADS, MEM_HEAD_DIM)
    return (y_p, y_s, re_p, im_p, conv_p, mem_k_p.reshape(head_shape), mem_v_p.reshape(head_shape),
            re_s, im_s, conv_s)
```

```python
import functools
import math

import jax
import jax.numpy as jnp
from jax import lax
from jax.experimental import pallas as pl
from jax.experimental.pallas import tpu as pltpu

F32 = jnp.float32
BF16 = jnp.bfloat16

D_MODEL = 1024
SSM_GROUP = 16
SSM_GROUPS = D_MODEL // SSM_GROUP
SSM_STATE = 64
SSM_BLOCK_GROUPS = 8
SSM_BLOCKS = SSM_GROUPS // SSM_BLOCK_GROUPS
MEM_TOKENS = 256
MEM_HEADS = 4
MEM_HEAD_DIM = D_MODEL // MEM_HEADS
PEER_HEADS = 8
PEER_KEYS = 128
PEER_EXPERTS = PEER_KEYS * PEER_KEYS
PEER_TOPK = 16
RMS_EPS = 1e-6

LANES = 128
SUBLANES = 8
ROW_TILE = 512
S5_CHUNK = 8
S5_COL_TILE = 512
EXPERT_TILE = 1024
VMEM_LIMIT = 48 * 1024 * 1024

_NT = (((1,), (1,)), ((), ()))


def _params(*sem):
    return pltpu.CompilerParams(dimension_semantics=sem, vmem_limit_bytes=VMEM_LIMIT)


def _rms(x, g):
    r = lax.rsqrt(jnp.mean(x * x, axis=-1, keepdims=True) + RMS_EPS)
    return (x * r) * g


def _gelu(x):
    return 0.5 * x * (1.0 + lax.erf(x * (1.0 / math.sqrt(2.0))))


def _sigmoid(x):
    return 1.0 / (1.0 + jnp.exp(-x))


def _norm_cast_kernel(x_ref, g_ref, o_ref):
    o_ref[...] = _rms(x_ref[...], g_ref[...]).astype(BF16)


def _norm_cast(x, g):
    t, d = x.shape
    tm = min(ROW_TILE, t)
    return pl.pallas_call(
        _norm_cast_kernel,
        grid=(t // tm,),
        in_specs=[pl.BlockSpec((tm, d), lambda i: (i, 0)), pl.BlockSpec((1, d), lambda i: (0, 0))],
        out_specs=pl.BlockSpec((tm, d), lambda i: (i, 0)),
        out_shape=jax.ShapeDtypeStruct((t, d), BF16),
        compiler_params=_params("parallel"),
        name="norm_cast",
    )(x, g.reshape(1, d))


def _mm_kernel(*refs, has_norm, has_res):
    x_ref, w_ref = refs[0], refs[1]
    pos = 2
    x = x_ref[...]
    if has_norm:
        x = _rms(x, refs[pos][...])
        pos += 1
    y = jnp.dot(x.astype(BF16), w_ref[...], preferred_element_type=F32)
    if has_res:
        y = y + refs[pos][...]
        pos += 1
    refs[pos][...] = y


def _mm(x, w, g=None, res=None, tm=ROW_TILE):
    t, k = x.shape
    n = w.shape[1]
    tm = min(tm, t)
    args = [x, w]
    specs = [pl.BlockSpec((tm, k), lambda i: (i, 0)), pl.BlockSpec((k, n), lambda i: (0, 0))]
    if g is not None:
        args.append(g.reshape(1, k))
        specs.append(pl.BlockSpec((1, k), lambda i: (0, 0)))
    if res is not None:
        args.append(res)
        specs.append(pl.BlockSpec((tm, n), lambda i: (i, 0)))
    return pl.pallas_call(
        functools.partial(_mm_kernel, has_norm=g is not None, has_res=res is not None),
        grid=(t // tm,),
        in_specs=specs,
        out_specs=pl.BlockSpec((tm, n), lambda i: (i, 0)),
        out_shape=jax.ShapeDtypeStruct((t, n), F32),
        compiler_params=_params("parallel"),
        name="mm_rows",
    )(*args)


def _s5_fold(a_re, a_im, log_dt, b_re, b_im, c_re, c_im, chunk):
    hi = lax.Precision.HIGHEST
    dt = jnp.exp(log_dt)[:, None]
    mag = jnp.exp(a_re * dt)
    ang = a_im * dt
    lb_re = mag * jnp.cos(ang)
    lb_im = mag * jnp.sin(ang)
    den = a_re * a_re + a_im * a_im
    f_re = ((lb_re - 1.0) * a_re + lb_im * a_im) / den
    f_im = (lb_im * a_re - (lb_re - 1.0) * a_im) / den
    bb_re = f_re[..., None] * b_re - f_im[..., None] * b_im
    bb_im = f_re[..., None] * b_im + f_im[..., None] * b_re
    pw_re, pw_im = [jnp.ones_like(lb_re)], [jnp.zeros_like(lb_im)]
    for _ in range(chunk):
        pr, pi = pw_re[-1], pw_im[-1]
        pw_re.append(pr * lb_re - pi * lb_im)
        pw_im.append(pr * lb_im + pi * lb_re)
    pw_re, pw_im = jnp.stack(pw_re), jnp.stack(pw_im)
    cl_re = c_re[None] * pw_re[:, :, None, :] - c_im[None] * pw_im[:, :, None, :]
    cl_im = c_re[None] * pw_im[:, :, None, :] + c_im[None] * pw_re[:, :, None, :]
    kern = (jnp.einsum('kgdp,gpc->kgdc', cl_re[:chunk], bb_re, precision=hi)
            - jnp.einsum('kgdp,gpc->kgdc', cl_im[:chunk], bb_im, precision=hi))
    step = jnp.arange(chunk)
    rp_re, rp_im = pw_re[chunk - 1 - step], pw_im[chunk - 1 - step]
    p_re = rp_re[..., None] * bb_re[None] - rp_im[..., None] * bb_im[None]
    p_im = rp_re[..., None] * bb_im[None] + rp_im[..., None] * bb_re[None]
    p_re = p_re.transpose(1, 0, 3, 2)
    p_im = p_im.transpose(1, 0, 3, 2)
    n_re = cl_re[1:].transpose(1, 3, 0, 2)
    n_im = (-cl_im[1:]).transpose(1, 3, 0, 2)

    nb, q = SSM_BLOCKS, SSM_BLOCK_GROUPS
    eye = jnp.eye(q, dtype=BF16)
    w = chunk * q * SSM_GROUP
    sl = q * SSM_STATE
    kq = kern.transpose(1, 3, 0, 2).astype(BF16).reshape(nb, q, SSM_GROUP, chunk, 1, SSM_GROUP)
    kq = (kq * eye[None, :, None, None, :, None]).reshape(nb, q, SSM_GROUP, chunk, q * SSM_GROUP)
    kq = jnp.pad(kq, ((0, 0), (0, 0), (0, 0), (chunk - 1, 0), (0, 0)))
    m = jnp.stack([kq[:, :, :, chunk - 1 - t:2 * chunk - 1 - t] for t in range(chunk)], axis=1)
    m = m.reshape(nb, w, w)

    def fold_p(z):
        z = z.astype(BF16).reshape(nb, q, chunk, SSM_GROUP, 1, SSM_STATE).transpose(0, 2, 1, 3, 4, 5)
        return (z * eye[None, None, :, None, :, None]).reshape(nb, w, sl)

    def fold_n(z):
        z = z.astype(BF16).reshape(nb, q, SSM_STATE, chunk, 1, SSM_GROUP)
        return (z * eye[None, :, None, None, :, None]).reshape(nb, sl, w)

    lam_re = pw_re[chunk].reshape(nb, 1, sl)
    lam_im = pw_im[chunk].reshape(nb, 1, sl)
    return m, fold_p(p_re), fold_p(p_im), fold_n(n_re), fold_n(n_im), lam_re, lam_im


def _s5_core_kernel(x_ref, m_ref, pre_ref, pim_ref, nre_ref, nim_ref, lre_ref, lim_ref,
                    s0re_ref, s0im_ref, y_ref, sre_ref, sim_ref, stre, stim, *, nchunks, rows):
    x = x_ref[0]

    @pl.when(pl.program_id(1) == 0)
    def _():
        stre[...] = jnp.dot(x, pre_ref[0], preferred_element_type=F32)
        stim[...] = jnp.dot(x, pim_ref[0], preferred_element_type=F32)
        width = stre.shape[1]
        lr = jnp.broadcast_to(lre_ref[0], (rows, width))
        li = jnp.broadcast_to(lim_ref[0], (rows, width))

        def body(k, carry):
            sr, si = carry
            off = pl.multiple_of(k * rows, rows)
            qr = stre[pl.ds(off, rows), :]
            qi = stim[pl.ds(off, rows), :]
            stre[pl.ds(off, rows), :] = sr
            stim[pl.ds(off, rows), :] = si
            return lr * sr - li * si + qr, lr * si + li * sr + qi

        sr, si = lax.fori_loop(0, nchunks, body, (s0re_ref[0], s0im_ref[0]))
        sre_ref[0] = sr
        sim_ref[0] = si

    y = jnp.dot(x, m_ref[0], preferred_element_type=F32)
    y = y + jnp.dot(stre[...].astype(BF16), nre_ref[0], preferred_element_type=F32)
    y = y + jnp.dot(stim[...].astype(BF16), nim_ref[0], preferred_element_type=F32)
    y_ref[0] = y


def _s5_core(xb, folded, s0_re, s0_im, nchunks, rows):
    m, p_re, p_im, n_re, n_im, lam_re, lam_im = folded
    nblk, nr, w = xb.shape
    sl = SSM_BLOCK_GROUPS * SSM_STATE
    wc = min(w, S5_COL_TILE)
    fixed = lambda a, b: pl.BlockSpec((1, a, b), lambda i, j: (i, 0, 0))
    cols = lambda a: pl.BlockSpec((1, a, wc), lambda i, j: (i, 0, j))
    return pl.pallas_call(
        functools.partial(_s5_core_kernel, nchunks=nchunks, rows=rows),
        grid=(nblk, w // wc),
        in_specs=[fixed(nr, w), cols(w), fixed(w, sl), fixed(w, sl), cols(sl), cols(sl),
                  fixed(1, sl), fixed(1, sl), fixed(rows, sl), fixed(rows, sl)],
        out_specs=[cols(nr), fixed(rows, sl), fixed(rows, sl)],
        out_shape=[jax.ShapeDtypeStruct((nblk, nr, w), F32),
                   jax.ShapeDtypeStruct((nblk, rows, sl), F32),
                   jax.ShapeDtypeStruct((nblk, rows, sl), F32)],
        scratch_shapes=[pltpu.VMEM((nr, sl), F32)] * 2,
        compiler_params=_params("parallel", "arbitrary"),
        name="s5_core",
    )(xb, m, p_re, p_im, n_re, n_im, lam_re, lam_im, s0_re, s0_im)


def _s5_out_kernel(x_ref, y_ref, g_ref, d_ref, w_ref, o_ref):
    x = x_ref[...]
    h = _rms(x, g_ref[...])
    z = _gelu(y_ref[...] + d_ref[...] * h)
    gg = jnp.dot(z.astype(BF16), w_ref[...], preferred_element_type=F32)
    o_ref[...] = x + gg[:, :D_MODEL] * _sigmoid(gg[:, D_MODEL:])


def _s5_out(x, y, g, d, w_glu):
    t = x.shape[0]
    tm = min(ROW_TILE, t)
    row = pl.BlockSpec((tm, D_MODEL), lambda i: (i, 0))
    vec = pl.BlockSpec((1, D_MODEL), lambda i: (0, 0))
    return pl.pallas_call(
        _s5_out_kernel,
        grid=(t // tm,),
        in_specs=[row, row, vec, vec, pl.BlockSpec((D_MODEL, 2 * D_MODEL), lambda i: (0, 0))],
        out_specs=row,
        out_shape=jax.ShapeDtypeStruct((t, D_MODEL), F32),
        compiler_params=_params("parallel"),
        name="s5_out",
    )(x, y, g.reshape(1, -1), d.reshape(1, -1), w_glu)


def _s5_mixer(x, s0_re, s0_im, g, folded, d_skip, w_glu, chunk):
    b, s, d = x.shape
    nk = s // chunk
    x2 = x.reshape(b * s, d)
    h = _norm_cast(x2, g)
    xb = (h.reshape(b, nk, chunk, SSM_BLOCKS, LANES).transpose(3, 1, 0, 2, 4)
          .reshape(SSM_BLOCKS, nk * b, chunk * LANES))
    sl = SSM_BLOCK_GROUPS * SSM_STATE
    to_block = lambda z: z.reshape(b, SSM_BLOCKS, sl).transpose(1, 0, 2)
    y, sr, si = _s5_core(xb, folded, to_block(s0_re), to_block(s0_im), nk, b)
    y = y.reshape(SSM_BLOCKS, nk, b, chunk, LANES).transpose(2, 1, 3, 0, 4).reshape(b * s, d)
    from_block = lambda z: z.transpose(1, 0, 2).reshape(b, SSM_GROUPS, SSM_STATE)
    out = _s5_out(x2, y, g, d_skip, w_glu)
    return out.reshape(b, s, d), from_block(sr), from_block(si)


CONV_HEAD = 8


def _conv_prompt_kernel(x_ref, g_ref, win_ref, wc_ref, wout_ref, o_ref, tail_ref, vbuf, *, tm):
    @pl.when(pl.program_id(1) == 0)
    def _():
        vbuf[0:CONV_HEAD, :] = jnp.zeros((CONV_HEAD, D_MODEL), F32)

    x = x_ref[0]
    h = _rms(x, g_ref[...])
    p = jnp.dot(h.astype(BF16), win_ref[...], preferred_element_type=F32)
    bg = p[:, :D_MODEL]
    v = p[:, D_MODEL:2 * D_MODEL] * p[:, 2 * D_MODEL:]
    vbuf[CONV_HEAD:CONV_HEAD + tm, :] = v
    v2 = vbuf[CONV_HEAD - 2:CONV_HEAD - 2 + tm, :]
    v1 = vbuf[CONV_HEAD - 1:CONV_HEAD - 1 + tm, :]
    conv = wc_ref[0:1, :] * v2 + wc_ref[1:2, :] * v1 + wc_ref[2:3, :] * v
    out = jnp.dot((bg * conv).astype(BF16), wout_ref[...], preferred_element_type=F32)
    o_ref[0] = x + out
    tail = vbuf[tm:tm + CONV_HEAD, :]
    vbuf[0:CONV_HEAD, :] = tail
    tail_ref[0] = tail


def _conv_prompt(x, g, w_in, w_conv, w_out):
    b, s, d = x.shape
    tm = min(ROW_TILE, s)
    return pl.pallas_call(
        functools.partial(_conv_prompt_kernel, tm=tm),
        grid=(b, s // tm),
        in_specs=[pl.BlockSpec((1, tm, d), lambda i, j: (i, j, 0)),
                  pl.BlockSpec((1, d), lambda i, j: (0, 0)),
                  pl.BlockSpec((d, 3 * d), lambda i, j: (0, 0)),
                  pl.BlockSpec((3, d), lambda i, j: (0, 0)),
                  pl.BlockSpec((d, d), lambda i, j: (0, 0))],
        out_specs=[pl.BlockSpec((1, tm, d), lambda i, j: (i, j, 0)),
                   pl.BlockSpec((1, CONV_HEAD, d), lambda i, j: (i, 0, 0))],
        out_shape=[jax.ShapeDtypeStruct((b, s, d), F32),
                   jax.ShapeDtypeStruct((b, CONV_HEAD, d), F32)],
        scratch_shapes=[pltpu.VMEM((CONV_HEAD + tm, d), F32)],
        compiler_params=_params("parallel", "arbitrary"),
        name="conv_prompt",
    )(x, g.reshape(1, d), w_in, w_conv, w_out)


def _conv_sample_kernel(x_ref, buf_ref, g_ref, win_ref, wc_ref, wout_ref, o_ref, nbuf_ref, *, steps, nb):
    x = x_ref[...]
    h = _rms(x, g_ref[...])
    p = jnp.dot(h.astype(BF16), win_ref[...], preferred_element_type=F32)
    bg = p[:, :D_MODEL]
    v = p[:, D_MODEL:2 * D_MODEL] * p[:, 2 * D_MODEL:]
    vp = [buf_ref[0:nb, :], buf_ref[nb:2 * nb, :]] + [v[t * nb:(t + 1) * nb, :] for t in range(steps)]
    conv = jnp.concatenate(
        [wc_ref[0:1, :] * vp[t] + wc_ref[1:2, :] * vp[t + 1] + wc_ref[2:3, :] * vp[t + 2]
         for t in range(steps)], axis=0)
    out = jnp.dot((bg * conv).astype(BF16), wout_ref[...], preferred_element_type=F32)
    o_ref[...] = x + out
    nbuf_ref[0:nb, :] = vp[-2]
    nbuf_ref[nb:2 * nb, :] = vp[-1]


def _conv_sample(x_tm, buf_tm, g, w_in, w_conv, w_out, steps, nb):
    d = D_MODEL
    full = lambda r, c: pl.BlockSpec((r, c), lambda i: (0, 0))
    return pl.pallas_call(
        functools.partial(_conv_sample_kernel, steps=steps, nb=nb),
        grid=(1,),
        in_specs=[full(steps * nb, d), full(2 * nb, d), full(1, d), full(d, 3 * d), full(3, d), full(d, d)],
        out_specs=[full(steps * nb, d), full(2 * nb, d)],
        out_shape=[jax.ShapeDtypeStruct((steps * nb, d), F32), jax.ShapeDtypeStruct((2 * nb, d), F32)],
        compiler_params=_params("arbitrary"),
        name="conv_sample",
    )(x_tm, buf_tm, g.reshape(1, d), w_in, w_conv, w_out)


def _softmax_rows(s):
    e = jnp.exp(s - jnp.max(s, axis=-1, keepdims=True))
    return e / jnp.sum(e, axis=-1, keepdims=True)


def _attn_heads(q, k, v):
    k = k.astype(BF16)
    v = v.astype(BF16)
    outs = []
    for hd in range(MEM_HEADS):
        sl = slice(hd * MEM_HEAD_DIM, (hd + 1) * MEM_HEAD_DIM)
        s = lax.dot_general(q[:, sl].astype(BF16), k[:, sl], _NT, preferred_element_type=F32)
        p = _softmax_rows(s * (MEM_HEAD_DIM ** -0.5))
        outs.append(jnp.dot(p.astype(BF16), v[:, sl], preferred_element_type=F32))
    return jnp.concatenate(outs, axis=-1)


def _attn_prompt_kernel(x_ref, g_ref, k_ref, v_ref, wq_ref, wo_ref, o_ref):
    x = x_ref[0]
    h = _rms(x, g_ref[...])
    q = jnp.dot(h.astype(BF16), wq_ref[...], preferred_element_type=F32)
    o = _attn_heads(q, k_ref[0], v_ref[0])
    o_ref[0] = x + jnp.dot(o.astype(BF16), wo_ref[...], preferred_element_type=F32)


def _attn_prompt(x, g, mem_k, mem_v, layer, w_q, w_o):
    b, s, d = x.shape
    tm = min(ROW_TILE, s)
    mem = pl.BlockSpec((None, 1, MEM_TOKENS, d), lambda i, j: (layer, i, 0, 0))
    wgt = pl.BlockSpec((d, d), lambda i, j: (0, 0))
    row = pl.BlockSpec((1, tm, d), lambda i, j: (i, j, 0))
    return pl.pallas_call(
        _attn_prompt_kernel,
        grid=(b, s // tm),
        in_specs=[row, pl.BlockSpec((1, d), lambda i, j: (0, 0)), mem, mem, wgt, wgt],
        out_specs=row,
        out_shape=jax.ShapeDtypeStruct((b, s, d), F32),
        compiler_params=_params("parallel", "parallel"),
        name="attn_prompt",
    )(x, g.reshape(1, d), mem_k, mem_v, w_q, w_o)


SAMPLE_Q_ROWS = 8
SAMPLE_SEQ_BLOCK = 4


def _attn_sample_kernel(q_ref, k_ref, v_ref, o_ref):
    def body(i, carry):
        o_ref[i] = _attn_heads(q_ref[i], k_ref[i], v_ref[i])
        return carry

    lax.fori_loop(0, SAMPLE_SEQ_BLOCK, body, 0)


def _attn_sample(x, g, cache_k, cache_v, layer, w_q, w_o):
    b, s, d = x.shape
    x2 = x.reshape(b * s, d)
    q = _mm(x2, w_q, g=g).reshape(b, s, d)
    q = jnp.pad(q, ((0, 0), (0, SAMPLE_Q_ROWS - s), (0, 0)))
    qblk = pl.BlockSpec((SAMPLE_SEQ_BLOCK, SAMPLE_Q_ROWS, d), lambda i: (i, 0, 0))
    mblk = pl.BlockSpec((None, SAMPLE_SEQ_BLOCK, MEM_TOKENS, d), lambda i: (layer, i, 0, 0))
    o = pl.pallas_call(
        _attn_sample_kernel,
        grid=(b // SAMPLE_SEQ_BLOCK,),
        in_specs=[qblk, mblk, mblk],
        out_specs=qblk,
        out_shape=jax.ShapeDtypeStruct((b, SAMPLE_Q_ROWS, d), F32),
        compiler_params=_params("parallel"),
        name="attn_sample",
    )(q, cache_k, cache_v)
    o = o[:, :s].reshape(b * s, d)
    return _mm(o, w_o, res=x2).reshape(b, s, d)


def _top16(s):
    iota = lax.broadcasted_iota(jnp.int32, s.shape, 0)
    pos = jnp.full(s.shape, PEER_TOPK, jnp.int32)
    vals = []
    for i in range(PEER_TOPK):
        m = jnp.max(s, axis=0, keepdims=True)
        idx = jnp.min(jnp.where(s == m, iota, s.shape[0]), axis=0, keepdims=True)
        hit = iota == idx
        pos = jnp.where(hit, i, pos)
        s = jnp.where(hit, -jnp.inf, s)
        vals.append(m)
    return vals, pos


def _pair_select(vals1, vals2):
    lanes = vals1[0].shape[1]
    iota16 = lax.broadcasted_iota(jnp.int32, (PEER_TOPK, lanes), 0)
    v2 = jnp.zeros((PEER_TOPK, lanes), F32)
    for b in range(PEER_TOPK):
        v2 = jnp.where(iota16 == b, vals2[b], v2)
    iota8 = iota16[:8]
    cand = [vals1[0] + v2]
    flat = [iota16]
    for a in range(1, PEER_TOPK):
        cand.append(jnp.where(iota8 < PEER_TOPK // (a + 1), vals1[a] + v2[:8], -jnp.inf))
        flat.append(iota8 + a * PEER_TOPK)
    nsel = jnp.zeros((PEER_TOPK, lanes), jnp.int32)
    z = jnp.zeros((1, lanes), F32)
    top = vals1[0] + vals2[0]
    nflat = PEER_TOPK * PEER_TOPK
    for _ in range(PEER_TOPK):
        mx = jnp.maximum(cand[0][:8], cand[0][8:])
        for a in range(1, PEER_TOPK):
            mx = jnp.maximum(mx, cand[a])
        m = jnp.max(mx, axis=0, keepdims=True)
        w0 = jnp.where(cand[0] == m, flat[0], nflat)
        mi = jnp.minimum(w0[:8], w0[8:])
        for a in range(1, PEER_TOPK):
            mi = jnp.minimum(mi, jnp.where(cand[a] == m, flat[a], nflat))
        idx = jnp.min(mi, axis=0, keepdims=True)
        cand = [jnp.where(flat[a] == idx, -jnp.inf, cand[a]) for a in range(PEER_TOPK)]
        nsel = nsel + (iota16 == (idx >> 4)).astype(jnp.int32)
        z = z + jnp.exp(m - top)
    return nsel, z


def _sort_network(n):
    pairs = []

    def merge(lo, cnt, r):
        m = r * 2
        if m < cnt:
            merge(lo, cnt, m)
            merge(lo + r, cnt, m)
            for i in range(lo + r, lo + cnt - r, m):
                pairs.append((i, i + r))
        else:
            pairs.append((lo, lo + r))

    def sort(lo, cnt):
        if cnt > 1:
            m = cnt // 2
            sort(lo, m)
            sort(lo + m, m)
            merge(lo, cnt, 1)

    sort(0, n)
    return tuple(pairs)


_SORT16 = _sort_network(PEER_TOPK)


def _exchange(v, i, j):
    v[i], v[j] = jnp.maximum(v[i], v[j]), jnp.minimum(v[i], v[j])


def _allreduce_rows(x, op):
    for shift in (4, 2, 1):
        x = op(x, pltpu.roll(x, shift, 0))
    return x


def _sorted_top16(s):
    v = [s[SUBLANES * j:SUBLANES * (j + 1), :] for j in range(PEER_TOPK)]
    for i, j in _SORT16:
        _exchange(v, i, j)
    for shift in (4, 2, 1):
        r = [pltpu.roll(x, shift, 0) for x in v]
        v = [jnp.maximum(v[i], r[PEER_TOPK - 1 - i]) for i in range(PEER_TOPK)]
        for stride in (8, 4, 2, 1):
            for i in range(PEER_TOPK):
                if i & stride == 0:
                    _exchange(v, i, i + stride)
    return v


def _rank_bits(x, v):
    b3 = v[7] > x
    b2 = jnp.where(b3, v[11], v[3]) > x
    b1 = jnp.where(b3, jnp.where(b2, v[13], v[9]), jnp.where(b2, v[5], v[1])) > x
    t = jnp.where(b3,
                  jnp.where(b2, jnp.where(b1, v[14], v[12]), jnp.where(b1, v[10], v[8])),
                  jnp.where(b2, jnp.where(b1, v[6], v[4]), jnp.where(b1, v[2], v[0])))
    return (b3, b2, b1, t > x), v[PEER_TOPK - 1] > x


def _select16(bits, vals):
    b3, b2, b1, b0 = bits
    lvl = [jnp.where(b0, vals[2 * i + 1], vals[2 * i]) for i in range(8)]
    lvl = [jnp.where(b1, lvl[2 * i + 1], lvl[2 * i]) for i in range(4)]
    lvl = [jnp.where(b2, lvl[2 * i + 1], lvl[2 * i]) for i in range(2)]
    return jnp.where(b3, lvl[1], lvl[0])


def _pair_counts(v1, v2):
    sub = lax.broadcasted_iota(jnp.int32, v1[0].shape, 0)
    ninf = -jnp.inf

    def column(v, off):
        col = v[off]
        for b in range(1, SUBLANES):
            col = jnp.where(sub == b, v[off + b], col)
        return col

    v2c0, v2c1 = column(v2, 0), column(v2, SUBLANES)
    v1c0, v1c1 = column(v1, 0), column(v1, SUBLANES)
    rows = [[v1[0] + v2c0, v1[0] + v2c1], [v1[1] + v2c0],
            [jnp.where(sub < 5, v1[2] + v2c0, ninf)], [jnp.where(sub < 4, v1[3] + v2c0, ninf)]]
    cols = [jnp.where(sub >= 4, v1c0 + v2[0], ninf), v1c1 + v2[0],
            jnp.where(sub >= 4, v1c0 + v2[1], ninf), jnp.where(sub == 4, v1c0 + v2[2], ninf)]
    cur = [c for r in rows for c in r] + cols
    top = v1[0] + v2[0]
    z = jnp.zeros_like(top)
    for _ in range(PEER_TOPK):
        m = cur[0]
        for c in cur[1:]:
            m = jnp.maximum(m, c)
        m = _allreduce_rows(m, jnp.maximum)
        z = z + jnp.exp(m - top)
        cur = [jnp.where(c == m, ninf, c) for c in cur]
    tau = m
    count = lambda c: jnp.where(c >= tau, 1.0, 0.0)
    nsel = []
    for r in rows:
        f = count(r[0])
        for c in r[1:]:
            f = f + count(c)
        nsel.append(_allreduce_rows(f, jnp.add))
    for a in range(4, PEER_TOPK):
        f = count(v1[a] + v2[0])
        for b in range(1, PEER_TOPK // (a + 1)):
            f = f + count(v1[a] + v2[b])
        nsel.append(f)
    total = nsel[0]
    for f in nsel[1:]:
        total = total + f
    return nsel, z, total != float(PEER_TOPK)


def _key_ranks(s, v, lookup=None):
    outs = []
    ranked = None
    for j in range(PEER_TOPK):
        x = s[SUBLANES * j:SUBLANES * (j + 1), :]
        bits, low = _rank_bits(x, v)
        if lookup is None:
            b3, b2, b1, b0 = bits
            val = (jnp.where(b3, 8.0, 0.0) + jnp.where(b2, 4.0, 0.0)
                   + jnp.where(b1, 2.0, 0.0) + jnp.where(b0, 1.0, 0.0) + jnp.where(low, 1.0, 0.0))
        else:
            val = jnp.where(low, 0.0, _select16(bits, lookup))
        outs.append(val)
        r = jnp.where(low, 0.0, 1.0)
        ranked = r if ranked is None else ranked + r
    tie = _allreduce_rows(ranked, jnp.add) != float(PEER_TOPK)
    for a in range(PEER_TOPK - 1):
        tie = tie | (v[a] == v[a + 1])
    return jnp.concatenate(outs, axis=0), tie


def _peer_select_kernel(x_ref, g_ref, wq_ref, k1_ref, k2_ref,
                        pos2_ref, m2_ref, nsel_ref, m1_ref, q_scr, s1_scr, s2_scr, *, tm):
    h = _rms(x_ref[...], g_ref[...]).astype(BF16)
    q = jnp.dot(h, wq_ref[...], preferred_element_type=F32)
    for j in range(2 * PEER_HEADS):
        q_scr[j] = q[:, j * PEER_KEYS:(j + 1) * PEER_KEYS].astype(BF16)
    k1 = k1_ref[...]
    k2 = k2_ref[...]

    def head(hh, carry):
        s1_scr[...] = lax.dot_general(k1, q_scr[2 * hh], _NT, preferred_element_type=F32)
        s2_scr[...] = lax.dot_general(k2, q_scr[2 * hh + 1], _NT, preferred_element_type=F32)
        tie = None
        for c in range(tm // LANES):
            sl = slice(c * LANES, (c + 1) * LANES)
            s1, s2 = s1_scr[:, sl], s2_scr[:, sl]
            v1 = _sorted_top16(s1)
            v2 = _sorted_top16(s2)
            nsel, z, t0 = _pair_counts(v1, v2)
            pos2, t2 = _key_ranks(s2, v2)
            nfull, t1 = _key_ranks(s1, v1, lookup=nsel)
            t = t0 | t1 | t2
            tie = t if tie is None else tie | t
            pos2_ref[hh, :, sl] = pos2.astype(BF16)
            m2_ref[hh, :, sl] = jnp.exp(s2 - v2[0][0:1, :]).astype(BF16)
            nsel_ref[hh, :, sl] = nfull
            m1_ref[hh, :, sl] = jnp.exp(s1 - v1[0][0:1, :]) * (1.0 / z[0:1, :])

        @pl.when(jnp.max(jnp.where(tie, 1.0, 0.0)) > 0.0)
        def _():
            for c in range(tm // LANES):
                sl = slice(c * LANES, (c + 1) * LANES)
                s1, s2 = s1_scr[:, sl], s2_scr[:, sl]
                vals1, pos1 = _top16(s1)
                vals2, pos2 = _top16(s2)
                nsel, z = _pair_select(vals1, vals2)
                nfull = jnp.zeros(s1.shape, jnp.int32)
                for a in range(PEER_TOPK):
                    nfull = jnp.where(pos1 == a, nsel[a:a + 1, :], nfull)
                pos2_ref[hh, :, sl] = pos2.astype(F32).astype(BF16)
                m2_ref[hh, :, sl] = jnp.exp(s2 - vals2[0]).astype(BF16)
                nsel_ref[hh, :, sl] = nfull.astype(F32)
                m1_ref[hh, :, sl] = jnp.exp(s1 - vals1[0]) * (1.0 / z)

        return carry

    lax.fori_loop(0, PEER_HEADS, head, 0)


PEER_SELECT_TILE = 256


def _peer_select(x, g, w_query, key1, key2):
    t, d = x.shape
    tm = PEER_SELECT_TILE
    nq = w_query.shape[1]
    head = pl.BlockSpec((PEER_HEADS, PEER_KEYS, tm), lambda i: (0, 0, i))
    keyspec = pl.BlockSpec((PEER_KEYS, PEER_KEYS), lambda i: (0, 0))
    shp = lambda dt: jax.ShapeDtypeStruct((PEER_HEADS, PEER_KEYS, t), dt)
    return pl.pallas_call(
        functools.partial(_peer_select_kernel, tm=tm),
        grid=(t // tm,),
        in_specs=[pl.BlockSpec((tm, d), lambda i: (i, 0)), pl.BlockSpec((1, d), lambda i: (0, 0)),
                  pl.BlockSpec((d, nq), lambda i: (0, 0)), keyspec, keyspec],
        out_specs=[head] * 4,
        out_shape=[shp(BF16), shp(BF16), shp(F32), shp(F32)],
        scratch_shapes=[pltpu.VMEM((2 * PEER_HEADS, tm, PEER_KEYS), BF16),
                        pltpu.VMEM((PEER_KEYS, tm), F32), pltpu.VMEM((PEER_KEYS, tm), F32)],
        compiler_params=_params("parallel"),
        name="peer_select",
    )(x, g.reshape(1, d), w_query, key1, key2)


def _zero_after(t):
    bits = pltpu.bitcast(t[0:2 * SUBLANES, :], jnp.uint32)
    return jnp.max(((bits >> 16) >> 16).astype(jnp.int32)).astype(F32).astype(BF16)


PEER_GATE_ROWS = 1


def _peer_dense_kernel(x_ref, g_ref, gf_ref, u_ref, vt_ref, pos2_ref, m2_ref, nsel_ref, m1_ref,
                       o_ref, h_scr, acc_scr, p2_scr, m2_scr, *, tm, te, final_norm):
    j = pl.program_id(1)

    @pl.when(j == 0)
    def _():
        h_scr[...] = _rms(x_ref[...], g_ref[...]).astype(BF16)
        acc_scr[...] = jnp.zeros(acc_scr.shape, F32)
        p2_scr[...] = pos2_ref[...]
        m2_scr[...] = m2_ref[...]

    n1 = te // PEER_KEYS
    first = pl.multiple_of(j * n1, n1)
    half = n1 // 2

    def activations(r0, r1, gate=None):
        hop = h_scr[...]
        if gate is not None:
            hop = hop + gate
        return lax.dot_general(u_ref[r0 * PEER_KEYS:r1 * PEER_KEYS, :], hop, _NT, preferred_element_type=F32)

    def weighted(a, base, r0, r1):
        blocks = []
        for r in range(r0, r1):
            tiles = []
            for ct in range(tm // LANES):
                sl = slice(ct * LANES, (ct + 1) * LANES)
                w = None
                for hh in range(PEER_HEADS):
                    nrow = nsel_ref[hh, pl.ds(first, n1), sl][r:r + 1, :].astype(BF16)
                    mrow = m1_ref[hh, pl.ds(first, n1), sl][r:r + 1, :].astype(BF16)
                    t = jnp.where(p2_scr[hh, :, sl] < nrow, m2_scr[hh, :, sl] * mrow, 0)
                    w = t if w is None else w + t
                at = a[(r - base) * PEER_KEYS:(r - base + 1) * PEER_KEYS, sl]
                tiles.append(_gelu(at).astype(BF16) * w)
            blocks.append(jnp.concatenate(tiles, axis=1))
        return blocks

    a_top = activations(0, half)
    z = weighted(a_top, 0, 0, PEER_GATE_ROWS)
    a_bot = activations(half, n1, gate=_zero_after(z[-1]))
    z = z + weighted(a_top, 0, PEER_GATE_ROWS, half) + weighted(a_bot, half, half, n1)
    acc_scr[...] += jnp.dot(vt_ref[...], jnp.concatenate(z, axis=0), preferred_element_type=F32)

    @pl.when(j == pl.num_programs(1) - 1)
    def _():
        out = x_ref[...] + acc_scr[...].T
        if final_norm:
            out = _rms(out, gf_ref[...])
        o_ref[...] = out


def _peer_dense(x, g, g_final, u, vt, sel, final_norm):
    t, d = x.shape
    tm = min(ROW_TILE, t)
    te = vt.shape[2]
    assert te // PEER_KEYS == SUBLANES
    row = pl.BlockSpec((tm, d), lambda i, j: (i, 0))
    vec = pl.BlockSpec((1, d), lambda i, j: (0, 0))
    head = pl.BlockSpec((PEER_HEADS, PEER_KEYS, tm), lambda i, j: (0, 0, i))
    return pl.pallas_call(
        functools.partial(_peer_dense_kernel, tm=tm, te=te, final_norm=final_norm),
        grid=(t // tm, PEER_EXPERTS // te),
        in_specs=[row, vec, vec, pl.BlockSpec((te, d), lambda i, j: (j, 0)),
                  pl.BlockSpec((None, d, te), lambda i, j: (j, 0, 0)), head, head, head, head],
        out_specs=row,
        out_shape=jax.ShapeDtypeStruct((t, d), F32),
        scratch_shapes=[pltpu.VMEM((tm, d), BF16), pltpu.VMEM((d, tm), F32),
                        pltpu.VMEM((PEER_HEADS, PEER_KEYS, tm), BF16),
                        pltpu.VMEM((PEER_HEADS, PEER_KEYS, tm), BF16)],
        compiler_params=_params("parallel", "arbitrary"),
        name="peer_dense",
    )(x, g.reshape(1, d), g_final.reshape(1, d), u, vt, *sel)


def _peer(x, g, g_final, w_query, key1, key2, u, vt, final_norm):
    shp = x.shape
    x2 = x.reshape(-1, shp[-1])
    sel = _peer_select(x2, g, w_query, key1, key2)
    return _peer_dense(x2, g, g_final, u, vt, sel, final_norm).reshape(shp)


def _trunk(x, ssm0, conv0, mem_k, mem_v, w, sample):
    b, s, d = x.shape
    chunk = s if sample else S5_CHUNK
    if ssm0 is None:
        ssm0 = (jnp.zeros((b, SSM_GROUPS, SSM_STATE), F32),) * 2
    x, ssm_re, ssm_im = _s5_mixer(x, ssm0[0], ssm0[1], w['norm_mix'][0], w['s5_fold'][chunk],
                                  w['ssm_d'][0], w['ssm_w_glu'][0], chunk)
    attn = _attn_sample if sample else _attn_prompt
    for i in range(2):
        if i == 1:
            if sample:
                x_tm = x.transpose(1, 0, 2).reshape(s * b, d)
                buf_tm = conv0.transpose(1, 0, 2).reshape(2 * b, d)
                x_tm, nbuf = _conv_sample(x_tm, buf_tm, w['norm_mix'][1], w['conv_w_in'][0],
                                          w['conv_w'][0], w['conv_w_out'][0], s, b)
                x = x_tm.reshape(s, b, d).transpose(1, 0, 2)
                conv_out = nbuf.reshape(2, b, d).transpose(1, 0, 2)
            else:
                x, tail = _conv_prompt(x, w['norm_mix'][1], w['conv_w_in'][0], w['conv_w'][0],
                                       w['conv_w_out'][0])
                conv_out = tail[:, CONV_HEAD - 2:]
        x = attn(x, w['norm_mem'][i], mem_k, mem_v, i, w['mem_w_q'][i], w['mem_w_o'][i])
        x = _peer(x, w['norm_ffn'][i], w['norm_final'], w['peer_w_query'][i], w['peer_key1'][i],
                  w['peer_key2'][i], w['peer_u'][i], w['peer_vt'][i], final_norm=(i == 1))
    return x, ssm_re[None], ssm_im[None], conv_out[None]


def kernel(x_prompt, x_sample, mem_prompt, state_ssm_re, state_ssm_im, state_conv, cache_mem_k, cache_mem_v, norm_mix, norm_mem, norm_ffn, norm_final, ssm_a_re, ssm_a_im, ssm_log_dt, ssm_b_re, ssm_b_im, ssm_c_re, ssm_c_im, ssm_d, ssm_w_glu, conv_w_in, conv_w, conv_w_out, mem_w_q, mem_w_k, mem_w_v, mem_w_o, peer_w_query, peer_key1, peer_key2, peer_u, peer_v):
    bsz, seq, d = x_prompt.shape
    dec_b, dec_s, _ = x_sample.shape
    depth = mem_w_q.shape[0]
    fold = lambda chunk: _s5_fold(ssm_a_re[0], ssm_a_im[0], ssm_log_dt[0], ssm_b_re[0], ssm_b_im[0],
                                  ssm_c_re[0], ssm_c_im[0], chunk)
    w = dict(
        norm_mix=norm_mix, norm_mem=norm_mem, norm_ffn=norm_ffn, norm_final=norm_final,
        s5_fold={S5_CHUNK: fold(S5_CHUNK), dec_s: fold(dec_s)},
        ssm_d=ssm_d, ssm_w_glu=ssm_w_glu.astype(BF16),
        conv_w_in=conv_w_in.astype(BF16), conv_w=conv_w, conv_w_out=conv_w_out.astype(BF16),
        mem_w_q=mem_w_q.astype(BF16), mem_w_o=mem_w_o.astype(BF16),
        peer_w_query=peer_w_query.astype(BF16), peer_key1=peer_key1.astype(BF16),
        peer_key2=peer_key2.astype(BF16), peer_u=peer_u.astype(BF16),
        peer_vt=peer_v.astype(BF16).reshape(depth, PEER_EXPERTS // EXPERT_TILE, EXPERT_TILE, d)
        .transpose(0, 1, 3, 2),
    )
    w_kv = jnp.concatenate([mem_w_k[i] for i in range(depth)] + [mem_w_v[i] for i in range(depth)],
                           axis=1).astype(BF16)
    kv = _mm(mem_prompt.reshape(bsz * MEM_TOKENS, d), w_kv, tm=256)
    kv = kv.reshape(bsz, MEM_TOKENS, 2 * depth, d).transpose(2, 0, 1, 3)
    mem_k_p, mem_v_p = kv[:depth], kv[depth:]

    y_p, re_p, im_p, conv_p = _trunk(x_prompt, None, None, mem_k_p, mem_v_p, w, sample=False)
    cache_k = cache_mem_k.reshape(depth, dec_b, MEM_TOKENS, d)
    cache_v = cache_mem_v.reshape(depth, dec_b, MEM_TOKENS, d)
    y_s, re_s, im_s, conv_s = _trunk(x_sample, (state_ssm_re[0], state_ssm_im[0]), state_conv[0],
                                     cache_k, cache_v, w, sample=True)
    head_shape = (depth, bsz, MEM_TOKENS, MEM_HEADS, MEM_HEAD_DIM)
    return (y_p, y_s, re_p, im_p, conv_p, mem_k_p.reshape(head_shape), mem_v_p.reshape(head_shape),
            re_s, im_s, conv_s)
```

```python
import functools
import math

import jax
import jax.numpy as jnp
from jax import lax
from jax.experimental import pallas as pl
from jax.experimental.pallas import tpu as pltpu

F32 = jnp.float32
BF16 = jnp.bfloat16

D_MODEL = 1024
SSM_GROUP = 16
SSM_GROUPS = D_MODEL // SSM_GROUP
SSM_STATE = 64
SSM_BLOCK_GROUPS = 8
SSM_BLOCKS = SSM_GROUPS // SSM_BLOCK_GROUPS
MEM_TOKENS = 256
MEM_HEADS = 4
MEM_HEAD_DIM = D_MODEL // MEM_HEADS
PEER_HEADS = 8
PEER_KEYS = 128
PEER_EXPERTS = PEER_KEYS * PEER_KEYS
PEER_TOPK = 16
RMS_EPS = 1e-6

LANES = 128
SUBLANES = 8
ROW_TILE = 512
S5_CHUNK = 8
S5_COL_TILE = 512
EXPERT_TILE = 2048
VMEM_LIMIT = 48 * 1024 * 1024

_NT = (((1,), (1,)), ((), ()))


def _params(*sem):
    return pltpu.CompilerParams(dimension_semantics=sem, vmem_limit_bytes=VMEM_LIMIT)


def _rms(x, g):
    r = lax.rsqrt(jnp.mean(x * x, axis=-1, keepdims=True) + RMS_EPS)
    return (x * r) * g


def _gelu(x):
    return 0.5 * x * (1.0 + lax.erf(x * (1.0 / math.sqrt(2.0))))


def _sigmoid(x):
    return 1.0 / (1.0 + jnp.exp(-x))


def _norm_cast_kernel(x_ref, g_ref, o_ref):
    o_ref[...] = _rms(x_ref[...], g_ref[...]).astype(BF16)


def _norm_cast(x, g):
    t, d = x.shape
    tm = min(ROW_TILE, t)
    return pl.pallas_call(
        _norm_cast_kernel,
        grid=(t // tm,),
        in_specs=[pl.BlockSpec((tm, d), lambda i: (i, 0)), pl.BlockSpec((1, d), lambda i: (0, 0))],
        out_specs=pl.BlockSpec((tm, d), lambda i: (i, 0)),
        out_shape=jax.ShapeDtypeStruct((t, d), BF16),
        compiler_params=_params("parallel"),
        name="norm_cast",
    )(x, g.reshape(1, d))


def _mm_kernel(*refs, has_norm, has_res):
    x_ref, w_ref = refs[0], refs[1]
    pos = 2
    x = x_ref[...]
    if has_norm:
        x = _rms(x, refs[pos][...])
        pos += 1
    y = jnp.dot(x.astype(BF16), w_ref[...], preferred_element_type=F32)
    if has_res:
        y = y + refs[pos][...]
        pos += 1
    refs[pos][...] = y


def _mm(x, w, g=None, res=None, tm=ROW_TILE):
    t, k = x.shape
    n = w.shape[1]
    tm = min(tm, t)
    args = [x, w]
    specs = [pl.BlockSpec((tm, k), lambda i: (i, 0)), pl.BlockSpec((k, n), lambda i: (0, 0))]
    if g is not None:
        args.append(g.reshape(1, k))
        specs.append(pl.BlockSpec((1, k), lambda i: (0, 0)))
    if res is not None:
        args.append(res)
        specs.append(pl.BlockSpec((tm, n), lambda i: (i, 0)))
    return pl.pallas_call(
        functools.partial(_mm_kernel, has_norm=g is not None, has_res=res is not None),
        grid=(t // tm,),
        in_specs=specs,
        out_specs=pl.BlockSpec((tm, n), lambda i: (i, 0)),
        out_shape=jax.ShapeDtypeStruct((t, n), F32),
        compiler_params=_params("parallel"),
        name="mm_rows",
    )(*args)


def _s5_fold(a_re, a_im, log_dt, b_re, b_im, c_re, c_im, chunk):
    hi = lax.Precision.HIGHEST
    dt = jnp.exp(log_dt)[:, None]
    mag = jnp.exp(a_re * dt)
    ang = a_im * dt
    lb_re = mag * jnp.cos(ang)
    lb_im = mag * jnp.sin(ang)
    den = a_re * a_re + a_im * a_im
    f_re = ((lb_re - 1.0) * a_re + lb_im * a_im) / den
    f_im = (lb_im * a_re - (lb_re - 1.0) * a_im) / den
    bb_re = f_re[..., None] * b_re - f_im[..., None] * b_im
    bb_im = f_re[..., None] * b_im + f_im[..., None] * b_re
    pw_re, pw_im = [jnp.ones_like(lb_re)], [jnp.zeros_like(lb_im)]
    for _ in range(chunk):
        pr, pi = pw_re[-1], pw_im[-1]
        pw_re.append(pr * lb_re - pi * lb_im)
        pw_im.append(pr * lb_im + pi * lb_re)
    pw_re, pw_im = jnp.stack(pw_re), jnp.stack(pw_im)
    cl_re = c_re[None] * pw_re[:, :, None, :] - c_im[None] * pw_im[:, :, None, :]
    cl_im = c_re[None] * pw_im[:, :, None, :] + c_im[None] * pw_re[:, :, None, :]
    kern = (jnp.einsum('kgdp,gpc->kgdc', cl_re[:chunk], bb_re, precision=hi)
            - jnp.einsum('kgdp,gpc->kgdc', cl_im[:chunk], bb_im, precision=hi))
    step = jnp.arange(chunk)
    rp_re, rp_im = pw_re[chunk - 1 - step], pw_im[chunk - 1 - step]
    p_re = rp_re[..., None] * bb_re[None] - rp_im[..., None] * bb_im[None]
    p_im = rp_re[..., None] * bb_im[None] + rp_im[..., None] * bb_re[None]
    p_re = p_re.transpose(1, 0, 3, 2)
    p_im = p_im.transpose(1, 0, 3, 2)
    n_re = cl_re[1:].transpose(1, 3, 0, 2)
    n_im = (-cl_im[1:]).transpose(1, 3, 0, 2)

    nb, q = SSM_BLOCKS, SSM_BLOCK_GROUPS
    eye = jnp.eye(q, dtype=BF16)
    w = chunk * q * SSM_GROUP
    sl = q * SSM_STATE
    kq = kern.transpose(1, 3, 0, 2).astype(BF16).reshape(nb, q, SSM_GROUP, chunk, 1, SSM_GROUP)
    kq = (kq * eye[None, :, None, None, :, None]).reshape(nb, q, SSM_GROUP, chunk, q * SSM_GROUP)
    kq = jnp.pad(kq, ((0, 0), (0, 0), (0, 0), (chunk - 1, 0), (0, 0)))
    m = jnp.stack([kq[:, :, :, chunk - 1 - t:2 * chunk - 1 - t] for t in range(chunk)], axis=1)
    m = m.reshape(nb, w, w)

    def fold_p(z):
        z = z.astype(BF16).reshape(nb, q, chunk, SSM_GROUP, 1, SSM_STATE).transpose(0, 2, 1, 3, 4, 5)
        return (z * eye[None, None, :, None, :, None]).reshape(nb, w, sl)

    def fold_n(z):
        z = z.astype(BF16).reshape(nb, q, SSM_STATE, chunk, 1, SSM_GROUP)
        return (z * eye[None, :, None, None, :, None]).reshape(nb, sl, w)

    lam_re = pw_re[chunk].reshape(nb, 1, sl)
    lam_im = pw_im[chunk].reshape(nb, 1, sl)
    return m, fold_p(p_re), fold_p(p_im), fold_n(n_re), fold_n(n_im), lam_re, lam_im


def _s5_core_kernel(x_ref, m_ref, pre_ref, pim_ref, nre_ref, nim_ref, lre_ref, lim_ref,
                    s0re_ref, s0im_ref, y_ref, sre_ref, sim_ref, stre, stim, *, nchunks, rows):
    x = x_ref[0]

    @pl.when(pl.program_id(1) == 0)
    def _():
        stre[...] = jnp.dot(x, pre_ref[0], preferred_element_type=F32)
        stim[...] = jnp.dot(x, pim_ref[0], preferred_element_type=F32)
        width = stre.shape[1]
        lr = jnp.broadcast_to(lre_ref[0], (rows, width))
        li = jnp.broadcast_to(lim_ref[0], (rows, width))

        def body(k, carry):
            sr, si = carry
            off = pl.multiple_of(k * rows, rows)
            qr = stre[pl.ds(off, rows), :]
            qi = stim[pl.ds(off, rows), :]
            stre[pl.ds(off, rows), :] = sr
            stim[pl.ds(off, rows), :] = si
            return lr * sr - li * si + qr, lr * si + li * sr + qi

        sr, si = lax.fori_loop(0, nchunks, body, (s0re_ref[0], s0im_ref[0]))
        sre_ref[0] = sr
        sim_ref[0] = si

    y = jnp.dot(x, m_ref[0], preferred_element_type=F32)
    y = y + jnp.dot(stre[...].astype(BF16), nre_ref[0], preferred_element_type=F32)
    y = y + jnp.dot(stim[...].astype(BF16), nim_ref[0], preferred_element_type=F32)
    y_ref[0] = y


def _s5_core(xb, folded, s0_re, s0_im, nchunks, rows):
    m, p_re, p_im, n_re, n_im, lam_re, lam_im = folded
    nblk, nr, w = xb.shape
    sl = SSM_BLOCK_GROUPS * SSM_STATE
    wc = min(w, S5_COL_TILE)
    fixed = lambda a, b: pl.BlockSpec((1, a, b), lambda i, j: (i, 0, 0))
    cols = lambda a: pl.BlockSpec((1, a, wc), lambda i, j: (i, 0, j))
    return pl.pallas_call(
        functools.partial(_s5_core_kernel, nchunks=nchunks, rows=rows),
        grid=(nblk, w // wc),
        in_specs=[fixed(nr, w), cols(w), fixed(w, sl), fixed(w, sl), cols(sl), cols(sl),
                  fixed(1, sl), fixed(1, sl), fixed(rows, sl), fixed(rows, sl)],
        out_specs=[cols(nr), fixed(rows, sl), fixed(rows, sl)],
        out_shape=[jax.ShapeDtypeStruct((nblk, nr, w), F32),
                   jax.ShapeDtypeStruct((nblk, rows, sl), F32),
                   jax.ShapeDtypeStruct((nblk, rows, sl), F32)],
        scratch_shapes=[pltpu.VMEM((nr, sl), F32)] * 2,
        compiler_params=_params("parallel", "arbitrary"),
        name="s5_core",
    )(xb, m, p_re, p_im, n_re, n_im, lam_re, lam_im, s0_re, s0_im)


def _s5_out_kernel(x_ref, y_ref, g_ref, d_ref, w_ref, o_ref):
    x = x_ref[...]
    h = _rms(x, g_ref[...])
    z = _gelu(y_ref[...] + d_ref[...] * h)
    gg = jnp.dot(z.astype(BF16), w_ref[...], preferred_element_type=F32)
    o_ref[...] = x + gg[:, :D_MODEL] * _sigmoid(gg[:, D_MODEL:])


def _s5_out(x, y, g, d, w_glu):
    t = x.shape[0]
    tm = min(ROW_TILE, t)
    row = pl.BlockSpec((tm, D_MODEL), lambda i: (i, 0))
    vec = pl.BlockSpec((1, D_MODEL), lambda i: (0, 0))
    return pl.pallas_call(
        _s5_out_kernel,
        grid=(t // tm,),
        in_specs=[row, row, vec, vec, pl.BlockSpec((D_MODEL, 2 * D_MODEL), lambda i: (0, 0))],
        out_specs=row,
        out_shape=jax.ShapeDtypeStruct((t, D_MODEL), F32),
        compiler_params=_params("parallel"),
        name="s5_out",
    )(x, y, g.reshape(1, -1), d.reshape(1, -1), w_glu)


def _s5_mixer(x, s0_re, s0_im, g, folded, d_skip, w_glu, chunk):
    b, s, d = x.shape
    nk = s // chunk
    x2 = x.reshape(b * s, d)
    h = _norm_cast(x2, g)
    xb = (h.reshape(b, nk, chunk, SSM_BLOCKS, LANES).transpose(3, 1, 0, 2, 4)
          .reshape(SSM_BLOCKS, nk * b, chunk * LANES))
    sl = SSM_BLOCK_GROUPS * SSM_STATE
    to_block = lambda z: z.reshape(b, SSM_BLOCKS, sl).transpose(1, 0, 2)
    y, sr, si = _s5_core(xb, folded, to_block(s0_re), to_block(s0_im), nk, b)
    y = y.reshape(SSM_BLOCKS, nk, b, chunk, LANES).transpose(2, 1, 3, 0, 4).reshape(b * s, d)
    from_block = lambda z: z.transpose(1, 0, 2).reshape(b, SSM_GROUPS, SSM_STATE)
    out = _s5_out(x2, y, g, d_skip, w_glu)
    return out.reshape(b, s, d), from_block(sr), from_block(si)


CONV_HEAD = 8


def _conv_prompt_kernel(x_ref, g_ref, win_ref, wc_ref, wout_ref, o_ref, tail_ref, vbuf, *, tm):
    @pl.when(pl.program_id(1) == 0)
    def _():
        vbuf[0:CONV_HEAD, :] = jnp.zeros((CONV_HEAD, D_MODEL), F32)

    x = x_ref[0]
    h = _rms(x, g_ref[...])
    p = jnp.dot(h.astype(BF16), win_ref[...], preferred_element_type=F32)
    bg = p[:, :D_MODEL]
    v = p[:, D_MODEL:2 * D_MODEL] * p[:, 2 * D_MODEL:]
    vbuf[CONV_HEAD:CONV_HEAD + tm, :] = v
    v2 = vbuf[CONV_HEAD - 2:CONV_HEAD - 2 + tm, :]
    v1 = vbuf[CONV_HEAD - 1:CONV_HEAD - 1 + tm, :]
    conv = wc_ref[0:1, :] * v2 + wc_ref[1:2, :] * v1 + wc_ref[2:3, :] * v
    out = jnp.dot((bg * conv).astype(BF16), wout_ref[...], preferred_element_type=F32)
    o_ref[0] = x + out
    tail = vbuf[tm:tm + CONV_HEAD, :]
    vbuf[0:CONV_HEAD, :] = tail
    tail_ref[0] = tail


def _conv_prompt(x, g, w_in, w_conv, w_out):
    b, s, d = x.shape
    tm = min(ROW_TILE, s)
    return pl.pallas_call(
        functools.partial(_conv_prompt_kernel, tm=tm),
        grid=(b, s // tm),
        in_specs=[pl.BlockSpec((1, tm, d), lambda i, j: (i, j, 0)),
                  pl.BlockSpec((1, d), lambda i, j: (0, 0)),
                  pl.BlockSpec((d, 3 * d), lambda i, j: (0, 0)),
                  pl.BlockSpec((3, d), lambda i, j: (0, 0)),
                  pl.BlockSpec((d, d), lambda i, j: (0, 0))],
        out_specs=[pl.BlockSpec((1, tm, d), lambda i, j: (i, j, 0)),
                   pl.BlockSpec((1, CONV_HEAD, d), lambda i, j: (i, 0, 0))],
        out_shape=[jax.ShapeDtypeStruct((b, s, d), F32),
                   jax.ShapeDtypeStruct((b, CONV_HEAD, d), F32)],
        scratch_shapes=[pltpu.VMEM((CONV_HEAD + tm, d), F32)],
        compiler_params=_params("parallel", "arbitrary"),
        name="conv_prompt",
    )(x, g.reshape(1, d), w_in, w_conv, w_out)


def _conv_sample_kernel(x_ref, buf_ref, g_ref, win_ref, wc_ref, wout_ref, o_ref, nbuf_ref, *, steps, nb):
    x = x_ref[...]
    h = _rms(x, g_ref[...])
    p = jnp.dot(h.astype(BF16), win_ref[...], preferred_element_type=F32)
    bg = p[:, :D_MODEL]
    v = p[:, D_MODEL:2 * D_MODEL] * p[:, 2 * D_MODEL:]
    vp = [buf_ref[0:nb, :], buf_ref[nb:2 * nb, :]] + [v[t * nb:(t + 1) * nb, :] for t in range(steps)]
    conv = jnp.concatenate(
        [wc_ref[0:1, :] * vp[t] + wc_ref[1:2, :] * vp[t + 1] + wc_ref[2:3, :] * vp[t + 2]
         for t in range(steps)], axis=0)
    out = jnp.dot((bg * conv).astype(BF16), wout_ref[...], preferred_element_type=F32)
    o_ref[...] = x + out
    nbuf_ref[0:nb, :] = vp[-2]
    nbuf_ref[nb:2 * nb, :] = vp[-1]


def _conv_sample(x_tm, buf_tm, g, w_in, w_conv, w_out, steps, nb):
    d = D_MODEL
    full = lambda r, c: pl.BlockSpec((r, c), lambda i: (0, 0))
    return pl.pallas_call(
        functools.partial(_conv_sample_kernel, steps=steps, nb=nb),
        grid=(1,),
        in_specs=[full(steps * nb, d), full(2 * nb, d), full(1, d), full(d, 3 * d), full(3, d), full(d, d)],
        out_specs=[full(steps * nb, d), full(2 * nb, d)],
        out_shape=[jax.ShapeDtypeStruct((steps * nb, d), F32), jax.ShapeDtypeStruct((2 * nb, d), F32)],
        compiler_params=_params("arbitrary"),
        name="conv_sample",
    )(x_tm, buf_tm, g.reshape(1, d), w_in, w_conv, w_out)


def _softmax_rows(s):
    e = jnp.exp(s - jnp.max(s, axis=-1, keepdims=True))
    return e / jnp.sum(e, axis=-1, keepdims=True)


def _attn_heads(q, k, v):
    k = k.astype(BF16)
    v = v.astype(BF16)
    outs = []
    for hd in range(MEM_HEADS):
        sl = slice(hd * MEM_HEAD_DIM, (hd + 1) * MEM_HEAD_DIM)
        s = lax.dot_general(q[:, sl].astype(BF16), k[:, sl], _NT, preferred_element_type=F32)
        p = _softmax_rows(s * (MEM_HEAD_DIM ** -0.5))
        outs.append(jnp.dot(p.astype(BF16), v[:, sl], preferred_element_type=F32))
    return jnp.concatenate(outs, axis=-1)


def _attn_prompt_kernel(x_ref, g_ref, k_ref, v_ref, wq_ref, wo_ref, o_ref):
    x = x_ref[0]
    h = _rms(x, g_ref[...])
    q = jnp.dot(h.astype(BF16), wq_ref[...], preferred_element_type=F32)
    o = _attn_heads(q, k_ref[0], v_ref[0])
    o_ref[0] = x + jnp.dot(o.astype(BF16), wo_ref[...], preferred_element_type=F32)


def _attn_prompt(x, g, mem_k, mem_v, layer, w_q, w_o):
    b, s, d = x.shape
    tm = min(ROW_TILE, s)
    mem = pl.BlockSpec((None, 1, MEM_TOKENS, d), lambda i, j: (layer, i, 0, 0))
    wgt = pl.BlockSpec((d, d), lambda i, j: (0, 0))
    row = pl.BlockSpec((1, tm, d), lambda i, j: (i, j, 0))
    return pl.pallas_call(
        _attn_prompt_kernel,
        grid=(b, s // tm),
        in_specs=[row, pl.BlockSpec((1, d), lambda i, j: (0, 0)), mem, mem, wgt, wgt],
        out_specs=row,
        out_shape=jax.ShapeDtypeStruct((b, s, d), F32),
        compiler_params=_params("parallel", "parallel"),
        name="attn_prompt",
    )(x, g.reshape(1, d), mem_k, mem_v, w_q, w_o)


SAMPLE_Q_ROWS = 8
SAMPLE_SEQ_BLOCK = 4


def _attn_sample_kernel(q_ref, k_ref, v_ref, o_ref):
    def body(i, carry):
        o_ref[i] = _attn_heads(q_ref[i], k_ref[i], v_ref[i])
        return carry

    lax.fori_loop(0, SAMPLE_SEQ_BLOCK, body, 0)


def _attn_sample(x, g, cache_k, cache_v, layer, w_q, w_o):
    b, s, d = x.shape
    x2 = x.reshape(b * s, d)
    q = _mm(x2, w_q, g=g).reshape(b, s, d)
    q = jnp.pad(q, ((0, 0), (0, SAMPLE_Q_ROWS - s), (0, 0)))
    qblk = pl.BlockSpec((SAMPLE_SEQ_BLOCK, SAMPLE_Q_ROWS, d), lambda i: (i, 0, 0))
    mblk = pl.BlockSpec((None, SAMPLE_SEQ_BLOCK, MEM_TOKENS, d), lambda i: (layer, i, 0, 0))
    o = pl.pallas_call(
        _attn_sample_kernel,
        grid=(b // SAMPLE_SEQ_BLOCK,),
        in_specs=[qblk, mblk, mblk],
        out_specs=qblk,
        out_shape=jax.ShapeDtypeStruct((b, SAMPLE_Q_ROWS, d), F32),
        compiler_params=_params("parallel"),
        name="attn_sample",
    )(q, cache_k, cache_v)
    o = o[:, :s].reshape(b * s, d)
    return _mm(o, w_o, res=x2).reshape(b, s, d)


def _top16(s):
    iota = lax.broadcasted_iota(jnp.int32, s.shape, 0)
    pos = jnp.full(s.shape, PEER_TOPK, jnp.int32)
    vals = []
    for i in range(PEER_TOPK):
        m = jnp.max(s, axis=0, keepdims=True)
        idx = jnp.min(jnp.where(s == m, iota, s.shape[0]), axis=0, keepdims=True)
        hit = iota == idx
        pos = jnp.where(hit, i, pos)
        s = jnp.where(hit, -jnp.inf, s)
        vals.append(m)
    return vals, pos


def _pair_select(vals1, vals2):
    lanes = vals1[0].shape[1]
    iota16 = lax.broadcasted_iota(jnp.int32, (PEER_TOPK, lanes), 0)
    v2 = jnp.zeros((PEER_TOPK, lanes), F32)
    for b in range(PEER_TOPK):
        v2 = jnp.where(iota16 == b, vals2[b], v2)
    iota8 = iota16[:8]
    cand = [vals1[0] + v2]
    flat = [iota16]
    for a in range(1, PEER_TOPK):
        cand.append(jnp.where(iota8 < PEER_TOPK // (a + 1), vals1[a] + v2[:8], -jnp.inf))
        flat.append(iota8 + a * PEER_TOPK)
    nsel = jnp.zeros((PEER_TOPK, lanes), jnp.int32)
    z = jnp.zeros((1, lanes), F32)
    top = vals1[0] + vals2[0]
    nflat = PEER_TOPK * PEER_TOPK
    for _ in range(PEER_TOPK):
        mx = jnp.maximum(cand[0][:8], cand[0][8:])
        for a in range(1, PEER_TOPK):
            mx = jnp.maximum(mx, cand[a])
        m = jnp.max(mx, axis=0, keepdims=True)
        w0 = jnp.where(cand[0] == m, flat[0], nflat)
        mi = jnp.minimum(w0[:8], w0[8:])
        for a in range(1, PEER_TOPK):
            mi = jnp.minimum(mi, jnp.where(cand[a] == m, flat[a], nflat))
        idx = jnp.min(mi, axis=0, keepdims=True)
        cand = [jnp.where(flat[a] == idx, -jnp.inf, cand[a]) for a in range(PEER_TOPK)]
        nsel = nsel + (iota16 == (idx >> 4)).astype(jnp.int32)
        z = z + jnp.exp(m - top)
    return nsel, z


def _sort_network(n):
    pairs = []

    def merge(lo, cnt, r):
        m = r * 2
        if m < cnt:
            merge(lo, cnt, m)
            merge(lo + r, cnt, m)
            for i in range(lo + r, lo + cnt - r, m):
                pairs.append((i, i + r))
        else:
            pairs.append((lo, lo + r))

    def sort(lo, cnt):
        if cnt > 1:
            m = cnt // 2
            sort(lo, m)
            sort(lo + m, m)
            merge(lo, cnt, 1)

    sort(0, n)
    return tuple(pairs)


_SORT16 = _sort_network(PEER_TOPK)


def _exchange(v, i, j):
    v[i], v[j] = jnp.maximum(v[i], v[j]), jnp.minimum(v[i], v[j])


def _allreduce_rows(x, op):
    for shift in (4, 2, 1):
        x = op(x, pltpu.roll(x, shift, 0))
    return x


def _sorted_top16(s):
    v = [s[SUBLANES * j:SUBLANES * (j + 1), :] for j in range(PEER_TOPK)]
    for i, j in _SORT16:
        _exchange(v, i, j)
    for shift in (4, 2, 1):
        r = [pltpu.roll(x, shift, 0) for x in v]
        v = [jnp.maximum(v[i], r[PEER_TOPK - 1 - i]) for i in range(PEER_TOPK)]
        for stride in (8, 4, 2, 1):
            for i in range(PEER_TOPK):
                if i & stride == 0:
                    _exchange(v, i, i + stride)
    return v


def _rank_bits(x, v):
    b3 = v[7] > x
    b2 = jnp.where(b3, v[11], v[3]) > x
    b1 = jnp.where(b3, jnp.where(b2, v[13], v[9]), jnp.where(b2, v[5], v[1])) > x
    t = jnp.where(b3,
                  jnp.where(b2, jnp.where(b1, v[14], v[12]), jnp.where(b1, v[10], v[8])),
                  jnp.where(b2, jnp.where(b1, v[6], v[4]), jnp.where(b1, v[2], v[0])))
    return (b3, b2, b1, t > x), v[PEER_TOPK - 1] > x


def _select16(bits, vals):
    b3, b2, b1, b0 = bits
    lvl = [jnp.where(b0, vals[2 * i + 1], vals[2 * i]) for i in range(8)]
    lvl = [jnp.where(b1, lvl[2 * i + 1], lvl[2 * i]) for i in range(4)]
    lvl = [jnp.where(b2, lvl[2 * i + 1], lvl[2 * i]) for i in range(2)]
    return jnp.where(b3, lvl[1], lvl[0])


def _pair_counts(v1, v2):
    sub = lax.broadcasted_iota(jnp.int32, v1[0].shape, 0)
    ninf = -jnp.inf

    def column(v, off):
        col = v[off]
        for b in range(1, SUBLANES):
            col = jnp.where(sub == b, v[off + b], col)
        return col

    v2c0, v2c1 = column(v2, 0), column(v2, SUBLANES)
    v1c0, v1c1 = column(v1, 0), column(v1, SUBLANES)
    rows = [[v1[0] + v2c0, v1[0] + v2c1], [v1[1] + v2c0],
            [jnp.where(sub < 5, v1[2] + v2c0, ninf)], [jnp.where(sub < 4, v1[3] + v2c0, ninf)]]
    cols = [jnp.where(sub >= 4, v1c0 + v2[0], ninf), v1c1 + v2[0],
            jnp.where(sub >= 4, v1c0 + v2[1], ninf), jnp.where(sub == 4, v1c0 + v2[2], ninf)]
    cur = [c for r in rows for c in r] + cols
    top = v1[0] + v2[0]
    z = jnp.zeros_like(top)
    for _ in range(PEER_TOPK):
        m = cur[0]
        for c in cur[1:]:
            m = jnp.maximum(m, c)
        m = _allreduce_rows(m, jnp.maximum)
        z = z + jnp.exp(m - top)
        cur = [jnp.where(c == m, ninf, c) for c in cur]
    tau = m
    count = lambda c: jnp.where(c >= tau, 1.0, 0.0)
    nsel = []
    for r in rows:
        f = count(r[0])
        for c in r[1:]:
            f = f + count(c)
        nsel.append(_allreduce_rows(f, jnp.add))
    for a in range(4, PEER_TOPK):
        f = count(v1[a] + v2[0])
        for b in range(1, PEER_TOPK // (a + 1)):
            f = f + count(v1[a] + v2[b])
        nsel.append(f)
    total = nsel[0]
    for f in nsel[1:]:
        total = total + f
    return nsel, z, total != float(PEER_TOPK)


def _key_ranks(s, v, lookup=None):
    outs = []
    ranked = None
    for j in range(PEER_TOPK):
        x = s[SUBLANES * j:SUBLANES * (j + 1), :]
        bits, low = _rank_bits(x, v)
        if lookup is None:
            b3, b2, b1, b0 = bits
            val = (jnp.where(b3, 8.0, 0.0) + jnp.where(b2, 4.0, 0.0)
                   + jnp.where(b1, 2.0, 0.0) + jnp.where(b0, 1.0, 0.0) + jnp.where(low, 1.0, 0.0))
        else:
            val = jnp.where(low, 0.0, _select16(bits, lookup))
        outs.append(val)
        r = jnp.where(low, 0.0, 1.0)
        ranked = r if ranked is None else ranked + r
    tie = _allreduce_rows(ranked, jnp.add) != float(PEER_TOPK)
    for a in range(PEER_TOPK - 1):
        tie = tie | (v[a] == v[a + 1])
    return jnp.concatenate(outs, axis=0), tie


def _peer_select_kernel(x_ref, g_ref, wq_ref, k1_ref, k2_ref,
                        pos2_ref, m2_ref, nsel_ref, m1_ref, q_scr, s1_scr, s2_scr, *, tm):
    h = _rms(x_ref[...], g_ref[...]).astype(BF16)
    q = jnp.dot(h, wq_ref[...], preferred_element_type=F32)
    for j in range(2 * PEER_HEADS):
        q_scr[j] = q[:, j * PEER_KEYS:(j + 1) * PEER_KEYS].astype(BF16)
    k1 = k1_ref[...]
    k2 = k2_ref[...]

    def head(hh, carry):
        s1_scr[...] = lax.dot_general(k1, q_scr[2 * hh], _NT, preferred_element_type=F32)
        s2_scr[...] = lax.dot_general(k2, q_scr[2 * hh + 1], _NT, preferred_element_type=F32)
        tie = None
        for c in range(tm // LANES):
            sl = slice(c * LANES, (c + 1) * LANES)
            s1, s2 = s1_scr[:, sl], s2_scr[:, sl]
            v1 = _sorted_top16(s1)
            v2 = _sorted_top16(s2)
            nsel, z, t0 = _pair_counts(v1, v2)
            pos2, t2 = _key_ranks(s2, v2)
            nfull, t1 = _key_ranks(s1, v1, lookup=nsel)
            t = t0 | t1 | t2
            tie = t if tie is None else tie | t
            pos2_ref[hh, :, sl] = pos2.astype(BF16)
            m2_ref[hh, :, sl] = jnp.exp(s2 - v2[0][0:1, :]).astype(BF16)
            nsel_ref[hh, :, sl] = nfull
            m1_ref[hh, :, sl] = jnp.exp(s1 - v1[0][0:1, :]) * (1.0 / z[0:1, :])

        @pl.when(jnp.max(jnp.where(tie, 1.0, 0.0)) > 0.0)
        def _():
            for c in range(tm // LANES):
                sl = slice(c * LANES, (c + 1) * LANES)
                s1, s2 = s1_scr[:, sl], s2_scr[:, sl]
                vals1, pos1 = _top16(s1)
                vals2, pos2 = _top16(s2)
                nsel, z = _pair_select(vals1, vals2)
                nfull = jnp.zeros(s1.shape, jnp.int32)
                for a in range(PEER_TOPK):
                    nfull = jnp.where(pos1 == a, nsel[a:a + 1, :], nfull)
                pos2_ref[hh, :, sl] = pos2.astype(F32).astype(BF16)
                m2_ref[hh, :, sl] = jnp.exp(s2 - vals2[0]).astype(BF16)
                nsel_ref[hh, :, sl] = nfull.astype(F32)
                m1_ref[hh, :, sl] = jnp.exp(s1 - vals1[0]) * (1.0 / z)

        return carry

    lax.fori_loop(0, PEER_HEADS, head, 0)


PEER_SELECT_TILE = 256


def _peer_select(x, g, w_query, key1, key2):
    t, d = x.shape
    tm = PEER_SELECT_TILE
    nq = w_query.shape[1]
    head = pl.BlockSpec((PEER_HEADS, PEER_KEYS, tm), lambda i: (0, 0, i))
    keyspec = pl.BlockSpec((PEER_KEYS, PEER_KEYS), lambda i: (0, 0))
    shp = lambda dt: jax.ShapeDtypeStruct((PEER_HEADS, PEER_KEYS, t), dt)
    return pl.pallas_call(
        functools.partial(_peer_select_kernel, tm=tm),
        grid=(t // tm,),
        in_specs=[pl.BlockSpec((tm, d), lambda i: (i, 0)), pl.BlockSpec((1, d), lambda i: (0, 0)),
                  pl.BlockSpec((d, nq), lambda i: (0, 0)), keyspec, keyspec],
        out_specs=[head] * 4,
        out_shape=[shp(BF16), shp(BF16), shp(F32), shp(F32)],
        scratch_shapes=[pltpu.VMEM((2 * PEER_HEADS, tm, PEER_KEYS), BF16),
                        pltpu.VMEM((PEER_KEYS, tm), F32), pltpu.VMEM((PEER_KEYS, tm), F32)],
        compiler_params=_params("parallel"),
        name="peer_select",
    )(x, g.reshape(1, d), w_query, key1, key2)


def _zero_after(t):
    bits = pltpu.bitcast(t[0:2 * SUBLANES, :], jnp.uint32)
    return jnp.max(((bits >> 16) >> 16).astype(jnp.int32)).astype(F32).astype(BF16)


PEER_GATE_ROWS = 1


def _peer_dense_kernel(x_ref, g_ref, gf_ref, u_ref, vt_ref, pos2_ref, m2_ref, nsel_ref, m1_ref,
                       o_ref, h_scr, acc_scr, p2_scr, m2_scr, *, tm, te, final_norm):
    j = pl.program_id(1)

    @pl.when(j == 0)
    def _():
        h_scr[...] = _rms(x_ref[...], g_ref[...]).astype(BF16)
        acc_scr[...] = jnp.zeros(acc_scr.shape, F32)
        p2_scr[...] = pos2_ref[...]
        m2_scr[...] = m2_ref[...]

    n1 = te // PEER_KEYS
    first = pl.multiple_of(j * n1, n1)
    half = n1 // 2

    def activations(r0, r1, gate=None):
        hop = h_scr[...]
        if gate is not None:
            hop = hop + gate
        return lax.dot_general(u_ref[r0 * PEER_KEYS:r1 * PEER_KEYS, :], hop, _NT, preferred_element_type=F32)

    def weighted(a, base, r0, r1):
        blocks = []
        for r in range(r0, r1):
            tiles = []
            for ct in range(tm // LANES):
                sl = slice(ct * LANES, (ct + 1) * LANES)
                w = None
                for hh in range(PEER_HEADS):
                    nrow = nsel_ref[hh, pl.ds(first, n1), sl][r:r + 1, :].astype(BF16)
                    mrow = m1_ref[hh, pl.ds(first, n1), sl][r:r + 1, :].astype(BF16)
                    t = jnp.where(p2_scr[hh, :, sl] < nrow, m2_scr[hh, :, sl] * mrow, 0)
                    w = t if w is None else w + t
                at = a[(r - base) * PEER_KEYS:(r - base + 1) * PEER_KEYS, sl]
                tiles.append(_gelu(at).astype(BF16) * w)
            blocks.append(jnp.concatenate(tiles, axis=1))
        return blocks

    a_top = activations(0, half)
    z = weighted(a_top, 0, 0, PEER_GATE_ROWS)
    a_bot = activations(half, n1, gate=_zero_after(z[-1]))
    z = z + weighted(a_top, 0, PEER_GATE_ROWS, half) + weighted(a_bot, half, half, n1)
    acc_scr[...] += jnp.dot(vt_ref[...], jnp.concatenate(z, axis=0), preferred_element_type=F32)

    @pl.when(j == pl.num_programs(1) - 1)
    def _():
        out = x_ref[...] + acc_scr[...].T
        if final_norm:
            out = _rms(out, gf_ref[...])
        o_ref[...] = out


def _peer_dense(x, g, g_final, u, vt, sel, final_norm):
    t, d = x.shape
    tm = min(ROW_TILE, t)
    te = vt.shape[2]
    assert (te // PEER_KEYS) % SUBLANES == 0
    row = pl.BlockSpec((tm, d), lambda i, j: (i, 0))
    vec = pl.BlockSpec((1, d), lambda i, j: (0, 0))
    head = pl.BlockSpec((PEER_HEADS, PEER_KEYS, tm), lambda i, j: (0, 0, i))
    return pl.pallas_call(
        functools.partial(_peer_dense_kernel, tm=tm, te=te, final_norm=final_norm),
        grid=(t // tm, PEER_EXPERTS // te),
        in_specs=[row, vec, vec, pl.BlockSpec((te, d), lambda i, j: (j, 0)),
                  pl.BlockSpec((None, d, te), lambda i, j: (j, 0, 0)), head, head, head, head],
        out_specs=row,
        out_shape=jax.ShapeDtypeStruct((t, d), F32),
        scratch_shapes=[pltpu.VMEM((tm, d), BF16), pltpu.VMEM((d, tm), F32),
                        pltpu.VMEM((PEER_HEADS, PEER_KEYS, tm), BF16),
                        pltpu.VMEM((PEER_HEADS, PEER_KEYS, tm), BF16)],
        compiler_params=_params("parallel", "arbitrary"),
        name="peer_dense",
    )(x, g.reshape(1, d), g_final.reshape(1, d), u, vt, *sel)


def _peer(x, g, g_final, w_query, key1, key2, u, vt, final_norm):
    shp = x.shape
    x2 = x.reshape(-1, shp[-1])
    sel = _peer_select(x2, g, w_query, key1, key2)
    return _peer_dense(x2, g, g_final, u, vt, sel, final_norm).reshape(shp)


def _trunk(x, ssm0, conv0, mem_k, mem_v, w, sample):
    b, s, d = x.shape
    chunk = s if sample else S5_CHUNK
    if ssm0 is None:
        ssm0 = (jnp.zeros((b, SSM_GROUPS, SSM_STATE), F32),) * 2
    x, ssm_re, ssm_im = _s5_mixer(x, ssm0[0], ssm0[1], w['norm_mix'][0], w['s5_fold'][chunk],
                                  w['ssm_d'][0], w['ssm_w_glu'][0], chunk)
    attn = _attn_sample if sample else _attn_prompt
    for i in range(2):
        if i == 1:
            if sample:
                x_tm = x.transpose(1, 0, 2).reshape(s * b, d)
                buf_tm = conv0.transpose(1, 0, 2).reshape(2 * b, d)
                x_tm, nbuf = _conv_sample(x_tm, buf_tm, w['norm_mix'][1], w['conv_w_in'][0],
                                          w['conv_w'][0], w['conv_w_out'][0], s, b)
                x = x_tm.reshape(s, b, d).transpose(1, 0, 2)
                conv_out = nbuf.reshape(2, b, d).transpose(1, 0, 2)
            else:
                x, tail = _conv_prompt(x, w['norm_mix'][1], w['conv_w_in'][0], w['conv_w'][0],
                                       w['conv_w_out'][0])
                conv_out = tail[:, CONV_HEAD - 2:]
        x = attn(x, w['norm_mem'][i], mem_k, mem_v, i, w['mem_w_q'][i], w['mem_w_o'][i])
        x = _peer(x, w['norm_ffn'][i], w['norm_final'], w['peer_w_query'][i], w['peer_key1'][i],
                  w['peer_key2'][i], w['peer_u'][i], w['peer_vt'][i], final_norm=(i == 1))
    return x, ssm_re[None], ssm_im[None], conv_out[None]


def kernel(x_prompt, x_sample, mem_prompt, state_ssm_re, state_ssm_im, state_conv, cache_mem_k, cache_mem_v, norm_mix, norm_mem, norm_ffn, norm_final, ssm_a_re, ssm_a_im, ssm_log_dt, ssm_b_re, ssm_b_im, ssm_c_re, ssm_c_im, ssm_d, ssm_w_glu, conv_w_in, conv_w, conv_w_out, mem_w_q, mem_w_k, mem_w_v, mem_w_o, peer_w_query, peer_key1, peer_key2, peer_u, peer_v):
    bsz, seq, d = x_prompt.shape
    dec_b, dec_s, _ = x_sample.shape
    depth = mem_w_q.shape[0]
    fold = lambda chunk: _s5_fold(ssm_a_re[0], ssm_a_im[0], ssm_log_dt[0], ssm_b_re[0], ssm_b_im[0],
                                  ssm_c_re[0], ssm_c_im[0], chunk)
    w = dict(
        norm_mix=norm_mix, norm_mem=norm_mem, norm_ffn=norm_ffn, norm_final=norm_final,
        s5_fold={S5_CHUNK: fold(S5_CHUNK), dec_s: fold(dec_s)},
        ssm_d=ssm_d, ssm_w_glu=ssm_w_glu.astype(BF16),
        conv_w_in=conv_w_in.astype(BF16), conv_w=conv_w, conv_w_out=conv_w_out.astype(BF16),
        mem_w_q=mem_w_q.astype(BF16), mem_w_o=mem_w_o.astype(BF16),
        peer_w_query=peer_w_query.astype(BF16), peer_key1=peer_key1.astype(BF16),
        peer_key2=peer_key2.astype(BF16), peer_u=peer_u.astype(BF16),
        peer_vt=peer_v.astype(BF16).reshape(depth, PEER_EXPERTS // EXPERT_TILE, EXPERT_TILE, d)
        .transpose(0, 1, 3, 2),
    )
    w_kv = jnp.concatenate([mem_w_k[i] for i in range(depth)] + [mem_w_v[i] for i in range(depth)],
                           axis=1).astype(BF16)
    kv = _mm(mem_prompt.reshape(bsz * MEM_TOKENS, d), w_kv, tm=256)
    kv = kv.reshape(bsz, MEM_TOKENS, 2 * depth, d).transpose(2, 0, 1, 3)
    mem_k_p, mem_v_p = kv[:depth], kv[depth:]

    y_p, re_p, im_p, conv_p = _trunk(x_prompt, None, None, mem_k_p, mem_v_p, w, sample=False)
    cache_k = cache_mem_k.reshape(depth, dec_b, MEM_TOKENS, d)
    cache_v = cache_mem_v.reshape(depth, dec_b, MEM_TOKENS, d)
    y_s, re_s, im_s, conv_s = _trunk(x_sample, (state_ssm_re[0], state_ssm_im[0]), state_conv[0],
                                     cache_k, cache_v, w, sample=True)
    head_shape = (depth, bsz, MEM_TOKENS, MEM_HEADS, MEM_HEAD_DIM)
    return (y_p, y_s, re_p, im_p, conv_p, mem_k_p.reshape(head_shape), mem_v_p.reshape(head_shape),
            re_s, im_s, conv_s)
```

```python
import functools
import math

import jax
import jax.numpy as jnp
from jax import lax
from jax.experimental import pallas as pl
from jax.experimental.pallas import tpu as pltpu

F32 = jnp.float32
BF16 = jnp.bfloat16

D_MODEL = 1024
SSM_GROUP = 16
SSM_GROUPS = D_MODEL // SSM_GROUP
SSM_STATE = 64
SSM_BLOCK_GROUPS = 8
SSM_BLOCKS = SSM_GROUPS // SSM_BLOCK_GROUPS
MEM_TOKENS = 256
MEM_HEADS = 4
MEM_HEAD_DIM = D_MODEL // MEM_HEADS
PEER_HEADS = 8
PEER_KEYS = 128
PEER_EXPERTS = PEER_KEYS * PEER_KEYS
PEER_TOPK = 16
RMS_EPS = 1e-6

LANES = 128
SUBLANES = 8
ROW_TILE = 512
S5_CHUNK = 8
S5_COL_TILE = 512
EXPERT_TILE = 2048
VMEM_LIMIT = 48 * 1024 * 1024

_NT = (((1,), (1,)), ((), ()))


def _params(*sem):
    return pltpu.CompilerParams(dimension_semantics=sem, vmem_limit_bytes=VMEM_LIMIT)


def _rms(x, g):
    r = lax.rsqrt(jnp.mean(x * x, axis=-1, keepdims=True) + RMS_EPS)
    return (x * r) * g


def _gelu(x):
    return 0.5 * x * (1.0 + lax.erf(x * (1.0 / math.sqrt(2.0))))


def _sigmoid(x):
    return 1.0 / (1.0 + jnp.exp(-x))


def _norm_cast_kernel(x_ref, g_ref, o_ref):
    o_ref[...] = _rms(x_ref[...], g_ref[...]).astype(BF16)


def _norm_cast(x, g):
    t, d = x.shape
    tm = min(ROW_TILE, t)
    return pl.pallas_call(
        _norm_cast_kernel,
        grid=(t // tm,),
        in_specs=[pl.BlockSpec((tm, d), lambda i: (i, 0)), pl.BlockSpec((1, d), lambda i: (0, 0))],
        out_specs=pl.BlockSpec((tm, d), lambda i: (i, 0)),
        out_shape=jax.ShapeDtypeStruct((t, d), BF16),
        compiler_params=_params("parallel"),
        name="norm_cast",
    )(x, g.reshape(1, d))


def _mm_kernel(*refs, has_norm, has_res):
    x_ref, w_ref = refs[0], refs[1]
    pos = 2
    x = x_ref[...]
    if has_norm:
        x = _rms(x, refs[pos][...])
        pos += 1
    y = jnp.dot(x.astype(BF16), w_ref[...], preferred_element_type=F32)
    if has_res:
        y = y + refs[pos][...]
        pos += 1
    refs[pos][...] = y


def _mm(x, w, g=None, res=None, tm=ROW_TILE):
    t, k = x.shape
    n = w.shape[1]
    tm = min(tm, t)
    args = [x, w]
    specs = [pl.BlockSpec((tm, k), lambda i: (i, 0)), pl.BlockSpec((k, n), lambda i: (0, 0))]
    if g is not None:
        args.append(g.reshape(1, k))
        specs.append(pl.BlockSpec((1, k), lambda i: (0, 0)))
    if res is not None:
        args.append(res)
        specs.append(pl.BlockSpec((tm, n), lambda i: (i, 0)))
    return pl.pallas_call(
        functools.partial(_mm_kernel, has_norm=g is not None, has_res=res is not None),
        grid=(t // tm,),
        in_specs=specs,
        out_specs=pl.BlockSpec((tm, n), lambda i: (i, 0)),
        out_shape=jax.ShapeDtypeStruct((t, n), F32),
        compiler_params=_params("parallel"),
        name="mm_rows",
    )(*args)


def _s5_fold(a_re, a_im, log_dt, b_re, b_im, c_re, c_im, chunk):
    hi = lax.Precision.HIGHEST
    dt = jnp.exp(log_dt)[:, None]
    mag = jnp.exp(a_re * dt)
    ang = a_im * dt
    lb_re = mag * jnp.cos(ang)
    lb_im = mag * jnp.sin(ang)
    den = a_re * a_re + a_im * a_im
    f_re = ((lb_re - 1.0) * a_re + lb_im * a_im) / den
    f_im = (lb_im * a_re - (lb_re - 1.0) * a_im) / den
    bb_re = f_re[..., None] * b_re - f_im[..., None] * b_im
    bb_im = f_re[..., None] * b_im + f_im[..., None] * b_re
    pw_re, pw_im = [jnp.ones_like(lb_re)], [jnp.zeros_like(lb_im)]
    for _ in range(chunk):
        pr, pi = pw_re[-1], pw_im[-1]
        pw_re.append(pr * lb_re - pi * lb_im)
        pw_im.append(pr * lb_im + pi * lb_re)
    pw_re, pw_im = jnp.stack(pw_re), jnp.stack(pw_im)
    cl_re = c_re[None] * pw_re[:, :, None, :] - c_im[None] * pw_im[:, :, None, :]
    cl_im = c_re[None] * pw_im[:, :, None, :] + c_im[None] * pw_re[:, :, None, :]
    kern = (jnp.einsum('kgdp,gpc->kgdc', cl_re[:chunk], bb_re, precision=hi)
            - jnp.einsum('kgdp,gpc->kgdc', cl_im[:chunk], bb_im, precision=hi))
    step = jnp.arange(chunk)
    rp_re, rp_im = pw_re[chunk - 1 - step], pw_im[chunk - 1 - step]
    p_re = rp_re[..., None] * bb_re[None] - rp_im[..., None] * bb_im[None]
    p_im = rp_re[..., None] * bb_im[None] + rp_im[..., None] * bb_re[None]
    p_re = p_re.transpose(1, 0, 3, 2)
    p_im = p_im.transpose(1, 0, 3, 2)
    n_re = cl_re[1:].transpose(1, 3, 0, 2)
    n_im = (-cl_im[1:]).transpose(1, 3, 0, 2)

    nb, q = SSM_BLOCKS, SSM_BLOCK_GROUPS
    eye = jnp.eye(q, dtype=BF16)
    w = chunk * q * SSM_GROUP
    sl = q * SSM_STATE
    kq = kern.transpose(1, 3, 0, 2).astype(BF16).reshape(nb, q, SSM_GROUP, chunk, 1, SSM_GROUP)
    kq = (kq * eye[None, :, None, None, :, None]).reshape(nb, q, SSM_GROUP, chunk, q * SSM_GROUP)
    kq = jnp.pad(kq, ((0, 0), (0, 0), (0, 0), (chunk - 1, 0), (0, 0)))
    m = jnp.stack([kq[:, :, :, chunk - 1 - t:2 * chunk - 1 - t] for t in range(chunk)], axis=1)
    m = m.reshape(nb, w, w)

    def fold_p(z):
        z = z.astype(BF16).reshape(nb, q, chunk, SSM_GROUP, 1, SSM_STATE).transpose(0, 2, 1, 3, 4, 5)
        return (z * eye[None, None, :, None, :, None]).reshape(nb, w, sl)

    def fold_n(z):
        z = z.astype(BF16).reshape(nb, q, SSM_STATE, chunk, 1, SSM_GROUP)
        return (z * eye[None, :, None, None, :, None]).reshape(nb, sl, w)

    lam_re = pw_re[chunk].reshape(nb, 1, sl)
    lam_im = pw_im[chunk].reshape(nb, 1, sl)
    return m, fold_p(p_re), fold_p(p_im), fold_n(n_re), fold_n(n_im), lam_re, lam_im


def _s5_core_kernel(x_ref, m_ref, pre_ref, pim_ref, nre_ref, nim_ref, lre_ref, lim_ref,
                    s0re_ref, s0im_ref, y_ref, sre_ref, sim_ref, stre, stim, *, nchunks, rows):
    x = x_ref[0]

    @pl.when(pl.program_id(1) == 0)
    def _():
        stre[...] = jnp.dot(x, pre_ref[0], preferred_element_type=F32)
        stim[...] = jnp.dot(x, pim_ref[0], preferred_element_type=F32)
        width = stre.shape[1]
        lr = jnp.broadcast_to(lre_ref[0], (rows, width))
        li = jnp.broadcast_to(lim_ref[0], (rows, width))

        def body(k, carry):
            sr, si = carry
            off = pl.multiple_of(k * rows, rows)
            qr = stre[pl.ds(off, rows), :]
            qi = stim[pl.ds(off, rows), :]
            stre[pl.ds(off, rows), :] = sr
            stim[pl.ds(off, rows), :] = si
            return lr * sr - li * si + qr, lr * si + li * sr + qi

        sr, si = lax.fori_loop(0, nchunks, body, (s0re_ref[0], s0im_ref[0]))
        sre_ref[0] = sr
        sim_ref[0] = si

    y = jnp.dot(x, m_ref[0], preferred_element_type=F32)
    y = y + jnp.dot(stre[...].astype(BF16), nre_ref[0], preferred_element_type=F32)
    y = y + jnp.dot(stim[...].astype(BF16), nim_ref[0], preferred_element_type=F32)
    y_ref[0] = y


def _s5_core(xb, folded, s0_re, s0_im, nchunks, rows):
    m, p_re, p_im, n_re, n_im, lam_re, lam_im = folded
    nblk, nr, w = xb.shape
    sl = SSM_BLOCK_GROUPS * SSM_STATE
    wc = min(w, S5_COL_TILE)
    fixed = lambda a, b: pl.BlockSpec((1, a, b), lambda i, j: (i, 0, 0))
    cols = lambda a: pl.BlockSpec((1, a, wc), lambda i, j: (i, 0, j))
    return pl.pallas_call(
        functools.partial(_s5_core_kernel, nchunks=nchunks, rows=rows),
        grid=(nblk, w // wc),
        in_specs=[fixed(nr, w), cols(w), fixed(w, sl), fixed(w, sl), cols(sl), cols(sl),
                  fixed(1, sl), fixed(1, sl), fixed(rows, sl), fixed(rows, sl)],
        out_specs=[cols(nr), fixed(rows, sl), fixed(rows, sl)],
        out_shape=[jax.ShapeDtypeStruct((nblk, nr, w), F32),
                   jax.ShapeDtypeStruct((nblk, rows, sl), F32),
                   jax.ShapeDtypeStruct((nblk, rows, sl), F32)],
        scratch_shapes=[pltpu.VMEM((nr, sl), F32)] * 2,
        compiler_params=_params("parallel", "arbitrary"),
        name="s5_core",
    )(xb, m, p_re, p_im, n_re, n_im, lam_re, lam_im, s0_re, s0_im)


def _s5_out_kernel(x_ref, y_ref, g_ref, d_ref, w_ref, o_ref):
    x = x_ref[...]
    h = _rms(x, g_ref[...])
    z = _gelu(y_ref[...] + d_ref[...] * h)
    gg = jnp.dot(z.astype(BF16), w_ref[...], preferred_element_type=F32)
    o_ref[...] = x + gg[:, :D_MODEL] * _sigmoid(gg[:, D_MODEL:])


def _s5_out(x, y, g, d, w_glu):
    t = x.shape[0]
    tm = min(ROW_TILE, t)
    row = pl.BlockSpec((tm, D_MODEL), lambda i: (i, 0))
    vec = pl.BlockSpec((1, D_MODEL), lambda i: (0, 0))
    return pl.pallas_call(
        _s5_out_kernel,
        grid=(t // tm,),
        in_specs=[row, row, vec, vec, pl.BlockSpec((D_MODEL, 2 * D_MODEL), lambda i: (0, 0))],
        out_specs=row,
        out_shape=jax.ShapeDtypeStruct((t, D_MODEL), F32),
        compiler_params=_params("parallel"),
        name="s5_out",
    )(x, y, g.reshape(1, -1), d.reshape(1, -1), w_glu)


def _s5_mixer(x, s0_re, s0_im, g, folded, d_skip, w_glu, chunk):
    b, s, d = x.shape
    nk = s // chunk
    x2 = x.reshape(b * s, d)
    h = _norm_cast(x2, g)
    xb = (h.reshape(b, nk, chunk, SSM_BLOCKS, LANES).transpose(3, 1, 0, 2, 4)
          .reshape(SSM_BLOCKS, nk * b, chunk * LANES))
    sl = SSM_BLOCK_GROUPS * SSM_STATE
    to_block = lambda z: z.reshape(b, SSM_BLOCKS, sl).transpose(1, 0, 2)
    y, sr, si = _s5_core(xb, folded, to_block(s0_re), to_block(s0_im), nk, b)
    y = y.reshape(SSM_BLOCKS, nk, b, chunk, LANES).transpose(2, 1, 3, 0, 4).reshape(b * s, d)
    from_block = lambda z: z.transpose(1, 0, 2).reshape(b, SSM_GROUPS, SSM_STATE)
    out = _s5_out(x2, y, g, d_skip, w_glu)
    return out.reshape(b, s, d), from_block(sr), from_block(si)


CONV_HEAD = 8


def _conv_prompt_kernel(x_ref, g_ref, win_ref, wc_ref, wout_ref, o_ref, tail_ref, vbuf, *, tm):
    @pl.when(pl.program_id(1) == 0)
    def _():
        vbuf[0:CONV_HEAD, :] = jnp.zeros((CONV_HEAD, D_MODEL), F32)

    x = x_ref[0]
    h = _rms(x, g_ref[...])
    p = jnp.dot(h.astype(BF16), win_ref[...], preferred_element_type=F32)
    bg = p[:, :D_MODEL]
    v = p[:, D_MODEL:2 * D_MODEL] * p[:, 2 * D_MODEL:]
    vbuf[CONV_HEAD:CONV_HEAD + tm, :] = v
    v2 = vbuf[CONV_HEAD - 2:CONV_HEAD - 2 + tm, :]
    v1 = vbuf[CONV_HEAD - 1:CONV_HEAD - 1 + tm, :]
    conv = wc_ref[0:1, :] * v2 + wc_ref[1:2, :] * v1 + wc_ref[2:3, :] * v
    out = jnp.dot((bg * conv).astype(BF16), wout_ref[...], preferred_element_type=F32)
    o_ref[0] = x + out
    tail = vbuf[tm:tm + CONV_HEAD, :]
    vbuf[0:CONV_HEAD, :] = tail
    tail_ref[0] = tail


def _conv_prompt(x, g, w_in, w_conv, w_out):
    b, s, d = x.shape
    tm = min(ROW_TILE, s)
    return pl.pallas_call(
        functools.partial(_conv_prompt_kernel, tm=tm),
        grid=(b, s // tm),
        in_specs=[pl.BlockSpec((1, tm, d), lambda i, j: (i, j, 0)),
                  pl.BlockSpec((1, d), lambda i, j: (0, 0)),
                  pl.BlockSpec((d, 3 * d), lambda i, j: (0, 0)),
                  pl.BlockSpec((3, d), lambda i, j: (0, 0)),
                  pl.BlockSpec((d, d), lambda i, j: (0, 0))],
        out_specs=[pl.BlockSpec((1, tm, d), lambda i, j: (i, j, 0)),
                   pl.BlockSpec((1, CONV_HEAD, d), lambda i, j: (i, 0, 0))],
        out_shape=[jax.ShapeDtypeStruct((b, s, d), F32),
                   jax.ShapeDtypeStruct((b, CONV_HEAD, d), F32)],
        scratch_shapes=[pltpu.VMEM((CONV_HEAD + tm, d), F32)],
        compiler_params=_params("parallel", "arbitrary"),
        name="conv_prompt",
    )(x, g.reshape(1, d), w_in, w_conv, w_out)


def _conv_sample_kernel(x_ref, buf_ref, g_ref, win_ref, wc_ref, wout_ref, o_ref, nbuf_ref, *, steps, nb):
    x = x_ref[...]
    h = _rms(x, g_ref[...])
    p = jnp.dot(h.astype(BF16), win_ref[...], preferred_element_type=F32)
    bg = p[:, :D_MODEL]
    v = p[:, D_MODEL:2 * D_MODEL] * p[:, 2 * D_MODEL:]
    vp = [buf_ref[0:nb, :], buf_ref[nb:2 * nb, :]] + [v[t * nb:(t + 1) * nb, :] for t in range(steps)]
    conv = jnp.concatenate(
        [wc_ref[0:1, :] * vp[t] + wc_ref[1:2, :] * vp[t + 1] + wc_ref[2:3, :] * vp[t + 2]
         for t in range(steps)], axis=0)
    out = jnp.dot((bg * conv).astype(BF16), wout_ref[...], preferred_element_type=F32)
    o_ref[...] = x + out
    nbuf_ref[0:nb, :] = vp[-2]
    nbuf_ref[nb:2 * nb, :] = vp[-1]


def _conv_sample(x_tm, buf_tm, g, w_in, w_conv, w_out, steps, nb):
    d = D_MODEL
    full = lambda r, c: pl.BlockSpec((r, c), lambda i: (0, 0))
    return pl.pallas_call(
        functools.partial(_conv_sample_kernel, steps=steps, nb=nb),
        grid=(1,),
        in_specs=[full(steps * nb, d), full(2 * nb, d), full(1, d), full(d, 3 * d), full(3, d), full(d, d)],
        out_specs=[full(steps * nb, d), full(2 * nb, d)],
        out_shape=[jax.ShapeDtypeStruct((steps * nb, d), F32), jax.ShapeDtypeStruct((2 * nb, d), F32)],
        compiler_params=_params("arbitrary"),
        name="conv_sample",
    )(x_tm, buf_tm, g.reshape(1, d), w_in, w_conv, w_out)


def _softmax_rows(s):
    e = jnp.exp(s - jnp.max(s, axis=-1, keepdims=True))
    return e / jnp.sum(e, axis=-1, keepdims=True)


def _attn_heads(q, k, v):
    k = k.astype(BF16)
    v = v.astype(BF16)
    outs = []
    for hd in range(MEM_HEADS):
        sl = slice(hd * MEM_HEAD_DIM, (hd + 1) * MEM_HEAD_DIM)
        s = lax.dot_general(q[:, sl].astype(BF16), k[:, sl], _NT, preferred_element_type=F32)
        p = _softmax_rows(s * (MEM_HEAD_DIM ** -0.5))
        outs.append(jnp.dot(p.astype(BF16), v[:, sl], preferred_element_type=F32))
    return jnp.concatenate(outs, axis=-1)


def _attn_prompt_kernel(x_ref, g_ref, k_ref, v_ref, wq_ref, wo_ref, o_ref):
    x = x_ref[0]
    h = _rms(x, g_ref[...])
    q = jnp.dot(h.astype(BF16), wq_ref[...], preferred_element_type=F32)
    o = _attn_heads(q, k_ref[0], v_ref[0])
    o_ref[0] = x + jnp.dot(o.astype(BF16), wo_ref[...], preferred_element_type=F32)


def _attn_prompt(x, g, mem_k, mem_v, layer, w_q, w_o):
    b, s, d = x.shape
    tm = min(ROW_TILE, s)
    mem = pl.BlockSpec((None, 1, MEM_TOKENS, d), lambda i, j: (layer, i, 0, 0))
    wgt = pl.BlockSpec((d, d), lambda i, j: (0, 0))
    row = pl.BlockSpec((1, tm, d), lambda i, j: (i, j, 0))
    return pl.pallas_call(
        _attn_prompt_kernel,
        grid=(b, s // tm),
        in_specs=[row, pl.BlockSpec((1, d), lambda i, j: (0, 0)), mem, mem, wgt, wgt],
        out_specs=row,
        out_shape=jax.ShapeDtypeStruct((b, s, d), F32),
        compiler_params=_params("parallel", "parallel"),
        name="attn_prompt",
    )(x, g.reshape(1, d), mem_k, mem_v, w_q, w_o)


SAMPLE_Q_ROWS = 8
SAMPLE_SEQ_BLOCK = 4


def _attn_sample_kernel(q_ref, k_ref, v_ref, o_ref):
    nrow = SAMPLE_Q_ROWS * MEM_HEADS
    ncol = MEM_TOKENS * MEM_HEADS
    rows = lax.broadcasted_iota(jnp.int32, (nrow, ncol), 0)
    cols = lax.broadcasted_iota(jnp.int32, (nrow, ncol), 1)
    same_head = (rows % MEM_HEADS) == (cols % MEM_HEADS)

    def body(i, carry):
        k = k_ref[i].reshape(ncol, MEM_HEAD_DIM).astype(BF16)
        v = v_ref[i].reshape(ncol, MEM_HEAD_DIM).astype(BF16)
        s = lax.dot_general(q_ref[i].astype(BF16), k, _NT, preferred_element_type=F32)
        p = _softmax_rows(jnp.where(same_head, s * (MEM_HEAD_DIM ** -0.5), -jnp.inf))
        o_ref[i] = jnp.dot(p.astype(BF16), v, preferred_element_type=F32)
        return carry

    lax.fori_loop(0, SAMPLE_SEQ_BLOCK, body, 0)


def _attn_sample(x, g, cache_k, cache_v, layer, w_q, w_o):
    b, s, d = x.shape
    x2 = x.reshape(b * s, d)
    q = _mm(x2, w_q, g=g).reshape(b, s, MEM_HEADS, MEM_HEAD_DIM)
    q = jnp.pad(q, ((0, 0), (0, SAMPLE_Q_ROWS - s), (0, 0), (0, 0)))
    q = q.reshape(b, SAMPLE_Q_ROWS * MEM_HEADS, MEM_HEAD_DIM)
    qblk = pl.BlockSpec((SAMPLE_SEQ_BLOCK, SAMPLE_Q_ROWS * MEM_HEADS, MEM_HEAD_DIM), lambda i: (i, 0, 0))
    mblk = pl.BlockSpec((None, SAMPLE_SEQ_BLOCK, MEM_TOKENS, MEM_HEADS, MEM_HEAD_DIM),
                        lambda i: (layer, i, 0, 0, 0))
    o = pl.pallas_call(
        _attn_sample_kernel,
        grid=(b // SAMPLE_SEQ_BLOCK,),
        in_specs=[qblk, mblk, mblk],
        out_specs=qblk,
        out_shape=jax.ShapeDtypeStruct((b, SAMPLE_Q_ROWS * MEM_HEADS, MEM_HEAD_DIM), F32),
        compiler_params=_params("parallel"),
        name="attn_sample",
    )(q, cache_k, cache_v)
    o = o.reshape(b, SAMPLE_Q_ROWS, d)[:, :s].reshape(b * s, d)
    return _mm(o, w_o, res=x2).reshape(b, s, d)


def _top16(s):
    iota = lax.broadcasted_iota(jnp.int32, s.shape, 0)
    pos = jnp.full(s.shape, PEER_TOPK, jnp.int32)
    vals = []
    for i in range(PEER_TOPK):
        m = jnp.max(s, axis=0, keepdims=True)
        idx = jnp.min(jnp.where(s == m, iota, s.shape[0]), axis=0, keepdims=True)
        hit = iota == idx
        pos = jnp.where(hit, i, pos)
        s = jnp.where(hit, -jnp.inf, s)
        vals.append(m)
    return vals, pos


def _pair_select(vals1, vals2):
    lanes = vals1[0].shape[1]
    iota16 = lax.broadcasted_iota(jnp.int32, (PEER_TOPK, lanes), 0)
    v2 = jnp.zeros((PEER_TOPK, lanes), F32)
    for b in range(PEER_TOPK):
        v2 = jnp.where(iota16 == b, vals2[b], v2)
    iota8 = iota16[:8]
    cand = [vals1[0] + v2]
    flat = [iota16]
    for a in range(1, PEER_TOPK):
        cand.append(jnp.where(iota8 < PEER_TOPK // (a + 1), vals1[a] + v2[:8], -jnp.inf))
        flat.append(iota8 + a * PEER_TOPK)
    nsel = jnp.zeros((PEER_TOPK, lanes), jnp.int32)
    z = jnp.zeros((1, lanes), F32)
    top = vals1[0] + vals2[0]
    nflat = PEER_TOPK * PEER_TOPK
    for _ in range(PEER_TOPK):
        mx = jnp.maximum(cand[0][:8], cand[0][8:])
        for a in range(1, PEER_TOPK):
            mx = jnp.maximum(mx, cand[a])
        m = jnp.max(mx, axis=0, keepdims=True)
        w0 = jnp.where(cand[0] == m, flat[0], nflat)
        mi = jnp.minimum(w0[:8], w0[8:])
        for a in range(1, PEER_TOPK):
            mi = jnp.minimum(mi, jnp.where(cand[a] == m, flat[a], nflat))
        idx = jnp.min(mi, axis=0, keepdims=True)
        cand = [jnp.where(flat[a] == idx, -jnp.inf, cand[a]) for a in range(PEER_TOPK)]
        nsel = nsel + (iota16 == (idx >> 4)).astype(jnp.int32)
        z = z + jnp.exp(m - top)
    return nsel, z


def _sort_network(n):
    pairs = []

    def merge(lo, cnt, r):
        m = r * 2
        if m < cnt:
            merge(lo, cnt, m)
            merge(lo + r, cnt, m)
            for i in range(lo + r, lo + cnt - r, m):
                pairs.append((i, i + r))
        else:
            pairs.append((lo, lo + r))

    def sort(lo, cnt):
        if cnt > 1:
            m = cnt // 2
            sort(lo, m)
            sort(lo + m, m)
            merge(lo, cnt, 1)

    sort(0, n)
    return tuple(pairs)


_SORT16 = _sort_network(PEER_TOPK)


def _exchange(v, i, j):
    v[i], v[j] = jnp.maximum(v[i], v[j]), jnp.minimum(v[i], v[j])


def _allreduce_rows(x, op):
    for shift in (4, 2, 1):
        x = op(x, pltpu.roll(x, shift, 0))
    return x


def _sorted_top16(s):
    v = [s[SUBLANES * j:SUBLANES * (j + 1), :] for j in range(PEER_TOPK)]
    for i, j in _SORT16:
        _exchange(v, i, j)
    for shift in (4, 2, 1):
        r = [pltpu.roll(x, shift, 0) for x in v]
        v = [jnp.maximum(v[i], r[PEER_TOPK - 1 - i]) for i in range(PEER_TOPK)]
        for stride in (8, 4, 2, 1):
            for i in range(PEER_TOPK):
                if i & stride == 0:
                    _exchange(v, i, i + stride)
    return v


def _rank_bits(x, v):
    b3 = v[7] > x
    b2 = jnp.where(b3, v[11], v[3]) > x
    b1 = jnp.where(b3, jnp.where(b2, v[13], v[9]), jnp.where(b2, v[5], v[1])) > x
    t = jnp.where(b3,
                  jnp.where(b2, jnp.where(b1, v[14], v[12]), jnp.where(b1, v[10], v[8])),
                  jnp.where(b2, jnp.where(b1, v[6], v[4]), jnp.where(b1, v[2], v[0])))
    return (b3, b2, b1, t > x), v[PEER_TOPK - 1] > x


def _select16(bits, vals):
    b3, b2, b1, b0 = bits
    lvl = [jnp.where(b0, vals[2 * i + 1], vals[2 * i]) for i in range(8)]
    lvl = [jnp.where(b1, lvl[2 * i + 1], lvl[2 * i]) for i in range(4)]
    lvl = [jnp.where(b2, lvl[2 * i + 1], lvl[2 * i]) for i in range(2)]
    return jnp.where(b3, lvl[1], lvl[0])


def _pair_counts(v1, v2):
    sub = lax.broadcasted_iota(jnp.int32, v1[0].shape, 0)
    ninf = -jnp.inf

    def column(v, off):
        col = v[off]
        for b in range(1, SUBLANES):
            col = jnp.where(sub == b, v[off + b], col)
        return col

    v2c0, v2c1 = column(v2, 0), column(v2, SUBLANES)
    v1c0, v1c1 = column(v1, 0), column(v1, SUBLANES)
    rows = [[v1[0] + v2c0, v1[0] + v2c1], [v1[1] + v2c0],
            [jnp.where(sub < 5, v1[2] + v2c0, ninf)], [jnp.where(sub < 4, v1[3] + v2c0, ninf)]]
    cols = [jnp.where(sub >= 4, v1c0 + v2[0], ninf), v1c1 + v2[0],
            jnp.where(sub >= 4, v1c0 + v2[1], ninf), jnp.where(sub == 4, v1c0 + v2[2], ninf)]
    cur = [c for r in rows for c in r] + cols
    top = v1[0] + v2[0]
    z = jnp.zeros_like(top)
    for _ in range(PEER_TOPK):
        m = cur[0]
        for c in cur[1:]:
            m = jnp.maximum(m, c)
        m = _allreduce_rows(m, jnp.maximum)
        z = z + jnp.exp(m - top)
        cur = [jnp.where(c == m, ninf, c) for c in cur]
    tau = m
    count = lambda c: jnp.where(c >= tau, 1.0, 0.0)
    nsel = []
    for r in rows:
        f = count(r[0])
        for c in r[1:]:
            f = f + count(c)
        nsel.append(_allreduce_rows(f, jnp.add))
    for a in range(4, PEER_TOPK):
        f = count(v1[a] + v2[0])
        for b in range(1, PEER_TOPK // (a + 1)):
            f = f + count(v1[a] + v2[b])
        nsel.append(f)
    total = nsel[0]
    for f in nsel[1:]:
        total = total + f
    return nsel, z, total != float(PEER_TOPK)


def _key_ranks(s, v, lookup=None):
    outs = []
    ranked = None
    for j in range(PEER_TOPK):
        x = s[SUBLANES * j:SUBLANES * (j + 1), :]
        bits, low = _rank_bits(x, v)
        if lookup is None:
            b3, b2, b1, b0 = bits
            val = (jnp.where(b3, 8.0, 0.0) + jnp.where(b2, 4.0, 0.0)
                   + jnp.where(b1, 2.0, 0.0) + jnp.where(b0, 1.0, 0.0) + jnp.where(low, 1.0, 0.0))
        else:
            val = jnp.where(low, 0.0, _select16(bits, lookup))
        outs.append(val)
        r = jnp.where(low, 0.0, 1.0)
        ranked = r if ranked is None else ranked + r
    tie = _allreduce_rows(ranked, jnp.add) != float(PEER_TOPK)
    for a in range(PEER_TOPK - 1):
        tie = tie | (v[a] == v[a + 1])
    return jnp.concatenate(outs, axis=0), tie


def _peer_select_kernel(x_ref, g_ref, wq_ref, k1_ref, k2_ref,
                        pos2_ref, m2_ref, nsel_ref, m1_ref, q_scr, s1_scr, s2_scr, *, tm):
    h = _rms(x_ref[...], g_ref[...]).astype(BF16)
    q = jnp.dot(h, wq_ref[...], preferred_element_type=F32)
    for j in range(2 * PEER_HEADS):
        q_scr[j] = q[:, j * PEER_KEYS:(j + 1) * PEER_KEYS].astype(BF16)
    k1 = k1_ref[...]
    k2 = k2_ref[...]

    def head(hh, carry):
        s1_scr[...] = lax.dot_general(k1, q_scr[2 * hh], _NT, preferred_element_type=F32)
        s2_scr[...] = lax.dot_general(k2, q_scr[2 * hh + 1], _NT, preferred_element_type=F32)
        tie = None
        for c in range(tm // LANES):
            sl = slice(c * LANES, (c + 1) * LANES)
            s1, s2 = s1_scr[:, sl], s2_scr[:, sl]
            v1 = _sorted_top16(s1)
            v2 = _sorted_top16(s2)
            nsel, z, t0 = _pair_counts(v1, v2)
            pos2, t2 = _key_ranks(s2, v2)
            nfull, t1 = _key_ranks(s1, v1, lookup=nsel)
            t = t0 | t1 | t2
            tie = t if tie is None else tie | t
            pos2_ref[hh, :, sl] = pos2.astype(BF16)
            m2_ref[hh, :, sl] = jnp.exp(s2 - v2[0][0:1, :]).astype(BF16)
            nsel_ref[hh, :, sl] = nfull
            m1_ref[hh, :, sl] = jnp.exp(s1 - v1[0][0:1, :]) * (1.0 / z[0:1, :])

        @pl.when(jnp.max(jnp.where(tie, 1.0, 0.0)) > 0.0)
        def _():
            for c in range(tm // LANES):
                sl = slice(c * LANES, (c + 1) * LANES)
                s1, s2 = s1_scr[:, sl], s2_scr[:, sl]
                vals1, pos1 = _top16(s1)
                vals2, pos2 = _top16(s2)
                nsel, z = _pair_select(vals1, vals2)
                nfull = jnp.zeros(s1.shape, jnp.int32)
                for a in range(PEER_TOPK):
                    nfull = jnp.where(pos1 == a, nsel[a:a + 1, :], nfull)
                pos2_ref[hh, :, sl] = pos2.astype(F32).astype(BF16)
                m2_ref[hh, :, sl] = jnp.exp(s2 - vals2[0]).astype(BF16)
                nsel_ref[hh, :, sl] = nfull.astype(F32)
                m1_ref[hh, :, sl] = jnp.exp(s1 - vals1[0]) * (1.0 / z)

        return carry

    lax.fori_loop(0, PEER_HEADS, head, 0)


PEER_SELECT_TILE = 256


def _peer_select(x, g, w_query, key1, key2):
    t, d = x.shape
    tm = PEER_SELECT_TILE
    nq = w_query.shape[1]
    head = pl.BlockSpec((PEER_HEADS, PEER_KEYS, tm), lambda i: (0, 0, i))
    keyspec = pl.BlockSpec((PEER_KEYS, PEER_KEYS), lambda i: (0, 0))
    shp = lambda dt: jax.ShapeDtypeStruct((PEER_HEADS, PEER_KEYS, t), dt)
    return pl.pallas_call(
        functools.partial(_peer_select_kernel, tm=tm),
        grid=(t // tm,),
        in_specs=[pl.BlockSpec((tm, d), lambda i: (i, 0)), pl.BlockSpec((1, d), lambda i: (0, 0)),
                  pl.BlockSpec((d, nq), lambda i: (0, 0)), keyspec, keyspec],
        out_specs=[head] * 4,
        out_shape=[shp(BF16), shp(BF16), shp(F32), shp(F32)],
        scratch_shapes=[pltpu.VMEM((2 * PEER_HEADS, tm, PEER_KEYS), BF16),
                        pltpu.VMEM((PEER_KEYS, tm), F32), pltpu.VMEM((PEER_KEYS, tm), F32)],
        compiler_params=_params("parallel"),
        name="peer_select",
    )(x, g.reshape(1, d), w_query, key1, key2)


def _zero_after(t):
    bits = pltpu.bitcast(t[0:2 * SUBLANES, :], jnp.uint32)
    return jnp.max(((bits >> 16) >> 16).astype(jnp.int32)).astype(F32).astype(BF16)


PEER_GATE_ROWS = 1


def _peer_dense_kernel(x_ref, g_ref, gf_ref, u_ref, vt_ref, pos2_ref, m2_ref, nsel_ref, m1_ref,
                       o_ref, h_scr, acc_scr, p2_scr, m2_scr, *, tm, te, final_norm):
    j = pl.program_id(1)

    @pl.when(j == 0)
    def _():
        h_scr[...] = _rms(x_ref[...], g_ref[...]).astype(BF16)
        acc_scr[...] = jnp.zeros(acc_scr.shape, F32)
        p2_scr[...] = pos2_ref[...]
        m2_scr[...] = m2_ref[...]

    n1 = te // PEER_KEYS
    first = pl.multiple_of(j * n1, n1)
    half = n1 // 2

    def activations(r0, r1, gate=None):
        hop = h_scr[...]
        if gate is not None:
            hop = hop + gate
        return lax.dot_general(u_ref[r0 * PEER_KEYS:r1 * PEER_KEYS, :], hop, _NT, preferred_element_type=F32)

    def weighted(a, base, r0, r1):
        blocks = []
        for r in range(r0, r1):
            tiles = []
            for ct in range(tm // LANES):
                sl = slice(ct * LANES, (ct + 1) * LANES)
                w = None
                for hh in range(PEER_HEADS):
                    nrow = nsel_ref[hh, pl.ds(first, n1), sl][r:r + 1, :].astype(BF16)
                    mrow = m1_ref[hh, pl.ds(first, n1), sl][r:r + 1, :].astype(BF16)
                    t = jnp.where(p2_scr[hh, :, sl] < nrow, m2_scr[hh, :, sl] * mrow, 0)
                    w = t if w is None else w + t
                at = a[(r - base) * PEER_KEYS:(r - base + 1) * PEER_KEYS, sl]
                tiles.append(_gelu(at).astype(BF16) * w)
            blocks.append(jnp.concatenate(tiles, axis=1))
        return blocks

    a_top = activations(0, half)
    z = weighted(a_top, 0, 0, PEER_GATE_ROWS)
    a_bot = activations(half, n1, gate=_zero_after(z[-1]))
    z = z + weighted(a_top, 0, PEER_GATE_ROWS, half) + weighted(a_bot, half, half, n1)
    acc_scr[...] += jnp.dot(vt_ref[...], jnp.concatenate(z, axis=0), preferred_element_type=F32)

    @pl.when(j == pl.num_programs(1) - 1)
    def _():
        out = x_ref[...] + acc_scr[...].T
        if final_norm:
            out = _rms(out, gf_ref[...])
        o_ref[...] = out


def _peer_dense(x, g, g_final, u, vt, sel, final_norm):
    t, d = x.shape
    tm = min(ROW_TILE, t)
    te = vt.shape[2]
    assert (te // PEER_KEYS) % SUBLANES == 0
    row = pl.BlockSpec((tm, d), lambda i, j: (i, 0))
    vec = pl.BlockSpec((1, d), lambda i, j: (0, 0))
    head = pl.BlockSpec((PEER_HEADS, PEER_KEYS, tm), lambda i, j: (0, 0, i))
    return pl.pallas_call(
        functools.partial(_peer_dense_kernel, tm=tm, te=te, final_norm=final_norm),
        grid=(t // tm, PEER_EXPERTS // te),
        in_specs=[row, vec, vec, pl.BlockSpec((te, d), lambda i, j: (j, 0)),
                  pl.BlockSpec((None, d, te), lambda i, j: (j, 0, 0)), head, head, head, head],
        out_specs=row,
        out_shape=jax.ShapeDtypeStruct((t, d), F32),
        scratch_shapes=[pltpu.VMEM((tm, d), BF16), pltpu.VMEM((d, tm), F32),
                        pltpu.VMEM((PEER_HEADS, PEER_KEYS, tm), BF16),
                        pltpu.VMEM((PEER_HEADS, PEER_KEYS, tm), BF16)],
        compiler_params=_params("parallel", "arbitrary"),
        name="peer_dense",
    )(x, g.reshape(1, d), g_final.reshape(1, d), u, vt, *sel)


def _peer(x, g, g_final, w_query, key1, key2, u, vt, final_norm):
    shp = x.shape
    x2 = x.reshape(-1, shp[-1])
    sel = _peer_select(x2, g, w_query, key1, key2)
    return _peer_dense(x2, g, g_final, u, vt, sel, final_norm).reshape(shp)


def _trunk(x, ssm0, conv0, mem_k, mem_v, w, sample):
    b, s, d = x.shape
    chunk = s if sample else S5_CHUNK
    if ssm0 is None:
        ssm0 = (jnp.zeros((b, SSM_GROUPS, SSM_STATE), F32),) * 2
    x, ssm_re, ssm_im = _s5_mixer(x, ssm0[0], ssm0[1], w['norm_mix'][0], w['s5_fold'][chunk],
                                  w['ssm_d'][0], w['ssm_w_glu'][0], chunk)
    attn = _attn_sample if sample else _attn_prompt
    for i in range(2):
        if i == 1:
            if sample:
                x_tm = x.transpose(1, 0, 2).reshape(s * b, d)
                buf_tm = conv0.transpose(1, 0, 2).reshape(2 * b, d)
                x_tm, nbuf = _conv_sample(x_tm, buf_tm, w['norm_mix'][1], w['conv_w_in'][0],
                                          w['conv_w'][0], w['conv_w_out'][0], s, b)
                x = x_tm.reshape(s, b, d).transpose(1, 0, 2)
                conv_out = nbuf.reshape(2, b, d).transpose(1, 0, 2)
            else:
                x, tail = _conv_prompt(x, w['norm_mix'][1], w['conv_w_in'][0], w['conv_w'][0],
                                       w['conv_w_out'][0])
                conv_out = tail[:, CONV_HEAD - 2:]
        x = attn(x, w['norm_mem'][i], mem_k, mem_v, i, w['mem_w_q'][i], w['mem_w_o'][i])
        x = _peer(x, w['norm_ffn'][i], w['norm_final'], w['peer_w_query'][i], w['peer_key1'][i],
                  w['peer_key2'][i], w['peer_u'][i], w['peer_vt'][i], final_norm=(i == 1))
    return x, ssm_re[None], ssm_im[None], conv_out[None]


def kernel(x_prompt, x_sample, mem_prompt, state_ssm_re, state_ssm_im, state_conv, cache_mem_k, cache_mem_v, norm_mix, norm_mem, norm_ffn, norm_final, ssm_a_re, ssm_a_im, ssm_log_dt, ssm_b_re, ssm_b_im, ssm_c_re, ssm_c_im, ssm_d, ssm_w_glu, conv_w_in, conv_w, conv_w_out, mem_w_q, mem_w_k, mem_w_v, mem_w_o, peer_w_query, peer_key1, peer_key2, peer_u, peer_v):
    bsz, seq, d = x_prompt.shape
    dec_b, dec_s, _ = x_sample.shape
    depth = mem_w_q.shape[0]
    fold = lambda chunk: _s5_fold(ssm_a_re[0], ssm_a_im[0], ssm_log_dt[0], ssm_b_re[0], ssm_b_im[0],
                                  ssm_c_re[0], ssm_c_im[0], chunk)
    w = dict(
        norm_mix=norm_mix, norm_mem=norm_mem, norm_ffn=norm_ffn, norm_final=norm_final,
        s5_fold={S5_CHUNK: fold(S5_CHUNK), dec_s: fold(dec_s)},
        ssm_d=ssm_d, ssm_w_glu=ssm_w_glu.astype(BF16),
        conv_w_in=conv_w_in.astype(BF16), conv_w=conv_w, conv_w_out=conv_w_out.astype(BF16),
        mem_w_q=mem_w_q.astype(BF16), mem_w_o=mem_w_o.astype(BF16),
        peer_w_query=peer_w_query.astype(BF16), peer_key1=peer_key1.astype(BF16),
        peer_key2=peer_key2.astype(BF16), peer_u=peer_u.astype(BF16),
        peer_vt=peer_v.astype(BF16).reshape(depth, PEER_EXPERTS // EXPERT_TILE, EXPERT_TILE, d)
        .transpose(0, 1, 3, 2),
    )
    w_kv = jnp.concatenate([mem_w_k[i] for i in range(depth)] + [mem_w_v[i] for i in range(depth)],
                           axis=1).astype(BF16)
    kv = _mm(mem_prompt.reshape(bsz * MEM_TOKENS, d), w_kv, tm=256)
    kv = kv.reshape(bsz, MEM_TOKENS, 2 * depth, d).transpose(2, 0, 1, 3)
    mem_k_p, mem_v_p = kv[:depth], kv[depth:]

    y_p, re_p, im_p, conv_p = _trunk(x_prompt, None, None, mem_k_p, mem_v_p, w, sample=False)
    y_s, re_s, im_s, conv_s = _trunk(x_sample, (state_ssm_re[0], state_ssm_im[0]), state_conv[0],
                                     cache_mem_k, cache_mem_v, w, sample=True)
    head_shape = (depth, bsz, MEM_TOKENS, MEM_HEADS, MEM_HEAD_DIM)
    return (y_p, y_s, re_p, im_p, conv_p, mem_k_p.reshape(head_shape), mem_v_p.reshape(head_shape),
            re_s, im_s, conv_s)
```

```python
import functools
import math

import jax
import jax.numpy as jnp
from jax import lax
from jax.experimental import pallas as pl
from jax.experimental.pallas import tpu as pltpu

F32 = jnp.float32
BF16 = jnp.bfloat16

D_MODEL = 1024
SSM_GROUP = 16
SSM_GROUPS = D_MODEL // SSM_GROUP
SSM_STATE = 64
SSM_BLOCK_GROUPS = 8
SSM_BLOCKS = SSM_GROUPS // SSM_BLOCK_GROUPS
MEM_TOKENS = 256
MEM_HEADS = 4
MEM_HEAD_DIM = D_MODEL // MEM_HEADS
PEER_HEADS = 8
PEER_KEYS = 128
PEER_EXPERTS = PEER_KEYS * PEER_KEYS
PEER_TOPK = 16
RMS_EPS = 1e-6

LANES = 128
SUBLANES = 8
ROW_TILE = 512
S5_CHUNK = 8
S5_COL_TILE = 512
EXPERT_TILE = 2048
VMEM_LIMIT = 48 * 1024 * 1024

_NT = (((1,), (1,)), ((), ()))


def _params(*sem):
    return pltpu.CompilerParams(dimension_semantics=sem, vmem_limit_bytes=VMEM_LIMIT)


def _rms(x, g):
    r = lax.rsqrt(jnp.mean(x * x, axis=-1, keepdims=True) + RMS_EPS)
    return (x * r) * g


def _gelu(x):
    return 0.5 * x * (1.0 + lax.erf(x * (1.0 / math.sqrt(2.0))))


def _sigmoid(x):
    return 1.0 / (1.0 + jnp.exp(-x))


def _norm_cast_kernel(x_ref, g_ref, o_ref):
    o_ref[...] = _rms(x_ref[...], g_ref[...]).astype(BF16)


def _norm_cast(x, g):
    t, d = x.shape
    tm = min(ROW_TILE, t)
    return pl.pallas_call(
        _norm_cast_kernel,
        grid=(t // tm,),
        in_specs=[pl.BlockSpec((tm, d), lambda i: (i, 0)), pl.BlockSpec((1, d), lambda i: (0, 0))],
        out_specs=pl.BlockSpec((tm, d), lambda i: (i, 0)),
        out_shape=jax.ShapeDtypeStruct((t, d), BF16),
        compiler_params=_params("parallel"),
        name="norm_cast",
    )(x, g.reshape(1, d))


def _mm_kernel(*refs, has_norm, has_res):
    x_ref, w_ref = refs[0], refs[1]
    pos = 2
    x = x_ref[...]
    if has_norm:
        x = _rms(x, refs[pos][...])
        pos += 1
    y = jnp.dot(x.astype(BF16), w_ref[...], preferred_element_type=F32)
    if has_res:
        y = y + refs[pos][...]
        pos += 1
    refs[pos][...] = y


def _mm(x, w, g=None, res=None, tm=ROW_TILE):
    t, k = x.shape
    n = w.shape[1]
    tm = min(tm, t)
    args = [x, w]
    specs = [pl.BlockSpec((tm, k), lambda i: (i, 0)), pl.BlockSpec((k, n), lambda i: (0, 0))]
    if g is not None:
        args.append(g.reshape(1, k))
        specs.append(pl.BlockSpec((1, k), lambda i: (0, 0)))
    if res is not None:
        args.append(res)
        specs.append(pl.BlockSpec((tm, n), lambda i: (i, 0)))
    return pl.pallas_call(
        functools.partial(_mm_kernel, has_norm=g is not None, has_res=res is not None),
        grid=(t // tm,),
        in_specs=specs,
        out_specs=pl.BlockSpec((tm, n), lambda i: (i, 0)),
        out_shape=jax.ShapeDtypeStruct((t, n), F32),
        compiler_params=_params("parallel"),
        name="mm_rows",
    )(*args)


def _mm_stacked_kernel(x_ref, w_ref, o_ref):
    o_ref[...] = jnp.dot(x_ref[...].astype(BF16), w_ref[...], preferred_element_type=F32)


def _mm_stacked(x, w, tm=ROW_TILE):
    t, k = x.shape
    n, _, c = w.shape
    tm = min(tm, t)
    return pl.pallas_call(
        _mm_stacked_kernel,
        grid=(n, t // tm),
        in_specs=[pl.BlockSpec((tm, k), lambda j, i: (i, 0)), pl.BlockSpec((None, k, c), lambda j, i: (j, 0, 0))],
        out_specs=pl.BlockSpec((None, tm, c), lambda j, i: (j, i, 0)),
        out_shape=jax.ShapeDtypeStruct((n, t, c), F32),
        compiler_params=_params("parallel", "parallel"),
        name="mm_stacked",
    )(x, w)


def _s5_fold(a_re, a_im, log_dt, b_re, b_im, c_re, c_im, chunk):
    hi = lax.Precision.HIGHEST
    dt = jnp.exp(log_dt)[:, None]
    mag = jnp.exp(a_re * dt)
    ang = a_im * dt
    lb_re = mag * jnp.cos(ang)
    lb_im = mag * jnp.sin(ang)
    den = a_re * a_re + a_im * a_im
    f_re = ((lb_re - 1.0) * a_re + lb_im * a_im) / den
    f_im = (lb_im * a_re - (lb_re - 1.0) * a_im) / den
    bb_re = f_re[..., None] * b_re - f_im[..., None] * b_im
    bb_im = f_re[..., None] * b_im + f_im[..., None] * b_re
    pw_re, pw_im = [jnp.ones_like(lb_re)], [jnp.zeros_like(lb_im)]
    for _ in range(chunk):
        pr, pi = pw_re[-1], pw_im[-1]
        pw_re.append(pr * lb_re - pi * lb_im)
        pw_im.append(pr * lb_im + pi * lb_re)
    pw_re, pw_im = jnp.stack(pw_re), jnp.stack(pw_im)
    cl_re = c_re[None] * pw_re[:, :, None, :] - c_im[None] * pw_im[:, :, None, :]
    cl_im = c_re[None] * pw_im[:, :, None, :] + c_im[None] * pw_re[:, :, None, :]
    kern = (jnp.einsum('kgdp,gpc->kgdc', cl_re[:chunk], bb_re, precision=hi)
            - jnp.einsum('kgdp,gpc->kgdc', cl_im[:chunk], bb_im, precision=hi))
    step = jnp.arange(chunk)
    rp_re, rp_im = pw_re[chunk - 1 - step], pw_im[chunk - 1 - step]
    p_re = rp_re[..., None] * bb_re[None] - rp_im[..., None] * bb_im[None]
    p_im = rp_re[..., None] * bb_im[None] + rp_im[..., None] * bb_re[None]
    p_re = p_re.transpose(1, 0, 3, 2)
    p_im = p_im.transpose(1, 0, 3, 2)
    n_re = cl_re[1:].transpose(1, 3, 0, 2)
    n_im = (-cl_im[1:]).transpose(1, 3, 0, 2)

    nb, q = SSM_BLOCKS, SSM_BLOCK_GROUPS
    eye = jnp.eye(q, dtype=BF16)
    w = chunk * q * SSM_GROUP
    sl = q * SSM_STATE
    kq = kern.transpose(1, 3, 0, 2).astype(BF16).reshape(nb, q, SSM_GROUP, chunk, 1, SSM_GROUP)
    kq = (kq * eye[None, :, None, None, :, None]).reshape(nb, q, SSM_GROUP, chunk, q * SSM_GROUP)
    kq = jnp.pad(kq, ((0, 0), (0, 0), (0, 0), (chunk - 1, 0), (0, 0)))
    m = jnp.stack([kq[:, :, :, chunk - 1 - t:2 * chunk - 1 - t] for t in range(chunk)], axis=1)
    m = m.reshape(nb, w, w)

    def fold_p(z):
        z = z.astype(BF16).reshape(nb, q, chunk, SSM_GROUP, 1, SSM_STATE).transpose(0, 2, 1, 3, 4, 5)
        return (z * eye[None, None, :, None, :, None]).reshape(nb, w, sl)

    def fold_n(z):
        z = z.astype(BF16).reshape(nb, q, SSM_STATE, chunk, 1, SSM_GROUP)
        return (z * eye[None, :, None, None, :, None]).reshape(nb, sl, w)

    lam_re = pw_re[chunk].reshape(nb, 1, sl)
    lam_im = pw_im[chunk].reshape(nb, 1, sl)
    return m, fold_p(p_re), fold_p(p_im), fold_n(n_re), fold_n(n_im), lam_re, lam_im


def _s5_core_kernel(x_ref, m_ref, pre_ref, pim_ref, nre_ref, nim_ref, lre_ref, lim_ref,
                    s0re_ref, s0im_ref, y_ref, sre_ref, sim_ref, stre, stim, *, nchunks, rows):
    x = x_ref[0]

    @pl.when(pl.program_id(1) == 0)
    def _():
        stre[...] = jnp.dot(x, pre_ref[0], preferred_element_type=F32)
        stim[...] = jnp.dot(x, pim_ref[0], preferred_element_type=F32)
        width = stre.shape[1]
        lr = jnp.broadcast_to(lre_ref[0], (rows, width))
        li = jnp.broadcast_to(lim_ref[0], (rows, width))

        def body(k, carry):
            sr, si = carry
            off = pl.multiple_of(k * rows, rows)
            qr = stre[pl.ds(off, rows), :]
            qi = stim[pl.ds(off, rows), :]
            stre[pl.ds(off, rows), :] = sr
            stim[pl.ds(off, rows), :] = si
            return lr * sr - li * si + qr, lr * si + li * sr + qi

        sr, si = lax.fori_loop(0, nchunks, body, (s0re_ref[0], s0im_ref[0]))
        sre_ref[0] = sr
        sim_ref[0] = si

    y = jnp.dot(x, m_ref[0], preferred_element_type=F32)
    y = y + jnp.dot(stre[...].astype(BF16), nre_ref[0], preferred_element_type=F32)
    y = y + jnp.dot(stim[...].astype(BF16), nim_ref[0], preferred_element_type=F32)
    y_ref[0] = y


def _s5_core(xb, folded, s0_re, s0_im, nchunks, rows):
    m, p_re, p_im, n_re, n_im, lam_re, lam_im = folded
    nblk, nr, w = xb.shape
    sl = SSM_BLOCK_GROUPS * SSM_STATE
    wc = min(w, S5_COL_TILE)
    fixed = lambda a, b: pl.BlockSpec((1, a, b), lambda i, j: (i, 0, 0))
    cols = lambda a: pl.BlockSpec((1, a, wc), lambda i, j: (i, 0, j))
    return pl.pallas_call(
        functools.partial(_s5_core_kernel, nchunks=nchunks, rows=rows),
        grid=(nblk, w // wc),
        in_specs=[fixed(nr, w), cols(w), fixed(w, sl), fixed(w, sl), cols(sl), cols(sl),
                  fixed(1, sl), fixed(1, sl), fixed(rows, sl), fixed(rows, sl)],
        out_specs=[cols(nr), fixed(rows, sl), fixed(rows, sl)],
        out_shape=[jax.ShapeDtypeStruct((nblk, nr, w), F32),
                   jax.ShapeDtypeStruct((nblk, rows, sl), F32),
                   jax.ShapeDtypeStruct((nblk, rows, sl), F32)],
        scratch_shapes=[pltpu.VMEM((nr, sl), F32)] * 2,
        compiler_params=_params("parallel", "arbitrary"),
        name="s5_core",
    )(xb, m, p_re, p_im, n_re, n_im, lam_re, lam_im, s0_re, s0_im)


def _s5_out_kernel(x_ref, y_ref, g_ref, d_ref, w_ref, o_ref):
    x = x_ref[...]
    h = _rms(x, g_ref[...])
    z = _gelu(y_ref[...] + d_ref[...] * h)
    gg = jnp.dot(z.astype(BF16), w_ref[...], preferred_element_type=F32)
    o_ref[...] = x + gg[:, :D_MODEL] * _sigmoid(gg[:, D_MODEL:])


def _s5_out(x, y, g, d, w_glu):
    t = x.shape[0]
    tm = min(ROW_TILE, t)
    row = pl.BlockSpec((tm, D_MODEL), lambda i: (i, 0))
    vec = pl.BlockSpec((1, D_MODEL), lambda i: (0, 0))
    return pl.pallas_call(
        _s5_out_kernel,
        grid=(t // tm,),
        in_specs=[row, row, vec, vec, pl.BlockSpec((D_MODEL, 2 * D_MODEL), lambda i: (0, 0))],
        out_specs=row,
        out_shape=jax.ShapeDtypeStruct((t, D_MODEL), F32),
        compiler_params=_params("parallel"),
        name="s5_out",
    )(x, y, g.reshape(1, -1), d.reshape(1, -1), w_glu)


def _s5_mixer(x, s0_re, s0_im, g, folded, d_skip, w_glu, chunk):
    b, s, d = x.shape
    nk = s // chunk
    x2 = x.reshape(b * s, d)
    h = _norm_cast(x2, g)
    xb = (h.reshape(b, nk, chunk, SSM_BLOCKS, LANES).transpose(3, 1, 0, 2, 4)
          .reshape(SSM_BLOCKS, nk * b, chunk * LANES))
    sl = SSM_BLOCK_GROUPS * SSM_STATE
    to_block = lambda z: z.reshape(b, SSM_BLOCKS, sl).transpose(1, 0, 2)
    y, sr, si = _s5_core(xb, folded, to_block(s0_re), to_block(s0_im), nk, b)
    y = y.reshape(SSM_BLOCKS, nk, b, chunk, LANES).transpose(2, 1, 3, 0, 4).reshape(b * s, d)
    from_block = lambda z: z.transpose(1, 0, 2).reshape(b, SSM_GROUPS, SSM_STATE)
    out = _s5_out(x2, y, g, d_skip, w_glu)
    return out.reshape(b, s, d), from_block(sr), from_block(si)


CONV_HEAD = 8


def _conv_prompt_kernel(x_ref, g_ref, win_ref, wc_ref, wout_ref, o_ref, tail_ref, vbuf, *, tm):
    @pl.when(pl.program_id(1) == 0)
    def _():
        vbuf[0:CONV_HEAD, :] = jnp.zeros((CONV_HEAD, D_MODEL), F32)

    x = x_ref[0]
    h = _rms(x, g_ref[...])
    p = jnp.dot(h.astype(BF16), win_ref[...], preferred_element_type=F32)
    bg = p[:, :D_MODEL]
    v = p[:, D_MODEL:2 * D_MODEL] * p[:, 2 * D_MODEL:]
    vbuf[CONV_HEAD:CONV_HEAD + tm, :] = v
    v2 = vbuf[CONV_HEAD - 2:CONV_HEAD - 2 + tm, :]
    v1 = vbuf[CONV_HEAD - 1:CONV_HEAD - 1 + tm, :]
    conv = wc_ref[0:1, :] * v2 + wc_ref[1:2, :] * v1 + wc_ref[2:3, :] * v
    out = jnp.dot((bg * conv).astype(BF16), wout_ref[...], preferred_element_type=F32)
    o_ref[0] = x + out
    tail = vbuf[tm:tm + CONV_HEAD, :]
    vbuf[0:CONV_HEAD, :] = tail
    tail_ref[0] = tail


def _conv_prompt(x, g, w_in, w_conv, w_out):
    b, s, d = x.shape
    tm = min(ROW_TILE, s)
    return pl.pallas_call(
        functools.partial(_conv_prompt_kernel, tm=tm),
        grid=(b, s // tm),
        in_specs=[pl.BlockSpec((1, tm, d), lambda i, j: (i, j, 0)),
                  pl.BlockSpec((1, d), lambda i, j: (0, 0)),
                  pl.BlockSpec((d, 3 * d), lambda i, j: (0, 0)),
                  pl.BlockSpec((3, d), lambda i, j: (0, 0)),
                  pl.BlockSpec((d, d), lambda i, j: (0, 0))],
        out_specs=[pl.BlockSpec((1, tm, d), lambda i, j: (i, j, 0)),
                   pl.BlockSpec((1, CONV_HEAD, d), lambda i, j: (i, 0, 0))],
        out_shape=[jax.ShapeDtypeStruct((b, s, d), F32),
                   jax.ShapeDtypeStruct((b, CONV_HEAD, d), F32)],
        scratch_shapes=[pltpu.VMEM((CONV_HEAD + tm, d), F32)],
        compiler_params=_params("parallel", "arbitrary"),
        name="conv_prompt",
    )(x, g.reshape(1, d), w_in, w_conv, w_out)


def _conv_sample_kernel(x_ref, buf_ref, g_ref, win_ref, wc_ref, wout_ref, o_ref, nbuf_ref, *, steps, nb):
    x = x_ref[...]
    h = _rms(x, g_ref[...])
    p = jnp.dot(h.astype(BF16), win_ref[...], preferred_element_type=F32)
    bg = p[:, :D_MODEL]
    v = p[:, D_MODEL:2 * D_MODEL] * p[:, 2 * D_MODEL:]
    vp = [buf_ref[0:nb, :], buf_ref[nb:2 * nb, :]] + [v[t * nb:(t + 1) * nb, :] for t in range(steps)]
    conv = jnp.concatenate(
        [wc_ref[0:1, :] * vp[t] + wc_ref[1:2, :] * vp[t + 1] + wc_ref[2:3, :] * vp[t + 2]
         for t in range(steps)], axis=0)
    out = jnp.dot((bg * conv).astype(BF16), wout_ref[...], preferred_element_type=F32)
    o_ref[...] = x + out
    nbuf_ref[0:nb, :] = vp[-2]
    nbuf_ref[nb:2 * nb, :] = vp[-1]


def _conv_sample(x_tm, buf_tm, g, w_in, w_conv, w_out, steps, nb):
    d = D_MODEL
    full = lambda r, c: pl.BlockSpec((r, c), lambda i: (0, 0))
    return pl.pallas_call(
        functools.partial(_conv_sample_kernel, steps=steps, nb=nb),
        grid=(1,),
        in_specs=[full(steps * nb, d), full(2 * nb, d), full(1, d), full(d, 3 * d), full(3, d), full(d, d)],
        out_specs=[full(steps * nb, d), full(2 * nb, d)],
        out_shape=[jax.ShapeDtypeStruct((steps * nb, d), F32), jax.ShapeDtypeStruct((2 * nb, d), F32)],
        compiler_params=_params("arbitrary"),
        name="conv_sample",
    )(x_tm, buf_tm, g.reshape(1, d), w_in, w_conv, w_out)


def _softmax_rows(s):
    e = jnp.exp(s - jnp.max(s, axis=-1, keepdims=True))
    return e / jnp.sum(e, axis=-1, keepdims=True)


def _attn_heads(q, k, v):
    k = k.astype(BF16)
    v = v.astype(BF16)
    outs = []
    for hd in range(MEM_HEADS):
        sl = slice(hd * MEM_HEAD_DIM, (hd + 1) * MEM_HEAD_DIM)
        s = lax.dot_general(q[:, sl].astype(BF16), k[:, sl], _NT, preferred_element_type=F32)
        p = _softmax_rows(s * (MEM_HEAD_DIM ** -0.5))
        outs.append(jnp.dot(p.astype(BF16), v[:, sl], preferred_element_type=F32))
    return jnp.concatenate(outs, axis=-1)


def _attn_prompt_kernel(x_ref, g_ref, k_ref, v_ref, wq_ref, wo_ref, o_ref):
    x = x_ref[0]
    h = _rms(x, g_ref[...])
    q = jnp.dot(h.astype(BF16), wq_ref[...], preferred_element_type=F32)
    o = _attn_heads(q, k_ref[0], v_ref[0])
    o_ref[0] = x + jnp.dot(o.astype(BF16), wo_ref[...], preferred_element_type=F32)


def _attn_prompt(x, g, mem_k, mem_v, layer, w_q, w_o):
    b, s, d = x.shape
    tm = min(ROW_TILE, s)
    mem = pl.BlockSpec((None, 1, MEM_TOKENS, d), lambda i, j: (layer, i, 0, 0))
    wgt = pl.BlockSpec((d, d), lambda i, j: (0, 0))
    row = pl.BlockSpec((1, tm, d), lambda i, j: (i, j, 0))
    return pl.pallas_call(
        _attn_prompt_kernel,
        grid=(b, s // tm),
        in_specs=[row, pl.BlockSpec((1, d), lambda i, j: (0, 0)), mem, mem, wgt, wgt],
        out_specs=row,
        out_shape=jax.ShapeDtypeStruct((b, s, d), F32),
        compiler_params=_params("parallel", "parallel"),
        name="attn_prompt",
    )(x, g.reshape(1, d), mem_k, mem_v, w_q, w_o)


SAMPLE_Q_ROWS = 8
SAMPLE_SEQ_BLOCK = 4


def _attn_sample_kernel(q_ref, k_ref, v_ref, o_ref):
    nrow = SAMPLE_Q_ROWS * MEM_HEADS
    ncol = MEM_TOKENS * MEM_HEADS
    rows = lax.broadcasted_iota(jnp.int32, (nrow, ncol), 0)
    cols = lax.broadcasted_iota(jnp.int32, (nrow, ncol), 1)
    same_head = (rows % MEM_HEADS) == (cols % MEM_HEADS)

    def body(i, carry):
        k = k_ref[i].reshape(ncol, MEM_HEAD_DIM).astype(BF16)
        v = v_ref[i].reshape(ncol, MEM_HEAD_DIM).astype(BF16)
        s = lax.dot_general(q_ref[i].astype(BF16), k, _NT, preferred_element_type=F32)
        p = _softmax_rows(jnp.where(same_head, s * (MEM_HEAD_DIM ** -0.5), -jnp.inf))
        o_ref[i] = jnp.dot(p.astype(BF16), v, preferred_element_type=F32)
        return carry

    lax.fori_loop(0, SAMPLE_SEQ_BLOCK, body, 0)


def _attn_sample(x, g, cache_k, cache_v, layer, w_q, w_o):
    b, s, d = x.shape
    x2 = x.reshape(b * s, d)
    q = _mm(x2, w_q, g=g).reshape(b, s, MEM_HEADS, MEM_HEAD_DIM)
    q = jnp.pad(q, ((0, 0), (0, SAMPLE_Q_ROWS - s), (0, 0), (0, 0)))
    q = q.reshape(b, SAMPLE_Q_ROWS * MEM_HEADS, MEM_HEAD_DIM)
    qblk = pl.BlockSpec((SAMPLE_SEQ_BLOCK, SAMPLE_Q_ROWS * MEM_HEADS, MEM_HEAD_DIM), lambda i: (i, 0, 0))
    mblk = pl.BlockSpec((None, SAMPLE_SEQ_BLOCK, MEM_TOKENS, MEM_HEADS, MEM_HEAD_DIM),
                        lambda i: (layer, i, 0, 0, 0))
    o = pl.pallas_call(
        _attn_sample_kernel,
        grid=(b // SAMPLE_SEQ_BLOCK,),
        in_specs=[qblk, mblk, mblk],
        out_specs=qblk,
        out_shape=jax.ShapeDtypeStruct((b, SAMPLE_Q_ROWS * MEM_HEADS, MEM_HEAD_DIM), F32),
        compiler_params=_params("parallel"),
        name="attn_sample",
    )(q, cache_k, cache_v)
    o = o.reshape(b, SAMPLE_Q_ROWS, d)[:, :s].reshape(b * s, d)
    return _mm(o, w_o, res=x2).reshape(b, s, d)


def _top16(s):
    iota = lax.broadcasted_iota(jnp.int32, s.shape, 0)
    pos = jnp.full(s.shape, PEER_TOPK, jnp.int32)
    vals = []
    for i in range(PEER_TOPK):
        m = jnp.max(s, axis=0, keepdims=True)
        idx = jnp.min(jnp.where(s == m, iota, s.shape[0]), axis=0, keepdims=True)
        hit = iota == idx
        pos = jnp.where(hit, i, pos)
        s = jnp.where(hit, -jnp.inf, s)
        vals.append(m)
    return vals, pos


def _pair_select(vals1, vals2):
    lanes = vals1[0].shape[1]
    iota16 = lax.broadcasted_iota(jnp.int32, (PEER_TOPK, lanes), 0)
    v2 = jnp.zeros((PEER_TOPK, lanes), F32)
    for b in range(PEER_TOPK):
        v2 = jnp.where(iota16 == b, vals2[b], v2)
    iota8 = iota16[:8]
    cand = [vals1[0] + v2]
    flat = [iota16]
    for a in range(1, PEER_TOPK):
        cand.append(jnp.where(iota8 < PEER_TOPK // (a + 1), vals1[a] + v2[:8], -jnp.inf))
        flat.append(iota8 + a * PEER_TOPK)
    nsel = jnp.zeros((PEER_TOPK, lanes), jnp.int32)
    z = jnp.zeros((1, lanes), F32)
    top = vals1[0] + vals2[0]
    nflat = PEER_TOPK * PEER_TOPK
    for _ in range(PEER_TOPK):
        mx = jnp.maximum(cand[0][:8], cand[0][8:])
        for a in range(1, PEER_TOPK):
            mx = jnp.maximum(mx, cand[a])
        m = jnp.max(mx, axis=0, keepdims=True)
        w0 = jnp.where(cand[0] == m, flat[0], nflat)
        mi = jnp.minimum(w0[:8], w0[8:])
        for a in range(1, PEER_TOPK):
            mi = jnp.minimum(mi, jnp.where(cand[a] == m, flat[a], nflat))
        idx = jnp.min(mi, axis=0, keepdims=True)
        cand = [jnp.where(flat[a] == idx, -jnp.inf, cand[a]) for a in range(PEER_TOPK)]
        nsel = nsel + (iota16 == (idx >> 4)).astype(jnp.int32)
        z = z + jnp.exp(m - top)
    return nsel, z


def _sort_network(n):
    pairs = []

    def merge(lo, cnt, r):
        m = r * 2
        if m < cnt:
            merge(lo, cnt, m)
            merge(lo + r, cnt, m)
            for i in range(lo + r, lo + cnt - r, m):
                pairs.append((i, i + r))
        else:
            pairs.append((lo, lo + r))

    def sort(lo, cnt):
        if cnt > 1:
            m = cnt // 2
            sort(lo, m)
            sort(lo + m, m)
            merge(lo, cnt, 1)

    sort(0, n)
    return tuple(pairs)


_SORT16 = _sort_network(PEER_TOPK)


def _exchange(v, i, j):
    v[i], v[j] = jnp.maximum(v[i], v[j]), jnp.minimum(v[i], v[j])


def _allreduce_rows(x, op):
    for shift in (4, 2, 1):
        x = op(x, pltpu.roll(x, shift, 0))
    return x


def _sorted_top16(s):
    v = [s[SUBLANES * j:SUBLANES * (j + 1), :] for j in range(PEER_TOPK)]
    for i, j in _SORT16:
        _exchange(v, i, j)
    for shift in (4, 2, 1):
        r = [pltpu.roll(x, shift, 0) for x in v]
        v = [jnp.maximum(v[i], r[PEER_TOPK - 1 - i]) for i in range(PEER_TOPK)]
        for stride in (8, 4, 2, 1):
            for i in range(PEER_TOPK):
                if i & stride == 0:
                    _exchange(v, i, i + stride)
    return v


def _rank_bits(x, v):
    b3 = v[7] > x
    b2 = jnp.where(b3, v[11], v[3]) > x
    b1 = jnp.where(b3, jnp.where(b2, v[13], v[9]), jnp.where(b2, v[5], v[1])) > x
    t = jnp.where(b3,
                  jnp.where(b2, jnp.where(b1, v[14], v[12]), jnp.where(b1, v[10], v[8])),
                  jnp.where(b2, jnp.where(b1, v[6], v[4]), jnp.where(b1, v[2], v[0])))
    return (b3, b2, b1, t > x), v[PEER_TOPK - 1] > x


def _select16(bits, vals):
    b3, b2, b1, b0 = bits
    lvl = [jnp.where(b0, vals[2 * i + 1], vals[2 * i]) for i in range(8)]
    lvl = [jnp.where(b1, lvl[2 * i + 1], lvl[2 * i]) for i in range(4)]
    lvl = [jnp.where(b2, lvl[2 * i + 1], lvl[2 * i]) for i in range(2)]
    return jnp.where(b3, lvl[1], lvl[0])


PAIR_ROW_FORM = 4


def _pair_counts(v1, v2):
    sub = lax.broadcasted_iota(jnp.int32, v1[0].shape, 0)
    ninf = -jnp.inf
    limit = lambda a: PEER_TOPK // (a + 1)

    def column(v, off):
        col = v[off]
        for b in range(1, SUBLANES):
            col = jnp.where(sub == b, v[off + b], col)
        return col

    def keep(c, off, lo, hi):
        lo, hi = max(lo, off), min(hi, off + SUBLANES)
        if lo >= hi:
            return None
        if (lo, hi) == (off, off + SUBLANES):
            return c
        if hi - lo == 1:
            return jnp.where(sub == lo - off, c, ninf)
        if lo == off:
            return jnp.where(sub < hi - off, c, ninf)
        assert hi == off + SUBLANES
        return jnp.where(sub >= lo - off, c, ninf)

    halves = (0, SUBLANES)
    v1c = [column(v1, off) for off in halves]
    v2c = [column(v2, off) for off in halves]
    rows = []
    for a in range(PAIR_ROW_FORM):
        parts = [keep(v1[a] + v2c[k], off, 0, limit(a)) for k, off in enumerate(halves)]
        rows.append([c for c in parts if c is not None])
    cols = []
    for b in range(limit(PAIR_ROW_FORM)):
        parts = [keep(v1c[k] + v2[b], off, PAIR_ROW_FORM, limit(b)) for k, off in enumerate(halves)]
        cols += [c for c in parts if c is not None]
    cur = [c for r in rows for c in r] + cols
    top = v1[0] + v2[0]
    z = jnp.zeros_like(top)
    for _ in range(PEER_TOPK):
        m = cur[0]
        for c in cur[1:]:
            m = jnp.maximum(m, c)
        m = _allreduce_rows(m, jnp.maximum)
        z = z + jnp.exp(m - top)
        cur = [jnp.where(c == m, ninf, c) for c in cur]
    tau = m
    count = lambda c: jnp.where(c >= tau, 1.0, 0.0)
    nsel = []
    for r in rows:
        f = count(r[0])
        for c in r[1:]:
            f = f + count(c)
        nsel.append(_allreduce_rows(f, jnp.add))
    for a in range(PAIR_ROW_FORM, PEER_TOPK):
        f = count(v1[a] + v2[0])
        for b in range(1, limit(a)):
            f = f + count(v1[a] + v2[b])
        nsel.append(f)
    total = nsel[0]
    for f in nsel[1:]:
        total = total + f
    return nsel, z, total != float(PEER_TOPK)


def _key_ranks(s, v, lookup=None):
    outs = []
    ranked = None
    for j in range(PEER_TOPK):
        x = s[SUBLANES * j:SUBLANES * (j + 1), :]
        bits, low = _rank_bits(x, v)
        if lookup is None:
            b3, b2, b1, b0 = bits
            val = (jnp.where(b3, 8.0, 0.0) + jnp.where(b2, 4.0, 0.0)
                   + jnp.where(b1, 2.0, 0.0) + jnp.where(b0, 1.0, 0.0) + jnp.where(low, 1.0, 0.0))
        else:
            val = jnp.where(low, 0.0, _select16(bits, lookup))
        outs.append(val)
        r = jnp.where(low, 0.0, 1.0)
        ranked = r if ranked is None else ranked + r
    tie = _allreduce_rows(ranked, jnp.add) != float(PEER_TOPK)
    for a in range(PEER_TOPK - 1):
        tie = tie | (v[a] == v[a + 1])
    return jnp.concatenate(outs, axis=0), tie


def _peer_select_kernel(x_ref, g_ref, wq_ref, k1_ref, k2_ref,
                        pos2_ref, m2_ref, nsel_ref, m1_ref, q_scr, s1_scr, s2_scr, *, tm):
    h = _rms(x_ref[...], g_ref[...]).astype(BF16)
    q = jnp.dot(h, wq_ref[...], preferred_element_type=F32)
    for j in range(2 * PEER_HEADS):
        q_scr[j] = q[:, j * PEER_KEYS:(j + 1) * PEER_KEYS].astype(BF16)
    k1 = k1_ref[...]
    k2 = k2_ref[...]

    def head(hh, carry):
        s1_scr[...] = lax.dot_general(k1, q_scr[2 * hh], _NT, preferred_element_type=F32)
        s2_scr[...] = lax.dot_general(k2, q_scr[2 * hh + 1], _NT, preferred_element_type=F32)
        tie = None
        for c in range(tm // LANES):
            sl = slice(c * LANES, (c + 1) * LANES)
            s1, s2 = s1_scr[:, sl], s2_scr[:, sl]
            v1 = _sorted_top16(s1)
            v2 = _sorted_top16(s2)
            nsel, z, t0 = _pair_counts(v1, v2)
            pos2, t2 = _key_ranks(s2, v2)
            nfull, t1 = _key_ranks(s1, v1, lookup=nsel)
            t = t0 | t1 | t2
            tie = t if tie is None else tie | t
            pos2_ref[hh, :, sl] = pos2.astype(BF16)
            m2_ref[hh, :, sl] = jnp.exp(s2 - v2[0][0:1, :]).astype(BF16)
            nsel_ref[hh, :, sl] = nfull
            m1_ref[hh, :, sl] = jnp.exp(s1 - v1[0][0:1, :]) * (1.0 / z[0:1, :])

        @pl.when(jnp.max(jnp.where(tie, 1.0, 0.0)) > 0.0)
        def _():
            for c in range(tm // LANES):
                sl = slice(c * LANES, (c + 1) * LANES)
                s1, s2 = s1_scr[:, sl], s2_scr[:, sl]
                vals1, pos1 = _top16(s1)
                vals2, pos2 = _top16(s2)
                nsel, z = _pair_select(vals1, vals2)
                nfull = jnp.zeros(s1.shape, jnp.int32)
                for a in range(PEER_TOPK):
                    nfull = jnp.where(pos1 == a, nsel[a:a + 1, :], nfull)
                pos2_ref[hh, :, sl] = pos2.astype(F32).astype(BF16)
                m2_ref[hh, :, sl] = jnp.exp(s2 - vals2[0]).astype(BF16)
                nsel_ref[hh, :, sl] = nfull.astype(F32)
                m1_ref[hh, :, sl] = jnp.exp(s1 - vals1[0]) * (1.0 / z)

        return carry

    lax.fori_loop(0, PEER_HEADS, head, 0)


PEER_SELECT_TILE = 256


def _peer_select(x, g, w_query, key1, key2):
    t, d = x.shape
    tm = PEER_SELECT_TILE
    nq = w_query.shape[1]
    head = pl.BlockSpec((PEER_HEADS, PEER_KEYS, tm), lambda i: (0, 0, i))
    keyspec = pl.BlockSpec((PEER_KEYS, PEER_KEYS), lambda i: (0, 0))
    shp = lambda dt: jax.ShapeDtypeStruct((PEER_HEADS, PEER_KEYS, t), dt)
    return pl.pallas_call(
        functools.partial(_peer_select_kernel, tm=tm),
        grid=(t // tm,),
        in_specs=[pl.BlockSpec((tm, d), lambda i: (i, 0)), pl.BlockSpec((1, d), lambda i: (0, 0)),
                  pl.BlockSpec((d, nq), lambda i: (0, 0)), keyspec, keyspec],
        out_specs=[head] * 4,
        out_shape=[shp(BF16), shp(BF16), shp(F32), shp(F32)],
        scratch_shapes=[pltpu.VMEM((2 * PEER_HEADS, tm, PEER_KEYS), BF16),
                        pltpu.VMEM((PEER_KEYS, tm), F32), pltpu.VMEM((PEER_KEYS, tm), F32)],
        compiler_params=_params("parallel"),
        name="peer_select",
    )(x, g.reshape(1, d), w_query, key1, key2)


def _zero_after(t):
    bits = pltpu.bitcast(t[0:2 * SUBLANES, :], jnp.uint32)
    return jnp.max(((bits >> 16) >> 16).astype(jnp.int32)).astype(F32).astype(BF16)


PEER_GATE_ROWS = 1


def _peer_dense_kernel(x_ref, g_ref, gf_ref, u_ref, vt_ref, pos2_ref, m2_ref, nsel_ref, m1_ref,
                       o_ref, h_scr, acc_scr, p2_scr, m2_scr, *, tm, te, final_norm):
    j = pl.program_id(1)

    @pl.when(j == 0)
    def _():
        h_scr[...] = _rms(x_ref[...], g_ref[...]).astype(BF16)
        acc_scr[...] = jnp.zeros(acc_scr.shape, F32)
        p2_scr[...] = pos2_ref[...]
        m2_scr[...] = m2_ref[...]

    n1 = te // PEER_KEYS
    first = pl.multiple_of(j * n1, n1)
    half = n1 // 2

    def activations(r0, r1, gate=None):
        hop = h_scr[...]
        if gate is not None:
            hop = hop + gate
        return lax.dot_general(u_ref[r0 * PEER_KEYS:r1 * PEER_KEYS, :], hop, _NT, preferred_element_type=F32)

    def weighted(a, base, r0, r1):
        blocks = []
        for r in range(r0, r1):
            tiles = []
            for ct in range(tm // LANES):
                sl = slice(ct * LANES, (ct + 1) * LANES)
                w = None
                for hh in range(PEER_HEADS):
                    nrow = nsel_ref[hh, pl.ds(first, n1), sl][r:r + 1, :].astype(BF16)
                    mrow = m1_ref[hh, pl.ds(first, n1), sl][r:r + 1, :].astype(BF16)
                    t = jnp.where(p2_scr[hh, :, sl] < nrow, m2_scr[hh, :, sl] * mrow, 0)
                    w = t if w is None else w + t
                at = a[(r - base) * PEER_KEYS:(r - base + 1) * PEER_KEYS, sl]
                tiles.append(_gelu(at).astype(BF16) * w)
            blocks.append(jnp.concatenate(tiles, axis=1))
        return blocks

    a_top = activations(0, half)
    z = weighted(a_top, 0, 0, PEER_GATE_ROWS)
    a_bot = activations(half, n1, gate=_zero_after(z[-1]))
    z = z + weighted(a_top, 0, PEER_GATE_ROWS, half) + weighted(a_bot, half, half, n1)
    acc_scr[...] += jnp.dot(vt_ref[...], jnp.concatenate(z, axis=0), preferred_element_type=F32)

    @pl.when(j == pl.num_programs(1) - 1)
    def _():
        out = x_ref[...] + acc_scr[...].T
        if final_norm:
            out = _rms(out, gf_ref[...])
        o_ref[...] = out


def _peer_dense(x, g, g_final, u, vt, sel, final_norm):
    t, d = x.shape
    tm = min(ROW_TILE, t)
    te = vt.shape[2]
    assert (te // PEER_KEYS) % SUBLANES == 0
    row = pl.BlockSpec((tm, d), lambda i, j: (i, 0))
    vec = pl.BlockSpec((1, d), lambda i, j: (0, 0))
    head = pl.BlockSpec((PEER_HEADS, PEER_KEYS, tm), lambda i, j: (0, 0, i))
    return pl.pallas_call(
        functools.partial(_peer_dense_kernel, tm=tm, te=te, final_norm=final_norm),
        grid=(t // tm, PEER_EXPERTS // te),
        in_specs=[row, vec, vec, pl.BlockSpec((te, d), lambda i, j: (j, 0)),
                  pl.BlockSpec((None, d, te), lambda i, j: (j, 0, 0)), head, head, head, head],
        out_specs=row,
        out_shape=jax.ShapeDtypeStruct((t, d), F32),
        scratch_shapes=[pltpu.VMEM((tm, d), BF16), pltpu.VMEM((d, tm), F32),
                        pltpu.VMEM((PEER_HEADS, PEER_KEYS, tm), BF16),
                        pltpu.VMEM((PEER_HEADS, PEER_KEYS, tm), BF16)],
        compiler_params=_params("parallel", "arbitrary"),
        name="peer_dense",
    )(x, g.reshape(1, d), g_final.reshape(1, d), u, vt, *sel)


def _peer(x, g, g_final, w_query, key1, key2, u, vt, final_norm):
    shp = x.shape
    x2 = x.reshape(-1, shp[-1])
    sel = _peer_select(x2, g, w_query, key1, key2)
    return _peer_dense(x2, g, g_final, u, vt, sel, final_norm).reshape(shp)


def _trunk(x, ssm0, conv0, mem_k, mem_v, w, sample):
    b, s, d = x.shape
    chunk = s if sample else S5_CHUNK
    if ssm0 is None:
        ssm0 = (jnp.zeros((b, SSM_GROUPS, SSM_STATE), F32),) * 2
    x, ssm_re, ssm_im = _s5_mixer(x, ssm0[0], ssm0[1], w['norm_mix'][0], w['s5_fold'][chunk],
                                  w['ssm_d'][0], w['ssm_w_glu'][0], chunk)
    attn = _attn_sample if sample else _attn_prompt
    for i in range(2):
        if i == 1:
            if sample:
                x_tm = x.transpose(1, 0, 2).reshape(s * b, d)
                buf_tm = conv0.transpose(1, 0, 2).reshape(2 * b, d)
                x_tm, nbuf = _conv_sample(x_tm, buf_tm, w['norm_mix'][1], w['conv_w_in'][0],
                                          w['conv_w'][0], w['conv_w_out'][0], s, b)
                x = x_tm.reshape(s, b, d).transpose(1, 0, 2)
                conv_out = nbuf.reshape(2, b, d).transpose(1, 0, 2)
            else:
                x, tail = _conv_prompt(x, w['norm_mix'][1], w['conv_w_in'][0], w['conv_w'][0],
                                       w['conv_w_out'][0])
                conv_out = tail[:, CONV_HEAD - 2:]
        x = attn(x, w['norm_mem'][i], mem_k, mem_v, i, w['mem_w_q'][i], w['mem_w_o'][i])
        x = _peer(x, w['norm_ffn'][i], w['norm_final'], w['peer_w_query'][i], w['peer_key1'][i],
                  w['peer_key2'][i], w['peer_u'][i], w['peer_vt'][i], final_norm=(i == 1))
    return x, ssm_re[None], ssm_im[None], conv_out[None]


def kernel(x_prompt, x_sample, mem_prompt, state_ssm_re, state_ssm_im, state_conv, cache_mem_k, cache_mem_v, norm_mix, norm_mem, norm_ffn, norm_final, ssm_a_re, ssm_a_im, ssm_log_dt, ssm_b_re, ssm_b_im, ssm_c_re, ssm_c_im, ssm_d, ssm_w_glu, conv_w_in, conv_w, conv_w_out, mem_w_q, mem_w_k, mem_w_v, mem_w_o, peer_w_query, peer_key1, peer_key2, peer_u, peer_v):
    bsz, seq, d = x_prompt.shape
    dec_b, dec_s, _ = x_sample.shape
    depth = mem_w_q.shape[0]
    fold = lambda chunk: _s5_fold(ssm_a_re[0], ssm_a_im[0], ssm_log_dt[0], ssm_b_re[0], ssm_b_im[0],
                                  ssm_c_re[0], ssm_c_im[0], chunk)
    w = dict(
        norm_mix=norm_mix, norm_mem=norm_mem, norm_ffn=norm_ffn, norm_final=norm_final,
        s5_fold={S5_CHUNK: fold(S5_CHUNK), dec_s: fold(dec_s)},
        ssm_d=ssm_d, ssm_w_glu=ssm_w_glu.astype(BF16),
        conv_w_in=conv_w_in.astype(BF16), conv_w=conv_w, conv_w_out=conv_w_out.astype(BF16),
        mem_w_q=mem_w_q.astype(BF16), mem_w_o=mem_w_o.astype(BF16),
        peer_w_query=peer_w_query.astype(BF16), peer_key1=peer_key1.astype(BF16),
        peer_key2=peer_key2.astype(BF16), peer_u=peer_u.astype(BF16),
        peer_vt=peer_v.astype(BF16).reshape(depth, PEER_EXPERTS // EXPERT_TILE, EXPERT_TILE, d)
        .transpose(0, 1, 3, 2),
    )
    w_kv = jnp.concatenate([mem_w_k, mem_w_v], axis=0).astype(BF16)
    kv = _mm_stacked(mem_prompt.reshape(bsz * MEM_TOKENS, d), w_kv).reshape(2 * depth, bsz, MEM_TOKENS, d)
    mem_k_p, mem_v_p = kv[:depth], kv[depth:]

    y_p, re_p, im_p, conv_p = _trunk(x_prompt, None, None, mem_k_p, mem_v_p, w, sample=False)
    y_s, re_s, im_s, conv_s = _trunk(x_sample, (state_ssm_re[0], state_ssm_im[0]), state_conv[0],
                                     cache_mem_k, cache_mem_v, w, sample=True)
    head_shape = (depth, bsz, MEM_TOKENS, MEM_HEADS, MEM_HEAD_DIM)
    return (y_p, y_s, re_p, im_p, conv_p, mem_k_p.reshape(head_shape), mem_v_p.reshape(head_shape),
            re_s, im_s, conv_s)
```

```python
import functools
import math

import jax
import jax.numpy as jnp
from jax import lax
from jax.experimental import pallas as pl
from jax.experimental.pallas import tpu as pltpu

F32 = jnp.float32
BF16 = jnp.bfloat16

D_MODEL = 1024
SSM_GROUP = 16
SSM_GROUPS = D_MODEL // SSM_GROUP
SSM_STATE = 64
SSM_BLOCK_GROUPS = 8
SSM_BLOCKS = SSM_GROUPS // SSM_BLOCK_GROUPS
MEM_TOKENS = 256
MEM_HEADS = 4
MEM_HEAD_DIM = D_MODEL // MEM_HEADS
PEER_HEADS = 8
PEER_KEYS = 128
PEER_EXPERTS = PEER_KEYS * PEER_KEYS
PEER_TOPK = 16
RMS_EPS = 1e-6

LANES = 128
SUBLANES = 8
ROW_TILE = 512
S5_CHUNK = 8
S5_COL_TILE = 512
EXPERT_TILE = 2048
VMEM_LIMIT = 48 * 1024 * 1024

_NT = (((1,), (1,)), ((), ()))


def _params(*sem):
    return pltpu.CompilerParams(dimension_semantics=sem, vmem_limit_bytes=VMEM_LIMIT)


def _rms(x, g):
    r = lax.rsqrt(jnp.mean(x * x, axis=-1, keepdims=True) + RMS_EPS)
    return (x * r) * g


def _gelu(x):
    return 0.5 * x * (1.0 + lax.erf(x * (1.0 / math.sqrt(2.0))))


def _sigmoid(x):
    return 1.0 / (1.0 + jnp.exp(-x))


def _norm_cast_kernel(x_ref, g_ref, o_ref):
    o_ref[...] = _rms(x_ref[...], g_ref[...]).astype(BF16)


def _norm_cast(x, g):
    t, d = x.shape
    tm = min(ROW_TILE, t)
    return pl.pallas_call(
        _norm_cast_kernel,
        grid=(t // tm,),
        in_specs=[pl.BlockSpec((tm, d), lambda i: (i, 0)), pl.BlockSpec((1, d), lambda i: (0, 0))],
        out_specs=pl.BlockSpec((tm, d), lambda i: (i, 0)),
        out_shape=jax.ShapeDtypeStruct((t, d), BF16),
        compiler_params=_params("parallel"),
        name="norm_cast",
    )(x, g.reshape(1, d))


def _mm_kernel(*refs, has_norm, has_res):
    x_ref, w_ref = refs[0], refs[1]
    pos = 2
    x = x_ref[...]
    if has_norm:
        x = _rms(x, refs[pos][...])
        pos += 1
    y = jnp.dot(x.astype(BF16), w_ref[...], preferred_element_type=F32)
    if has_res:
        y = y + refs[pos][...]
        pos += 1
    refs[pos][...] = y


def _mm(x, w, g=None, res=None, tm=ROW_TILE):
    t, k = x.shape
    n = w.shape[1]
    tm = min(tm, t)
    args = [x, w]
    specs = [pl.BlockSpec((tm, k), lambda i: (i, 0)), pl.BlockSpec((k, n), lambda i: (0, 0))]
    if g is not None:
        args.append(g.reshape(1, k))
        specs.append(pl.BlockSpec((1, k), lambda i: (0, 0)))
    if res is not None:
        args.append(res)
        specs.append(pl.BlockSpec((tm, n), lambda i: (i, 0)))
    return pl.pallas_call(
        functools.partial(_mm_kernel, has_norm=g is not None, has_res=res is not None),
        grid=(t // tm,),
        in_specs=specs,
        out_specs=pl.BlockSpec((tm, n), lambda i: (i, 0)),
        out_shape=jax.ShapeDtypeStruct((t, n), F32),
        compiler_params=_params("parallel"),
        name="mm_rows",
    )(*args)


def _mm_stacked_kernel(x_ref, w_ref, o_ref):
    o_ref[...] = jnp.dot(x_ref[...].astype(BF16), w_ref[...], preferred_element_type=F32)


def _mm_stacked(x, w, tm=ROW_TILE):
    t, k = x.shape
    n, _, c = w.shape
    tm = min(tm, t)
    return pl.pallas_call(
        _mm_stacked_kernel,
        grid=(n, t // tm),
        in_specs=[pl.BlockSpec((tm, k), lambda j, i: (i, 0)), pl.BlockSpec((None, k, c), lambda j, i: (j, 0, 0))],
        out_specs=pl.BlockSpec((None, tm, c), lambda j, i: (j, i, 0)),
        out_shape=jax.ShapeDtypeStruct((n, t, c), F32),
        compiler_params=_params("parallel", "parallel"),
        name="mm_stacked",
    )(x, w)


def _s5_fold(a_re, a_im, log_dt, b_re, b_im, c_re, c_im, chunk):
    hi = lax.Precision.HIGHEST
    dt = jnp.exp(log_dt)[:, None]
    mag = jnp.exp(a_re * dt)
    ang = a_im * dt
    lb_re = mag * jnp.cos(ang)
    lb_im = mag * jnp.sin(ang)
    den = a_re * a_re + a_im * a_im
    f_re = ((lb_re - 1.0) * a_re + lb_im * a_im) / den
    f_im = (lb_im * a_re - (lb_re - 1.0) * a_im) / den
    bb_re = f_re[..., None] * b_re - f_im[..., None] * b_im
    bb_im = f_re[..., None] * b_im + f_im[..., None] * b_re
    pw_re, pw_im = [jnp.ones_like(lb_re)], [jnp.zeros_like(lb_im)]
    for _ in range(chunk):
        pr, pi = pw_re[-1], pw_im[-1]
        pw_re.append(pr * lb_re - pi * lb_im)
        pw_im.append(pr * lb_im + pi * lb_re)
    pw_re, pw_im = jnp.stack(pw_re), jnp.stack(pw_im)
    cl_re = c_re[None] * pw_re[:, :, None, :] - c_im[None] * pw_im[:, :, None, :]
    cl_im = c_re[None] * pw_im[:, :, None, :] + c_im[None] * pw_re[:, :, None, :]
    kern = (jnp.einsum('kgdp,gpc->kgdc', cl_re[:chunk], bb_re, precision=hi)
            - jnp.einsum('kgdp,gpc->kgdc', cl_im[:chunk], bb_im, precision=hi))
    step = jnp.arange(chunk)
    rp_re, rp_im = pw_re[chunk - 1 - step], pw_im[chunk - 1 - step]
    p_re = rp_re[..., None] * bb_re[None] - rp_im[..., None] * bb_im[None]
    p_im = rp_re[..., None] * bb_im[None] + rp_im[..., None] * bb_re[None]
    p_re = p_re.transpose(1, 0, 3, 2)
    p_im = p_im.transpose(1, 0, 3, 2)
    n_re = cl_re[1:].transpose(1, 3, 0, 2)
    n_im = (-cl_im[1:]).transpose(1, 3, 0, 2)

    nb, q = SSM_BLOCKS, SSM_BLOCK_GROUPS
    eye = jnp.eye(q, dtype=BF16)
    w = chunk * q * SSM_GROUP
    sl = q * SSM_STATE
    kq = kern.transpose(1, 3, 0, 2).astype(BF16).reshape(nb, q, SSM_GROUP, chunk, 1, SSM_GROUP)
    kq = (kq * eye[None, :, None, None, :, None]).reshape(nb, q, SSM_GROUP, chunk, q * SSM_GROUP)
    kq = jnp.pad(kq, ((0, 0), (0, 0), (0, 0), (chunk - 1, 0), (0, 0)))
    m = jnp.stack([kq[:, :, :, chunk - 1 - t:2 * chunk - 1 - t] for t in range(chunk)], axis=1)
    m = m.reshape(nb, w, w)

    def fold_p(z):
        z = z.astype(BF16).reshape(nb, q, chunk, SSM_GROUP, 1, SSM_STATE).transpose(0, 2, 1, 3, 4, 5)
        return (z * eye[None, None, :, None, :, None]).reshape(nb, w, sl)

    def fold_n(z):
        z = z.astype(BF16).reshape(nb, q, SSM_STATE, chunk, 1, SSM_GROUP)
        return (z * eye[None, :, None, None, :, None]).reshape(nb, sl, w)

    lam_re = pw_re[chunk].reshape(nb, 1, sl)
    lam_im = pw_im[chunk].reshape(nb, 1, sl)
    return m, fold_p(p_re), fold_p(p_im), fold_n(n_re), fold_n(n_im), lam_re, lam_im


def _s5_core_kernel(x_ref, m_ref, pre_ref, pim_ref, nre_ref, nim_ref, lre_ref, lim_ref,
                    s0re_ref, s0im_ref, y_ref, sre_ref, sim_ref, stre, stim, *, nchunks, rows):
    x = x_ref[0]

    @pl.when(pl.program_id(1) == 0)
    def _():
        stre[...] = jnp.dot(x, pre_ref[0], preferred_element_type=F32)
        stim[...] = jnp.dot(x, pim_ref[0], preferred_element_type=F32)
        width = stre.shape[1]
        lr = jnp.broadcast_to(lre_ref[0], (rows, width))
        li = jnp.broadcast_to(lim_ref[0], (rows, width))

        def body(k, carry):
            sr, si = carry
            off = pl.multiple_of(k * rows, rows)
            qr = stre[pl.ds(off, rows), :]
            qi = stim[pl.ds(off, rows), :]
            stre[pl.ds(off, rows), :] = sr
            stim[pl.ds(off, rows), :] = si
            return lr * sr - li * si + qr, lr * si + li * sr + qi

        sr, si = lax.fori_loop(0, nchunks, body, (s0re_ref[0], s0im_ref[0]))
        sre_ref[0] = sr
        sim_ref[0] = si

    y = jnp.dot(x, m_ref[0], preferred_element_type=F32)
    y = y + jnp.dot(stre[...].astype(BF16), nre_ref[0], preferred_element_type=F32)
    y = y + jnp.dot(stim[...].astype(BF16), nim_ref[0], preferred_element_type=F32)
    y_ref[0] = y


def _s5_core(xb, folded, s0_re, s0_im, nchunks, rows):
    m, p_re, p_im, n_re, n_im, lam_re, lam_im = folded
    nblk, nr, w = xb.shape
    sl = SSM_BLOCK_GROUPS * SSM_STATE
    wc = min(w, S5_COL_TILE)
    fixed = lambda a, b: pl.BlockSpec((1, a, b), lambda i, j: (i, 0, 0))
    cols = lambda a: pl.BlockSpec((1, a, wc), lambda i, j: (i, 0, j))
    return pl.pallas_call(
        functools.partial(_s5_core_kernel, nchunks=nchunks, rows=rows),
        grid=(nblk, w // wc),
        in_specs=[fixed(nr, w), cols(w), fixed(w, sl), fixed(w, sl), cols(sl), cols(sl),
                  fixed(1, sl), fixed(1, sl), fixed(rows, sl), fixed(rows, sl)],
        out_specs=[cols(nr), fixed(rows, sl), fixed(rows, sl)],
        out_shape=[jax.ShapeDtypeStruct((nblk, nr, w), F32),
                   jax.ShapeDtypeStruct((nblk, rows, sl), F32),
                   jax.ShapeDtypeStruct((nblk, rows, sl), F32)],
        scratch_shapes=[pltpu.VMEM((nr, sl), F32)] * 2,
        compiler_params=_params("parallel", "arbitrary"),
        name="s5_core",
    )(xb, m, p_re, p_im, n_re, n_im, lam_re, lam_im, s0_re, s0_im)


def _s5_out_kernel(x_ref, y_ref, g_ref, d_ref, w_ref, o_ref):
    x = x_ref[...]
    h = _rms(x, g_ref[...])
    z = _gelu(y_ref[...] + d_ref[...] * h)
    gg = jnp.dot(z.astype(BF16), w_ref[...], preferred_element_type=F32)
    o_ref[...] = x + gg[:, :D_MODEL] * _sigmoid(gg[:, D_MODEL:])


def _s5_out(x, y, g, d, w_glu):
    t = x.shape[0]
    tm = min(ROW_TILE, t)
    row = pl.BlockSpec((tm, D_MODEL), lambda i: (i, 0))
    vec = pl.BlockSpec((1, D_MODEL), lambda i: (0, 0))
    return pl.pallas_call(
        _s5_out_kernel,
        grid=(t // tm,),
        in_specs=[row, row, vec, vec, pl.BlockSpec((D_MODEL, 2 * D_MODEL), lambda i: (0, 0))],
        out_specs=row,
        out_shape=jax.ShapeDtypeStruct((t, D_MODEL), F32),
        compiler_params=_params("parallel"),
        name="s5_out",
    )(x, y, g.reshape(1, -1), d.reshape(1, -1), w_glu)


def _s5_mixer(x, s0_re, s0_im, g, folded, d_skip, w_glu, chunk):
    b, s, d = x.shape
    nk = s // chunk
    x2 = x.reshape(b * s, d)
    h = _norm_cast(x2, g)
    xb = (h.reshape(b, nk, chunk, SSM_BLOCKS, LANES).transpose(3, 1, 0, 2, 4)
          .reshape(SSM_BLOCKS, nk * b, chunk * LANES))
    sl = SSM_BLOCK_GROUPS * SSM_STATE
    to_block = lambda z: z.reshape(b, SSM_BLOCKS, sl).transpose(1, 0, 2)
    y, sr, si = _s5_core(xb, folded, to_block(s0_re), to_block(s0_im), nk, b)
    y = y.reshape(SSM_BLOCKS, nk, b, chunk, LANES).transpose(2, 1, 3, 0, 4).reshape(b * s, d)
    from_block = lambda z: z.transpose(1, 0, 2).reshape(b, SSM_GROUPS, SSM_STATE)
    out = _s5_out(x2, y, g, d_skip, w_glu)
    return out.reshape(b, s, d), from_block(sr), from_block(si)


CONV_HEAD = 8


def _conv_prompt_kernel(x_ref, g_ref, win_ref, wc_ref, wout_ref, o_ref, tail_ref, vbuf, *, tm):
    @pl.when(pl.program_id(1) == 0)
    def _():
        vbuf[0:CONV_HEAD, :] = jnp.zeros((CONV_HEAD, D_MODEL), F32)

    x = x_ref[0]
    h = _rms(x, g_ref[...])
    p = jnp.dot(h.astype(BF16), win_ref[...], preferred_element_type=F32)
    bg = p[:, :D_MODEL]
    v = p[:, D_MODEL:2 * D_MODEL] * p[:, 2 * D_MODEL:]
    vbuf[CONV_HEAD:CONV_HEAD + tm, :] = v
    v2 = vbuf[CONV_HEAD - 2:CONV_HEAD - 2 + tm, :]
    v1 = vbuf[CONV_HEAD - 1:CONV_HEAD - 1 + tm, :]
    conv = wc_ref[0:1, :] * v2 + wc_ref[1:2, :] * v1 + wc_ref[2:3, :] * v
    out = jnp.dot((bg * conv).astype(BF16), wout_ref[...], preferred_element_type=F32)
    o_ref[0] = x + out
    tail = vbuf[tm:tm + CONV_HEAD, :]
    vbuf[0:CONV_HEAD, :] = tail
    tail_ref[0] = tail


def _conv_prompt(x, g, w_in, w_conv, w_out):
    b, s, d = x.shape
    tm = min(ROW_TILE, s)
    return pl.pallas_call(
        functools.partial(_conv_prompt_kernel, tm=tm),
        grid=(b, s // tm),
        in_specs=[pl.BlockSpec((1, tm, d), lambda i, j: (i, j, 0)),
                  pl.BlockSpec((1, d), lambda i, j: (0, 0)),
                  pl.BlockSpec((d, 3 * d), lambda i, j: (0, 0)),
                  pl.BlockSpec((3, d), lambda i, j: (0, 0)),
                  pl.BlockSpec((d, d), lambda i, j: (0, 0))],
        out_specs=[pl.BlockSpec((1, tm, d), lambda i, j: (i, j, 0)),
                   pl.BlockSpec((1, CONV_HEAD, d), lambda i, j: (i, 0, 0))],
        out_shape=[jax.ShapeDtypeStruct((b, s, d), F32),
                   jax.ShapeDtypeStruct((b, CONV_HEAD, d), F32)],
        scratch_shapes=[pltpu.VMEM((CONV_HEAD + tm, d), F32)],
        compiler_params=_params("parallel", "arbitrary"),
        name="conv_prompt",
    )(x, g.reshape(1, d), w_in, w_conv, w_out)


def _conv_sample_kernel(x_ref, buf_ref, g_ref, win_ref, wc_ref, wout_ref, o_ref, nbuf_ref, *, steps, nb):
    x = x_ref[...]
    h = _rms(x, g_ref[...])
    p = jnp.dot(h.astype(BF16), win_ref[...], preferred_element_type=F32)
    bg = p[:, :D_MODEL]
    v = p[:, D_MODEL:2 * D_MODEL] * p[:, 2 * D_MODEL:]
    vp = [buf_ref[0:nb, :], buf_ref[nb:2 * nb, :]] + [v[t * nb:(t + 1) * nb, :] for t in range(steps)]
    conv = jnp.concatenate(
        [wc_ref[0:1, :] * vp[t] + wc_ref[1:2, :] * vp[t + 1] + wc_ref[2:3, :] * vp[t + 2]
         for t in range(steps)], axis=0)
    out = jnp.dot((bg * conv).astype(BF16), wout_ref[...], preferred_element_type=F32)
    o_ref[...] = x + out
    nbuf_ref[0:nb, :] = vp[-2]
    nbuf_ref[nb:2 * nb, :] = vp[-1]


def _conv_sample(x_tm, buf_tm, g, w_in, w_conv, w_out, steps, nb):
    d = D_MODEL
    full = lambda r, c: pl.BlockSpec((r, c), lambda i: (0, 0))
    return pl.pallas_call(
        functools.partial(_conv_sample_kernel, steps=steps, nb=nb),
        grid=(1,),
        in_specs=[full(steps * nb, d), full(2 * nb, d), full(1, d), full(d, 3 * d), full(3, d), full(d, d)],
        out_specs=[full(steps * nb, d), full(2 * nb, d)],
        out_shape=[jax.ShapeDtypeStruct((steps * nb, d), F32), jax.ShapeDtypeStruct((2 * nb, d), F32)],
        compiler_params=_params("arbitrary"),
        name="conv_sample",
    )(x_tm, buf_tm, g.reshape(1, d), w_in, w_conv, w_out)


def _softmax_rows(s):
    e = jnp.exp(s - jnp.max(s, axis=-1, keepdims=True))
    return e / jnp.sum(e, axis=-1, keepdims=True)


def _attn_heads(q, k, v):
    k = k.astype(BF16)
    v = v.astype(BF16)
    outs = []
    for hd in range(MEM_HEADS):
        sl = slice(hd * MEM_HEAD_DIM, (hd + 1) * MEM_HEAD_DIM)
        s = lax.dot_general(q[:, sl].astype(BF16), k[:, sl], _NT, preferred_element_type=F32)
        p = _softmax_rows(s * (MEM_HEAD_DIM ** -0.5))
        outs.append(jnp.dot(p.astype(BF16), v[:, sl], preferred_element_type=F32))
    return jnp.concatenate(outs, axis=-1)


def _attn_prompt_kernel(x_ref, g_ref, k_ref, v_ref, wq_ref, wo_ref, o_ref):
    x = x_ref[0]
    h = _rms(x, g_ref[...])
    q = jnp.dot(h.astype(BF16), wq_ref[...], preferred_element_type=F32)
    o = _attn_heads(q, k_ref[0], v_ref[0])
    o_ref[0] = x + jnp.dot(o.astype(BF16), wo_ref[...], preferred_element_type=F32)


def _attn_prompt(x, g, mem_k, mem_v, layer, w_q, w_o):
    b, s, d = x.shape
    tm = min(ROW_TILE, s)
    mem = pl.BlockSpec((None, 1, MEM_TOKENS, d), lambda i, j: (layer, i, 0, 0))
    wgt = pl.BlockSpec((d, d), lambda i, j: (0, 0))
    row = pl.BlockSpec((1, tm, d), lambda i, j: (i, j, 0))
    return pl.pallas_call(
        _attn_prompt_kernel,
        grid=(b, s // tm),
        in_specs=[row, pl.BlockSpec((1, d), lambda i, j: (0, 0)), mem, mem, wgt, wgt],
        out_specs=row,
        out_shape=jax.ShapeDtypeStruct((b, s, d), F32),
        compiler_params=_params("parallel", "parallel"),
        name="attn_prompt",
    )(x, g.reshape(1, d), mem_k, mem_v, w_q, w_o)


SAMPLE_Q_ROWS = 8
SAMPLE_SEQ_BLOCK = 4


def _attn_sample_kernel(q_ref, k_ref, v_ref, o_ref):
    nrow = SAMPLE_Q_ROWS * MEM_HEADS
    ncol = MEM_TOKENS * MEM_HEADS
    rows = lax.broadcasted_iota(jnp.int32, (nrow, ncol), 0)
    cols = lax.broadcasted_iota(jnp.int32, (nrow, ncol), 1)
    same_head = (rows % MEM_HEADS) == (cols % MEM_HEADS)

    def body(i, carry):
        k = k_ref[i].reshape(ncol, MEM_HEAD_DIM).astype(BF16)
        v = v_ref[i].reshape(ncol, MEM_HEAD_DIM).astype(BF16)
        s = lax.dot_general(q_ref[i].astype(BF16), k, _NT, preferred_element_type=F32)
        p = _softmax_rows(jnp.where(same_head, s * (MEM_HEAD_DIM ** -0.5), -jnp.inf))
        o_ref[i] = jnp.dot(p.astype(BF16), v, preferred_element_type=F32)
        return carry

    lax.fori_loop(0, SAMPLE_SEQ_BLOCK, body, 0)


def _attn_sample(x, g, cache_k, cache_v, layer, w_q, w_o):
    b, s, d = x.shape
    x2 = x.reshape(b * s, d)
    q = _mm(x2, w_q, g=g).reshape(b, s, MEM_HEADS, MEM_HEAD_DIM)
    q = jnp.pad(q, ((0, 0), (0, SAMPLE_Q_ROWS - s), (0, 0), (0, 0)))
    q = q.reshape(b, SAMPLE_Q_ROWS * MEM_HEADS, MEM_HEAD_DIM)
    qblk = pl.BlockSpec((SAMPLE_SEQ_BLOCK, SAMPLE_Q_ROWS * MEM_HEADS, MEM_HEAD_DIM), lambda i: (i, 0, 0))
    mblk = pl.BlockSpec((None, SAMPLE_SEQ_BLOCK, MEM_TOKENS, MEM_HEADS, MEM_HEAD_DIM),
                        lambda i: (layer, i, 0, 0, 0))
    o = pl.pallas_call(
        _attn_sample_kernel,
        grid=(b // SAMPLE_SEQ_BLOCK,),
        in_specs=[qblk, mblk, mblk],
        out_specs=qblk,
        out_shape=jax.ShapeDtypeStruct((b, SAMPLE_Q_ROWS * MEM_HEADS, MEM_HEAD_DIM), F32),
        compiler_params=_params("parallel"),
        name="attn_sample",
    )(q, cache_k, cache_v)
    o = o.reshape(b, SAMPLE_Q_ROWS, d)[:, :s].reshape(b * s, d)
    return _mm(o, w_o, res=x2).reshape(b, s, d)


def _top16(s):
    iota = lax.broadcasted_iota(jnp.int32, s.shape, 0)
    pos = jnp.full(s.shape, PEER_TOPK, jnp.int32)
    vals = []
    for i in range(PEER_TOPK):
        m = jnp.max(s, axis=0, keepdims=True)
        idx = jnp.min(jnp.where(s == m, iota, s.shape[0]), axis=0, keepdims=True)
        hit = iota == idx
        pos = jnp.where(hit, i, pos)
        s = jnp.where(hit, -jnp.inf, s)
        vals.append(m)
    return vals, pos


def _pair_select(vals1, vals2):
    lanes = vals1[0].shape[1]
    iota16 = lax.broadcasted_iota(jnp.int32, (PEER_TOPK, lanes), 0)
    v2 = jnp.zeros((PEER_TOPK, lanes), F32)
    for b in range(PEER_TOPK):
        v2 = jnp.where(iota16 == b, vals2[b], v2)
    iota8 = iota16[:SUBLANES]
    cand = [vals1[0] + v2]
    flat = [iota16]
    for a in range(1, PEER_TOPK):
        cand.append(jnp.where(iota8 < PEER_TOPK // (a + 1), vals1[a] + v2[:SUBLANES], -jnp.inf))
        flat.append(iota8 + a * PEER_TOPK)
    nsel = jnp.zeros((PEER_TOPK, lanes), jnp.int32)
    z = jnp.zeros((1, lanes), F32)
    top = vals1[0] + vals2[0]
    nflat = PEER_TOPK * PEER_TOPK
    for _ in range(PEER_TOPK):
        mx = jnp.maximum(cand[0][:SUBLANES], cand[0][SUBLANES:])
        for a in range(1, PEER_TOPK):
            mx = jnp.maximum(mx, cand[a])
        m = jnp.max(mx, axis=0, keepdims=True)
        w0 = jnp.where(cand[0] == m, flat[0], nflat)
        mi = jnp.minimum(w0[:SUBLANES], w0[SUBLANES:])
        for a in range(1, PEER_TOPK):
            mi = jnp.minimum(mi, jnp.where(cand[a] == m, flat[a], nflat))
        idx = jnp.min(mi, axis=0, keepdims=True)
        cand = [jnp.where(flat[a] == idx, -jnp.inf, cand[a]) for a in range(PEER_TOPK)]
        nsel = nsel + (iota16 == (idx >> (PEER_TOPK.bit_length() - 1))).astype(jnp.int32)
        z = z + jnp.exp(m - top)
    return nsel, z


def _sort_network(n):
    pairs = []

    def merge(lo, cnt, r):
        m = r * 2
        if m < cnt:
            merge(lo, cnt, m)
            merge(lo + r, cnt, m)
            for i in range(lo + r, lo + cnt - r, m):
                pairs.append((i, i + r))
        else:
            pairs.append((lo, lo + r))

    def sort(lo, cnt):
        if cnt > 1:
            m = cnt // 2
            sort(lo, m)
            sort(lo + m, m)
            merge(lo, cnt, 1)

    sort(0, n)
    return tuple(pairs)


_SORT16 = _sort_network(PEER_TOPK)
_HALVINGS = tuple(PEER_TOPK >> k for k in range(1, PEER_TOPK.bit_length()))
assert _HALVINGS[0] == SUBLANES


def _exchange(v, i, j):
    v[i], v[j] = jnp.maximum(v[i], v[j]), jnp.minimum(v[i], v[j])


def _allreduce_rows(x, op):
    for shift in _HALVINGS[1:]:
        x = op(x, pltpu.roll(x, shift, 0))
    return x


def _sorted_top16(s):
    v = [s[SUBLANES * j:SUBLANES * (j + 1), :] for j in range(PEER_TOPK)]
    for i, j in _SORT16:
        _exchange(v, i, j)
    for shift in _HALVINGS[1:]:
        r = [pltpu.roll(x, shift, 0) for x in v]
        v = [jnp.maximum(v[i], r[PEER_TOPK - 1 - i]) for i in range(PEER_TOPK)]
        for stride in _HALVINGS:
            for i in range(PEER_TOPK):
                if i & stride == 0:
                    _exchange(v, i, i + stride)
    return v


def _rank_bits(x, v):
    b3 = v[7] > x
    b2 = jnp.where(b3, v[11], v[3]) > x
    b1 = jnp.where(b3, jnp.where(b2, v[13], v[9]), jnp.where(b2, v[5], v[1])) > x
    t = jnp.where(b3,
                  jnp.where(b2, jnp.where(b1, v[14], v[12]), jnp.where(b1, v[10], v[8])),
                  jnp.where(b2, jnp.where(b1, v[6], v[4]), jnp.where(b1, v[2], v[0])))
    return (b3, b2, b1, t > x), v[PEER_TOPK - 1] > x


def _select16(bits, vals):
    b3, b2, b1, b0 = bits
    lvl = [jnp.where(b0, vals[2 * i + 1], vals[2 * i]) for i in range(8)]
    lvl = [jnp.where(b1, lvl[2 * i + 1], lvl[2 * i]) for i in range(4)]
    lvl = [jnp.where(b2, lvl[2 * i + 1], lvl[2 * i]) for i in range(2)]
    return jnp.where(b3, lvl[1], lvl[0])


PAIR_ROW_FORM = 4


def _pair_counts(v1, v2):
    sub = lax.broadcasted_iota(jnp.int32, v1[0].shape, 0)
    ninf = -jnp.inf
    limit = lambda a: PEER_TOPK // (a + 1)

    def column(v, off):
        col = v[off]
        for b in range(1, SUBLANES):
            col = jnp.where(sub == b, v[off + b], col)
        return col

    def keep(c, off, lo, hi):
        lo, hi = max(lo, off), min(hi, off + SUBLANES)
        if lo >= hi:
            return None
        if (lo, hi) == (off, off + SUBLANES):
            return c
        if hi - lo == 1:
            return jnp.where(sub == lo - off, c, ninf)
        if lo == off:
            return jnp.where(sub < hi - off, c, ninf)
        assert hi == off + SUBLANES
        return jnp.where(sub >= lo - off, c, ninf)

    halves = (0, SUBLANES)
    v1c = [column(v1, off) for off in halves]
    v2c = [column(v2, off) for off in halves]
    rows = []
    for a in range(PAIR_ROW_FORM):
        parts = [keep(v1[a] + v2c[k], off, 0, limit(a)) for k, off in enumerate(halves)]
        rows.append([c for c in parts if c is not None])
    cols = []
    for b in range(limit(PAIR_ROW_FORM)):
        parts = [keep(v1c[k] + v2[b], off, PAIR_ROW_FORM, limit(b)) for k, off in enumerate(halves)]
        cols += [c for c in parts if c is not None]
    cur = [c for r in rows for c in r] + cols
    top = v1[0] + v2[0]
    z = jnp.zeros_like(top)
    for _ in range(PEER_TOPK):
        m = cur[0]
        for c in cur[1:]:
            m = jnp.maximum(m, c)
        m = _allreduce_rows(m, jnp.maximum)
        z = z + jnp.exp(m - top)
        cur = [jnp.where(c == m, ninf, c) for c in cur]
    tau = m
    count = lambda c: jnp.where(c >= tau, 1.0, 0.0)
    nsel = []
    for r in rows:
        f = count(r[0])
        for c in r[1:]:
            f = f + count(c)
        nsel.append(_allreduce_rows(f, jnp.add))
    for a in range(PAIR_ROW_FORM, PEER_TOPK):
        f = count(v1[a] + v2[0])
        for b in range(1, limit(a)):
            f = f + count(v1[a] + v2[b])
        nsel.append(f)
    total = nsel[0]
    for f in nsel[1:]:
        total = total + f
    return nsel, z, total != float(PEER_TOPK)


def _key_ranks(s, v, lookup=None):
    outs = []
    ranked = None
    for j in range(PEER_TOPK):
        x = s[SUBLANES * j:SUBLANES * (j + 1), :]
        bits, low = _rank_bits(x, v)
        if lookup is None:
            b3, b2, b1, b0 = bits
            val = (jnp.where(b3, 8.0, 0.0) + jnp.where(b2, 4.0, 0.0)
                   + jnp.where(b1, 2.0, 0.0) + jnp.where(b0, 1.0, 0.0) + jnp.where(low, 1.0, 0.0))
        else:
            val = jnp.where(low, 0.0, _select16(bits, lookup))
        outs.append(val)
        r = jnp.where(low, 0.0, 1.0)
        ranked = r if ranked is None else ranked + r
    tie = _allreduce_rows(ranked, jnp.add) != float(PEER_TOPK)
    for a in range(PEER_TOPK - 1):
        tie = tie | (v[a] == v[a + 1])
    return jnp.concatenate(outs, axis=0), tie


def _peer_select_kernel(x_ref, g_ref, wq_ref, k1_ref, k2_ref,
                        pos2_ref, m2_ref, nsel_ref, m1_ref, q_scr, s1_scr, s2_scr, *, tm):
    h = _rms(x_ref[...], g_ref[...]).astype(BF16)
    q = jnp.dot(h, wq_ref[...], preferred_element_type=F32)
    for j in range(2 * PEER_HEADS):
        q_scr[j] = q[:, j * PEER_KEYS:(j + 1) * PEER_KEYS].astype(BF16)
    k1 = k1_ref[...]
    k2 = k2_ref[...]

    def head(hh, carry):
        s1_scr[...] = lax.dot_general(k1, q_scr[2 * hh], _NT, preferred_element_type=F32)
        s2_scr[...] = lax.dot_general(k2, q_scr[2 * hh + 1], _NT, preferred_element_type=F32)
        tie = None
        for c in range(tm // LANES):
            sl = slice(c * LANES, (c + 1) * LANES)
            s1, s2 = s1_scr[:, sl], s2_scr[:, sl]
            v1 = _sorted_top16(s1)
            v2 = _sorted_top16(s2)
            nsel, z, t0 = _pair_counts(v1, v2)
            pos2, t2 = _key_ranks(s2, v2)
            nfull, t1 = _key_ranks(s1, v1, lookup=nsel)
            t = t0 | t1 | t2
            tie = t if tie is None else tie | t
            pos2_ref[hh, :, sl] = pos2.astype(BF16)
            m2_ref[hh, :, sl] = jnp.exp(s2 - v2[0][0:1, :]).astype(BF16)
            nsel_ref[hh, :, sl] = nfull
            m1_ref[hh, :, sl] = jnp.exp(s1 - v1[0][0:1, :]) * (1.0 / z[0:1, :])

        @pl.when(jnp.max(jnp.where(tie, 1.0, 0.0)) > 0.0)
        def _():
            for c in range(tm // LANES):
                sl = slice(c * LANES, (c + 1) * LANES)
                s1, s2 = s1_scr[:, sl], s2_scr[:, sl]
                vals1, pos1 = _top16(s1)
                vals2, pos2 = _top16(s2)
                nsel, z = _pair_select(vals1, vals2)
                nfull = jnp.zeros(s1.shape, jnp.int32)
                for a in range(PEER_TOPK):
                    nfull = jnp.where(pos1 == a, nsel[a:a + 1, :], nfull)
                pos2_ref[hh, :, sl] = pos2.astype(F32).astype(BF16)
                m2_ref[hh, :, sl] = jnp.exp(s2 - vals2[0]).astype(BF16)
                nsel_ref[hh, :, sl] = nfull.astype(F32)
                m1_ref[hh, :, sl] = jnp.exp(s1 - vals1[0]) * (1.0 / z)

        return carry

    lax.fori_loop(0, PEER_HEADS, head, 0)


PEER_SELECT_TILE = 256


def _peer_select(x, g, w_query, key1, key2):
    t, d = x.shape
    tm = PEER_SELECT_TILE
    nq = w_query.shape[1]
    head = pl.BlockSpec((PEER_HEADS, PEER_KEYS, tm), lambda i: (0, 0, i))
    keyspec = pl.BlockSpec((PEER_KEYS, PEER_KEYS), lambda i: (0, 0))
    shp = lambda dt: jax.ShapeDtypeStruct((PEER_HEADS, PEER_KEYS, t), dt)
    return pl.pallas_call(
        functools.partial(_peer_select_kernel, tm=tm),
        grid=(t // tm,),
        in_specs=[pl.BlockSpec((tm, d), lambda i: (i, 0)), pl.BlockSpec((1, d), lambda i: (0, 0)),
                  pl.BlockSpec((d, nq), lambda i: (0, 0)), keyspec, keyspec],
        out_specs=[head] * 4,
        out_shape=[shp(BF16), shp(BF16), shp(F32), shp(F32)],
        scratch_shapes=[pltpu.VMEM((2 * PEER_HEADS, tm, PEER_KEYS), BF16),
                        pltpu.VMEM((PEER_KEYS, tm), F32), pltpu.VMEM((PEER_KEYS, tm), F32)],
        compiler_params=_params("parallel"),
        name="peer_select",
    )(x, g.reshape(1, d), w_query, key1, key2)


def _zero_after(t):
    bits = pltpu.bitcast(t[0:2 * SUBLANES, :], jnp.uint32)
    return jnp.max(((bits >> 16) >> 16).astype(jnp.int32)).astype(F32).astype(BF16)


PEER_GATE_ROWS = 1


def _peer_dense_kernel(x_ref, g_ref, gf_ref, u_ref, vt_ref, pos2_ref, m2_ref, nsel_ref, m1_ref,
                       o_ref, h_scr, acc_scr, p2_scr, m2_scr, *, tm, te, final_norm):
    j = pl.program_id(1)

    @pl.when(j == 0)
    def _():
        h_scr[...] = _rms(x_ref[...], g_ref[...]).astype(BF16)
        acc_scr[...] = jnp.zeros(acc_scr.shape, F32)
        p2_scr[...] = pos2_ref[...]
        m2_scr[...] = m2_ref[...]

    n1 = te // PEER_KEYS
    first = pl.multiple_of(j * n1, n1)
    half = n1 // 2

    def activations(r0, r1, gate=None):
        hop = h_scr[...]
        if gate is not None:
            hop = hop + gate
        return lax.dot_general(u_ref[r0 * PEER_KEYS:r1 * PEER_KEYS, :], hop, _NT, preferred_element_type=F32)

    def weighted(a, base, r0, r1):
        blocks = []
        for r in range(r0, r1):
            tiles = []
            for ct in range(tm // LANES):
                sl = slice(ct * LANES, (ct + 1) * LANES)
                w = None
                for hh in range(PEER_HEADS):
                    nrow = nsel_ref[hh, pl.ds(first, n1), sl][r:r + 1, :].astype(BF16)
                    mrow = m1_ref[hh, pl.ds(first, n1), sl][r:r + 1, :].astype(BF16)
                    t = jnp.where(p2_scr[hh, :, sl] < nrow, m2_scr[hh, :, sl] * mrow, 0)
                    w = t if w is None else w + t
                at = a[(r - base) * PEER_KEYS:(r - base + 1) * PEER_KEYS, sl]
                tiles.append(_gelu(at).astype(BF16) * w)
            blocks.append(jnp.concatenate(tiles, axis=1))
        return blocks

    a_top = activations(0, half)
    z = weighted(a_top, 0, 0, PEER_GATE_ROWS)
    a_bot = activations(half, n1, gate=_zero_after(z[-1]))
    z = z + weighted(a_top, 0, PEER_GATE_ROWS, half) + weighted(a_bot, half, half, n1)
    acc_scr[...] += jnp.dot(vt_ref[...], jnp.concatenate(z, axis=0), preferred_element_type=F32)

    @pl.when(j == pl.num_programs(1) - 1)
    def _():
        out = x_ref[...] + acc_scr[...].T
        if final_norm:
            out = _rms(out, gf_ref[...])
        o_ref[...] = out


def _peer_dense(x, g, g_final, u, vt, sel, final_norm):
    t, d = x.shape
    tm = min(ROW_TILE, t)
    te = vt.shape[2]
    assert (te // PEER_KEYS) % SUBLANES == 0
    row = pl.BlockSpec((tm, d), lambda i, j: (i, 0))
    vec = pl.BlockSpec((1, d), lambda i, j: (0, 0))
    head = pl.BlockSpec((PEER_HEADS, PEER_KEYS, tm), lambda i, j: (0, 0, i))
    return pl.pallas_call(
        functools.partial(_peer_dense_kernel, tm=tm, te=te, final_norm=final_norm),
        grid=(t // tm, PEER_EXPERTS // te),
        in_specs=[row, vec, vec, pl.BlockSpec((te, d), lambda i, j: (j, 0)),
                  pl.BlockSpec((None, d, te), lambda i, j: (j, 0, 0)), head, head, head, head],
        out_specs=row,
        out_shape=jax.ShapeDtypeStruct((t, d), F32),
        scratch_shapes=[pltpu.VMEM((tm, d), BF16), pltpu.VMEM((d, tm), F32),
                        pltpu.VMEM((PEER_HEADS, PEER_KEYS, tm), BF16),
                        pltpu.VMEM((PEER_HEADS, PEER_KEYS, tm), BF16)],
        compiler_params=_params("parallel", "arbitrary"),
        name="peer_dense",
    )(x, g.reshape(1, d), g_final.reshape(1, d), u, vt, *sel)


def _peer(x, g, g_final, w_query, key1, key2, u, vt, final_norm):
    shp = x.shape
    x2 = x.reshape(-1, shp[-1])
    sel = _peer_select(x2, g, w_query, key1, key2)
    return _peer_dense(x2, g, g_final, u, vt, sel, final_norm).reshape(shp)


def _trunk(x, ssm0, conv0, mem_k, mem_v, w, sample):
    b, s, d = x.shape
    chunk = s if sample else S5_CHUNK
    if ssm0 is None:
        ssm0 = (jnp.zeros((b, SSM_GROUPS, SSM_STATE), F32),) * 2
    x, ssm_re, ssm_im = _s5_mixer(x, ssm0[0], ssm0[1], w['norm_mix'][0], w['s5_fold'][chunk],
                                  w['ssm_d'][0], w['ssm_w_glu'][0], chunk)
    attn = _attn_sample if sample else _attn_prompt
    for i in range(2):
        if i == 1:
            if sample:
                x_tm = x.transpose(1, 0, 2).reshape(s * b, d)
                buf_tm = conv0.transpose(1, 0, 2).reshape(2 * b, d)
                x_tm, nbuf = _conv_sample(x_tm, buf_tm, w['norm_mix'][1], w['conv_w_in'][0],
                                          w['conv_w'][0], w['conv_w_out'][0], s, b)
                x = x_tm.reshape(s, b, d).transpose(1, 0, 2)
                conv_out = nbuf.reshape(2, b, d).transpose(1, 0, 2)
            else:
                x, tail = _conv_prompt(x, w['norm_mix'][1], w['conv_w_in'][0], w['conv_w'][0],
                                       w['conv_w_out'][0])
                conv_out = tail[:, CONV_HEAD - 2:]
        x = attn(x, w['norm_mem'][i], mem_k, mem_v, i, w['mem_w_q'][i], w['mem_w_o'][i])
        x = _peer(x, w['norm_ffn'][i], w['norm_final'], w['peer_w_query'][i], w['peer_key1'][i],
                  w['peer_key2'][i], w['peer_u'][i], w['peer_vt'][i], final_norm=(i == 1))
    return x, ssm_re[None], ssm_im[None], conv_out[None]


def kernel(x_prompt, x_sample, mem_prompt, state_ssm_re, state_ssm_im, state_conv, cache_mem_k, cache_mem_v, norm_mix, norm_mem, norm_ffn, norm_final, ssm_a_re, ssm_a_im, ssm_log_dt, ssm_b_re, ssm_b_im, ssm_c_re, ssm_c_im, ssm_d, ssm_w_glu, conv_w_in, conv_w, conv_w_out, mem_w_q, mem_w_k, mem_w_v, mem_w_o, peer_w_query, peer_key1, peer_key2, peer_u, peer_v):
    bsz, seq, d = x_prompt.shape
    dec_b, dec_s, _ = x_sample.shape
    depth = mem_w_q.shape[0]
    fold = lambda chunk: _s5_fold(ssm_a_re[0], ssm_a_im[0], ssm_log_dt[0], ssm_b_re[0], ssm_b_im[0],
                                  ssm_c_re[0], ssm_c_im[0], chunk)
    w = dict(
        norm_mix=norm_mix, norm_mem=norm_mem, norm_ffn=norm_ffn, norm_final=norm_final,
        s5_fold={S5_CHUNK: fold(S5_CHUNK), dec_s: fold(dec_s)},
        ssm_d=ssm_d, ssm_w_glu=ssm_w_glu.astype(BF16),
        conv_w_in=conv_w_in.astype(BF16), conv_w=conv_w, conv_w_out=conv_w_out.astype(BF16),
        mem_w_q=mem_w_q.astype(BF16), mem_w_o=mem_w_o.astype(BF16),
        peer_w_query=peer_w_query.astype(BF16), peer_key1=peer_key1.astype(BF16),
        peer_key2=peer_key2.astype(BF16), peer_u=peer_u.astype(BF16),
        peer_vt=peer_v.astype(BF16).reshape(depth, PEER_EXPERTS // EXPERT_TILE, EXPERT_TILE, d)
        .transpose(0, 1, 3, 2),
    )
    w_kv = jnp.concatenate([mem_w_k, mem_w_v], axis=0).astype(BF16)
    kv = _mm_stacked(mem_prompt.reshape(bsz * MEM_TOKENS, d), w_kv).reshape(2 * depth, bsz, MEM_TOKENS, d)
    mem_k_p, mem_v_p = kv[:depth], kv[depth:]

    y_p, re_p, im_p, conv_p = _trunk(x_prompt, None, None, mem_k_p, mem_v_p, w, sample=False)
    y_s, re_s, im_s, conv_s = _trunk(x_sample, (state_ssm_re[0], state_ssm_im[0]), state_conv[0],
                                     cache_mem_k, cache_mem_v, w, sample=True)
    head_shape = (depth, bsz, MEM_TOKENS, MEM_HEADS, MEM_HEAD_DIM)
    return (y_p, y_s, re_p, im_p, conv_p, mem_k_p.reshape(head_shape), mem_v_p.reshape(head_shape),
            re_s, im_s, conv_s)
```

```python
import functools
import math

import jax
import jax.numpy as jnp
from jax import lax
from jax.experimental import pallas as pl
from jax.experimental.pallas import tpu as pltpu

F32 = jnp.float32
BF16 = jnp.bfloat16

D_MODEL = 1024
SSM_GROUP = 16
SSM_GROUPS = D_MODEL // SSM_GROUP
SSM_STATE = 64
SSM_BLOCK_GROUPS = 8
SSM_BLOCKS = SSM_GROUPS // SSM_BLOCK_GROUPS
MEM_TOKENS = 256
MEM_HEADS = 4
MEM_HEAD_DIM = D_MODEL // MEM_HEADS
PEER_HEADS = 8
PEER_KEYS = 128
PEER_EXPERTS = PEER_KEYS * PEER_KEYS
PEER_TOPK = 16
RMS_EPS = 1e-6

LANES = 128
SUBLANES = 8
ROW_TILE = 512
S5_CHUNK = 8
S5_COL_TILE = 512
EXPERT_TILE = 2048
VMEM_LIMIT = 48 * 1024 * 1024

_NT = (((1,), (1,)), ((), ()))


def _params(*sem):
    return pltpu.CompilerParams(dimension_semantics=sem, vmem_limit_bytes=VMEM_LIMIT)


def _rms(x, g):
    r = lax.rsqrt(jnp.mean(x * x, axis=-1, keepdims=True) + RMS_EPS)
    return (x * r) * g


def _gelu(x):
    return 0.5 * x * (1.0 + lax.erf(x * (1.0 / math.sqrt(2.0))))


def _sigmoid(x):
    return 1.0 / (1.0 + jnp.exp(-x))


def _norm_cast_kernel(x_ref, g_ref, o_ref):
    o_ref[...] = _rms(x_ref[...], g_ref[...]).astype(BF16)


def _norm_cast(x, g):
    t, d = x.shape
    tm = min(ROW_TILE, t)
    return pl.pallas_call(
        _norm_cast_kernel,
        grid=(t // tm,),
        in_specs=[pl.BlockSpec((tm, d), lambda i: (i, 0)), pl.BlockSpec((1, d), lambda i: (0, 0))],
        out_specs=pl.BlockSpec((tm, d), lambda i: (i, 0)),
        out_shape=jax.ShapeDtypeStruct((t, d), BF16),
        compiler_params=_params("parallel"),
        name="norm_cast",
    )(x, g.reshape(1, d))


def _mm_kernel(*refs, has_norm, has_res):
    x_ref, w_ref = refs[0], refs[1]
    pos = 2
    x = x_ref[...]
    if has_norm:
        x = _rms(x, refs[pos][...])
        pos += 1
    y = jnp.dot(x.astype(BF16), w_ref[...], preferred_element_type=F32)
    if has_res:
        y = y + refs[pos][...]
        pos += 1
    refs[pos][...] = y


def _mm(x, w, g=None, res=None, tm=ROW_TILE):
    t, k = x.shape
    n = w.shape[1]
    tm = min(tm, t)
    args = [x, w]
    specs = [pl.BlockSpec((tm, k), lambda i: (i, 0)), pl.BlockSpec((k, n), lambda i: (0, 0))]
    if g is not None:
        args.append(g.reshape(1, k))
        specs.append(pl.BlockSpec((1, k), lambda i: (0, 0)))
    if res is not None:
        args.append(res)
        specs.append(pl.BlockSpec((tm, n), lambda i: (i, 0)))
    return pl.pallas_call(
        functools.partial(_mm_kernel, has_norm=g is not None, has_res=res is not None),
        grid=(t // tm,),
        in_specs=specs,
        out_specs=pl.BlockSpec((tm, n), lambda i: (i, 0)),
        out_shape=jax.ShapeDtypeStruct((t, n), F32),
        compiler_params=_params("parallel"),
        name="mm_rows",
    )(*args)


def _mm_stacked_kernel(x_ref, w_ref, o_ref):
    o_ref[...] = jnp.dot(x_ref[...].astype(BF16), w_ref[...], preferred_element_type=F32)


def _mm_stacked(x, w, tm=ROW_TILE):
    t, k = x.shape
    n, _, c = w.shape
    tm = min(tm, t)
    return pl.pallas_call(
        _mm_stacked_kernel,
        grid=(n, t // tm),
        in_specs=[pl.BlockSpec((tm, k), lambda j, i: (i, 0)), pl.BlockSpec((None, k, c), lambda j, i: (j, 0, 0))],
        out_specs=pl.BlockSpec((None, tm, c), lambda j, i: (j, i, 0)),
        out_shape=jax.ShapeDtypeStruct((n, t, c), F32),
        compiler_params=_params("parallel", "parallel"),
        name="mm_stacked",
    )(x, w)


def _s5_fold(a_re, a_im, log_dt, b_re, b_im, c_re, c_im, chunk):
    hi = lax.Precision.HIGHEST
    dt = jnp.exp(log_dt)[:, None]
    mag = jnp.exp(a_re * dt)
    ang = a_im * dt
    lb_re = mag * jnp.cos(ang)
    lb_im = mag * jnp.sin(ang)
    den = a_re * a_re + a_im * a_im
    f_re = ((lb_re - 1.0) * a_re + lb_im * a_im) / den
    f_im = (lb_im * a_re - (lb_re - 1.0) * a_im) / den
    bb_re = f_re[..., None] * b_re - f_im[..., None] * b_im
    bb_im = f_re[..., None] * b_im + f_im[..., None] * b_re
    pw_re, pw_im = [jnp.ones_like(lb_re)], [jnp.zeros_like(lb_im)]
    for _ in range(chunk):
        pr, pi = pw_re[-1], pw_im[-1]
        pw_re.append(pr * lb_re - pi * lb_im)
        pw_im.append(pr * lb_im + pi * lb_re)
    pw_re, pw_im = jnp.stack(pw_re), jnp.stack(pw_im)
    cl_re = c_re[None] * pw_re[:, :, None, :] - c_im[None] * pw_im[:, :, None, :]
    cl_im = c_re[None] * pw_im[:, :, None, :] + c_im[None] * pw_re[:, :, None, :]
    kern = (jnp.einsum('kgdp,gpc->kgdc', cl_re[:chunk], bb_re, precision=hi)
            - jnp.einsum('kgdp,gpc->kgdc', cl_im[:chunk], bb_im, precision=hi))
    step = jnp.arange(chunk)
    rp_re, rp_im = pw_re[chunk - 1 - step], pw_im[chunk - 1 - step]
    p_re = rp_re[..., None] * bb_re[None] - rp_im[..., None] * bb_im[None]
    p_im = rp_re[..., None] * bb_im[None] + rp_im[..., None] * bb_re[None]
    p_re = p_re.transpose(1, 0, 3, 2)
    p_im = p_im.transpose(1, 0, 3, 2)
    n_re = cl_re[1:].transpose(1, 3, 0, 2)
    n_im = (-cl_im[1:]).transpose(1, 3, 0, 2)

    nb, q = SSM_BLOCKS, SSM_BLOCK_GROUPS
    eye = jnp.eye(q, dtype=BF16)
    w = chunk * q * SSM_GROUP
    sl = q * SSM_STATE
    kq = kern.transpose(1, 3, 0, 2).astype(BF16).reshape(nb, q, SSM_GROUP, chunk, 1, SSM_GROUP)
    kq = (kq * eye[None, :, None, None, :, None]).reshape(nb, q, SSM_GROUP, chunk, q * SSM_GROUP)
    kq = jnp.pad(kq, ((0, 0), (0, 0), (0, 0), (chunk - 1, 0), (0, 0)))
    m = kq.transpose(0, 3, 1, 2, 4).reshape(nb, 2 * chunk - 1, q * SSM_GROUP, q * SSM_GROUP)

    def fold_p(z):
        z = z.astype(BF16).reshape(nb, q, chunk, SSM_GROUP, 1, SSM_STATE).transpose(0, 2, 1, 3, 4, 5)
        return (z * eye[None, None, :, None, :, None]).reshape(nb, w, sl)

    def fold_n(z):
        z = z.astype(BF16).reshape(nb, q, SSM_STATE, chunk, 1, SSM_GROUP)
        return (z * eye[None, :, None, None, :, None]).reshape(nb, sl, w)

    lam_re = pw_re[chunk].reshape(nb, 1, sl)
    lam_im = pw_im[chunk].reshape(nb, 1, sl)
    return m, fold_p(p_re), fold_p(p_im), fold_n(n_re), fold_n(n_im), lam_re, lam_im


def _s5_core_kernel(x_ref, m_ref, pre_ref, pim_ref, nre_ref, nim_ref, lre_ref, lim_ref,
                    s0re_ref, s0im_ref, y_ref, sre_ref, sim_ref, stre, stim, m_scr, *, nchunks, rows, chunk):
    x = x_ref[0]
    col_steps = m_scr.shape[1] // LANES
    for t in range(chunk):
        for uu in range(col_steps):
            lag = chunk - 1 - t + pl.program_id(1) * col_steps + uu
            m_scr[t * LANES:(t + 1) * LANES, uu * LANES:(uu + 1) * LANES] = m_ref[0, lag]

    @pl.when(pl.program_id(1) == 0)
    def _():
        stre[...] = jnp.dot(x, pre_ref[0], preferred_element_type=F32)
        stim[...] = jnp.dot(x, pim_ref[0], preferred_element_type=F32)
        width = stre.shape[1]
        lr = jnp.broadcast_to(lre_ref[0], (rows, width))
        li = jnp.broadcast_to(lim_ref[0], (rows, width))

        def body(k, carry):
            sr, si = carry
            off = pl.multiple_of(k * rows, rows)
            qr = stre[pl.ds(off, rows), :]
            qi = stim[pl.ds(off, rows), :]
            stre[pl.ds(off, rows), :] = sr
            stim[pl.ds(off, rows), :] = si
            return lr * sr - li * si + qr, lr * si + li * sr + qi

        sr, si = lax.fori_loop(0, nchunks, body, (s0re_ref[0], s0im_ref[0]))
        sre_ref[0] = sr
        sim_ref[0] = si

    y = jnp.dot(x, m_scr[...], preferred_element_type=F32)
    y = y + jnp.dot(stre[...].astype(BF16), nre_ref[0], preferred_element_type=F32)
    y = y + jnp.dot(stim[...].astype(BF16), nim_ref[0], preferred_element_type=F32)
    y_ref[0] = y


def _s5_core(xb, folded, s0_re, s0_im, nchunks, rows):
    m, p_re, p_im, n_re, n_im, lam_re, lam_im = folded
    nblk, nr, w = xb.shape
    sl = SSM_BLOCK_GROUPS * SSM_STATE
    wc = min(w, S5_COL_TILE)
    chunk = w // LANES
    assert m.shape == (nblk, 2 * chunk - 1, LANES, LANES)
    fixed = lambda a, b: pl.BlockSpec((1, a, b), lambda i, j: (i, 0, 0))
    cols = lambda a: pl.BlockSpec((1, a, wc), lambda i, j: (i, 0, j))
    lags = pl.BlockSpec((1,) + m.shape[1:], lambda i, j: (i, 0, 0, 0))
    return pl.pallas_call(
        functools.partial(_s5_core_kernel, nchunks=nchunks, rows=rows, chunk=chunk),
        grid=(nblk, w // wc),
        in_specs=[fixed(nr, w), lags, fixed(w, sl), fixed(w, sl), cols(sl), cols(sl),
                  fixed(1, sl), fixed(1, sl), fixed(rows, sl), fixed(rows, sl)],
        out_specs=[cols(nr), fixed(rows, sl), fixed(rows, sl)],
        out_shape=[jax.ShapeDtypeStruct((nblk, nr, w), F32),
                   jax.ShapeDtypeStruct((nblk, rows, sl), F32),
                   jax.ShapeDtypeStruct((nblk, rows, sl), F32)],
        scratch_shapes=[pltpu.VMEM((nr, sl), F32)] * 2 + [pltpu.VMEM((w, wc), BF16)],
        compiler_params=_params("parallel", "arbitrary"),
        name="s5_core",
    )(xb, m, p_re, p_im, n_re, n_im, lam_re, lam_im, s0_re, s0_im)


def _s5_out_kernel(x_ref, y_ref, g_ref, d_ref, w_ref, o_ref):
    x = x_ref[...]
    h = _rms(x, g_ref[...])
    z = _gelu(y_ref[...] + d_ref[...] * h)
    gg = jnp.dot(z.astype(BF16), w_ref[...], preferred_element_type=F32)
    o_ref[...] = x + gg[:, :D_MODEL] * _sigmoid(gg[:, D_MODEL:])


def _s5_out(x, y, g, d, w_glu):
    t = x.shape[0]
    tm = min(ROW_TILE, t)
    row = pl.BlockSpec((tm, D_MODEL), lambda i: (i, 0))
    vec = pl.BlockSpec((1, D_MODEL), lambda i: (0, 0))
    return pl.pallas_call(
        _s5_out_kernel,
        grid=(t // tm,),
        in_specs=[row, row, vec, vec, pl.BlockSpec((D_MODEL, 2 * D_MODEL), lambda i: (0, 0))],
        out_specs=row,
        out_shape=jax.ShapeDtypeStruct((t, D_MODEL), F32),
        compiler_params=_params("parallel"),
        name="s5_out",
    )(x, y, g.reshape(1, -1), d.reshape(1, -1), w_glu)


def _s5_mixer(x, s0_re, s0_im, g, folded, d_skip, w_glu, chunk):
    b, s, d = x.shape
    nk = s // chunk
    x2 = x.reshape(b * s, d)
    h = _norm_cast(x2, g)
    xb = (h.reshape(b, nk, chunk, SSM_BLOCKS, LANES).transpose(3, 1, 0, 2, 4)
          .reshape(SSM_BLOCKS, nk * b, chunk * LANES))
    sl = SSM_BLOCK_GROUPS * SSM_STATE
    to_block = lambda z: z.reshape(b, SSM_BLOCKS, sl).transpose(1, 0, 2)
    y, sr, si = _s5_core(xb, folded, to_block(s0_re), to_block(s0_im), nk, b)
    y = y.reshape(SSM_BLOCKS, nk, b, chunk, LANES).transpose(2, 1, 3, 0, 4).reshape(b * s, d)
    from_block = lambda z: z.transpose(1, 0, 2).reshape(b, SSM_GROUPS, SSM_STATE)
    out = _s5_out(x2, y, g, d_skip, w_glu)
    return out.reshape(b, s, d), from_block(sr), from_block(si)


CONV_HEAD = 8


def _conv_prompt_kernel(x_ref, g_ref, win_ref, wc_ref, wout_ref, o_ref, tail_ref, vbuf, *, tm):
    @pl.when(pl.program_id(1) == 0)
    def _():
        vbuf[0:CONV_HEAD, :] = jnp.zeros((CONV_HEAD, D_MODEL), F32)

    x = x_ref[0]
    h = _rms(x, g_ref[...])
    p = jnp.dot(h.astype(BF16), win_ref[...], preferred_element_type=F32)
    bg = p[:, :D_MODEL]
    v = p[:, D_MODEL:2 * D_MODEL] * p[:, 2 * D_MODEL:]
    vbuf[CONV_HEAD:CONV_HEAD + tm, :] = v
    v2 = vbuf[CONV_HEAD - 2:CONV_HEAD - 2 + tm, :]
    v1 = vbuf[CONV_HEAD - 1:CONV_HEAD - 1 + tm, :]
    conv = wc_ref[0:1, :] * v2 + wc_ref[1:2, :] * v1 + wc_ref[2:3, :] * v
    out = jnp.dot((bg * conv).astype(BF16), wout_ref[...], preferred_element_type=F32)
    o_ref[0] = x + out
    tail = vbuf[tm:tm + CONV_HEAD, :]
    vbuf[0:CONV_HEAD, :] = tail
    tail_ref[0] = tail


def _conv_prompt(x, g, w_in, w_conv, w_out):
    b, s, d = x.shape
    tm = min(ROW_TILE, s)
    return pl.pallas_call(
        functools.partial(_conv_prompt_kernel, tm=tm),
        grid=(b, s // tm),
        in_specs=[pl.BlockSpec((1, tm, d), lambda i, j: (i, j, 0)),
                  pl.BlockSpec((1, d), lambda i, j: (0, 0)),
                  pl.BlockSpec((d, 3 * d), lambda i, j: (0, 0)),
                  pl.BlockSpec((3, d), lambda i, j: (0, 0)),
                  pl.BlockSpec((d, d), lambda i, j: (0, 0))],
        out_specs=[pl.BlockSpec((1, tm, d), lambda i, j: (i, j, 0)),
                   pl.BlockSpec((1, CONV_HEAD, d), lambda i, j: (i, 0, 0))],
        out_shape=[jax.ShapeDtypeStruct((b, s, d), F32),
                   jax.ShapeDtypeStruct((b, CONV_HEAD, d), F32)],
        scratch_shapes=[pltpu.VMEM((CONV_HEAD + tm, d), F32)],
        compiler_params=_params("parallel", "arbitrary"),
        name="conv_prompt",
    )(x, g.reshape(1, d), w_in, w_conv, w_out)


def _conv_sample_kernel(x_ref, buf_ref, g_ref, win_ref, wc_ref, wout_ref, o_ref, nbuf_ref, *, steps, nb):
    x = x_ref[...]
    h = _rms(x, g_ref[...])
    p = jnp.dot(h.astype(BF16), win_ref[...], preferred_element_type=F32)
    bg = p[:, :D_MODEL]
    v = p[:, D_MODEL:2 * D_MODEL] * p[:, 2 * D_MODEL:]
    vp = [buf_ref[0:nb, :], buf_ref[nb:2 * nb, :]] + [v[t * nb:(t + 1) * nb, :] for t in range(steps)]
    conv = jnp.concatenate(
        [wc_ref[0:1, :] * vp[t] + wc_ref[1:2, :] * vp[t + 1] + wc_ref[2:3, :] * vp[t + 2]
         for t in range(steps)], axis=0)
    out = jnp.dot((bg * conv).astype(BF16), wout_ref[...], preferred_element_type=F32)
    o_ref[...] = x + out
    nbuf_ref[0:nb, :] = vp[-2]
    nbuf_ref[nb:2 * nb, :] = vp[-1]


def _conv_sample(x_tm, buf_tm, g, w_in, w_conv, w_out, steps, nb):
    d = D_MODEL
    full = lambda r, c: pl.BlockSpec((r, c), lambda i: (0, 0))
    return pl.pallas_call(
        functools.partial(_conv_sample_kernel, steps=steps, nb=nb),
        grid=(1,),
        in_specs=[full(steps * nb, d), full(2 * nb, d), full(1, d), full(d, 3 * d), full(3, d), full(d, d)],
        out_specs=[full(steps * nb, d), full(2 * nb, d)],
        out_shape=[jax.ShapeDtypeStruct((steps * nb, d), F32), jax.ShapeDtypeStruct((2 * nb, d), F32)],
        compiler_params=_params("arbitrary"),
        name="conv_sample",
    )(x_tm, buf_tm, g.reshape(1, d), w_in, w_conv, w_out)


def _softmax_rows(s):
    e = jnp.exp(s - jnp.max(s, axis=-1, keepdims=True))
    return e / jnp.sum(e, axis=-1, keepdims=True)


def _attn_heads(q, k, v):
    k = k.astype(BF16)
    v = v.astype(BF16)
    outs = []
    for hd in range(MEM_HEADS):
        sl = slice(hd * MEM_HEAD_DIM, (hd + 1) * MEM_HEAD_DIM)
        s = lax.dot_general(q[:, sl].astype(BF16), k[:, sl], _NT, preferred_element_type=F32)
        p = _softmax_rows(s * (MEM_HEAD_DIM ** -0.5))
        outs.append(jnp.dot(p.astype(BF16), v[:, sl], preferred_element_type=F32))
    return jnp.concatenate(outs, axis=-1)


def _attn_prompt_kernel(x_ref, g_ref, k_ref, v_ref, wq_ref, wo_ref, o_ref):
    x = x_ref[0]
    h = _rms(x, g_ref[...])
    q = jnp.dot(h.astype(BF16), wq_ref[...], preferred_element_type=F32)
    o = _attn_heads(q, k_ref[0], v_ref[0])
    o_ref[0] = x + jnp.dot(o.astype(BF16), wo_ref[...], preferred_element_type=F32)


def _attn_prompt(x, g, mem_k, mem_v, layer, w_q, w_o):
    b, s, d = x.shape
    tm = min(ROW_TILE, s)
    mem = pl.BlockSpec((None, 1, MEM_TOKENS, d), lambda i, j: (layer, i, 0, 0))
    wgt = pl.BlockSpec((d, d), lambda i, j: (0, 0))
    row = pl.BlockSpec((1, tm, d), lambda i, j: (i, j, 0))
    return pl.pallas_call(
        _attn_prompt_kernel,
        grid=(b, s // tm),
        in_specs=[row, pl.BlockSpec((1, d), lambda i, j: (0, 0)), mem, mem, wgt, wgt],
        out_specs=row,
        out_shape=jax.ShapeDtypeStruct((b, s, d), F32),
        compiler_params=_params("parallel", "parallel"),
        name="attn_prompt",
    )(x, g.reshape(1, d), mem_k, mem_v, w_q, w_o)


SAMPLE_Q_ROWS = 8
SAMPLE_SEQ_BLOCK = 4


def _attn_sample_kernel(q_ref, k_ref, v_ref, o_ref):
    nrow = SAMPLE_Q_ROWS * MEM_HEADS
    ncol = MEM_TOKENS * MEM_HEADS
    rows = lax.broadcasted_iota(jnp.int32, (nrow, ncol), 0)
    cols = lax.broadcasted_iota(jnp.int32, (nrow, ncol), 1)
    same_head = (rows % MEM_HEADS) == (cols % MEM_HEADS)

    def body(i, carry):
        k = k_ref[i].reshape(ncol, MEM_HEAD_DIM).astype(BF16)
        v = v_ref[i].reshape(ncol, MEM_HEAD_DIM).astype(BF16)
        s = lax.dot_general(q_ref[i].astype(BF16), k, _NT, preferred_element_type=F32)
        p = _softmax_rows(jnp.where(same_head, s * (MEM_HEAD_DIM ** -0.5), -jnp.inf))
        o_ref[i] = jnp.dot(p.astype(BF16), v, preferred_element_type=F32)
        return carry

    lax.fori_loop(0, SAMPLE_SEQ_BLOCK, body, 0)


def _attn_sample(x, g, cache_k, cache_v, layer, w_q, w_o):
    b, s, d = x.shape
    x2 = x.reshape(b * s, d)
    q = _mm(x2, w_q, g=g).reshape(b, s, MEM_HEADS, MEM_HEAD_DIM)
    q = jnp.pad(q, ((0, 0), (0, SAMPLE_Q_ROWS - s), (0, 0), (0, 0)))
    q = q.reshape(b, SAMPLE_Q_ROWS * MEM_HEADS, MEM_HEAD_DIM)
    qblk = pl.BlockSpec((SAMPLE_SEQ_BLOCK, SAMPLE_Q_ROWS * MEM_HEADS, MEM_HEAD_DIM), lambda i: (i, 0, 0))
    mblk = pl.BlockSpec((None, SAMPLE_SEQ_BLOCK, MEM_TOKENS, MEM_HEADS, MEM_HEAD_DIM),
                        lambda i: (layer, i, 0, 0, 0))
    o = pl.pallas_call(
        _attn_sample_kernel,
        grid=(b // SAMPLE_SEQ_BLOCK,),
        in_specs=[qblk, mblk, mblk],
        out_specs=qblk,
        out_shape=jax.ShapeDtypeStruct((b, SAMPLE_Q_ROWS * MEM_HEADS, MEM_HEAD_DIM), F32),
        compiler_params=_params("parallel"),
        name="attn_sample",
    )(q, cache_k, cache_v)
    o = o.reshape(b, SAMPLE_Q_ROWS, d)[:, :s].reshape(b * s, d)
    return _mm(o, w_o, res=x2).reshape(b, s, d)


def _top16(s):
    iota = lax.broadcasted_iota(jnp.int32, s.shape, 0)
    pos = jnp.full(s.shape, PEER_TOPK, jnp.int32)
    vals = []
    for i in range(PEER_TOPK):
        m = jnp.max(s, axis=0, keepdims=True)
        idx = jnp.min(jnp.where(s == m, iota, s.shape[0]), axis=0, keepdims=True)
        hit = iota == idx
        pos = jnp.where(hit, i, pos)
        s = jnp.where(hit, -jnp.inf, s)
        vals.append(m)
    return vals, pos


def _pair_select(vals1, vals2):
    lanes = vals1[0].shape[1]
    iota16 = lax.broadcasted_iota(jnp.int32, (PEER_TOPK, lanes), 0)
    v2 = jnp.zeros((PEER_TOPK, lanes), F32)
    for b in range(PEER_TOPK):
        v2 = jnp.where(iota16 == b, vals2[b], v2)
    iota8 = iota16[:SUBLANES]
    cand = [vals1[0] + v2]
    flat = [iota16]
    for a in range(1, PEER_TOPK):
        cand.append(jnp.where(iota8 < PEER_TOPK // (a + 1), vals1[a] + v2[:SUBLANES], -jnp.inf))
        flat.append(iota8 + a * PEER_TOPK)
    nsel = jnp.zeros((PEER_TOPK, lanes), jnp.int32)
    z = jnp.zeros((1, lanes), F32)
    top = vals1[0] + vals2[0]
    nflat = PEER_TOPK * PEER_TOPK
    for _ in range(PEER_TOPK):
        mx = jnp.maximum(cand[0][:SUBLANES], cand[0][SUBLANES:])
        for a in range(1, PEER_TOPK):
            mx = jnp.maximum(mx, cand[a])
        m = jnp.max(mx, axis=0, keepdims=True)
        w0 = jnp.where(cand[0] == m, flat[0], nflat)
        mi = jnp.minimum(w0[:SUBLANES], w0[SUBLANES:])
        for a in range(1, PEER_TOPK):
            mi = jnp.minimum(mi, jnp.where(cand[a] == m, flat[a], nflat))
        idx = jnp.min(mi, axis=0, keepdims=True)
        cand = [jnp.where(flat[a] == idx, -jnp.inf, cand[a]) for a in range(PEER_TOPK)]
        nsel = nsel + (iota16 == (idx >> (PEER_TOPK.bit_length() - 1))).astype(jnp.int32)
        z = z + jnp.exp(m - top)
    return nsel, z


def _sort_network(n):
    pairs = []

    def merge(lo, cnt, r):
        m = r * 2
        if m < cnt:
            merge(lo, cnt, m)
            merge(lo + r, cnt, m)
            for i in range(lo + r, lo + cnt - r, m):
                pairs.append((i, i + r))
        else:
            pairs.append((lo, lo + r))

    def sort(lo, cnt):
        if cnt > 1:
            m = cnt // 2
            sort(lo, m)
            sort(lo + m, m)
            merge(lo, cnt, 1)

    sort(0, n)
    return tuple(pairs)


_SORT16 = _sort_network(PEER_TOPK)
_HALVINGS = tuple(PEER_TOPK >> k for k in range(1, PEER_TOPK.bit_length()))
assert _HALVINGS[0] == SUBLANES


def _exchange(v, i, j):
    v[i], v[j] = jnp.maximum(v[i], v[j]), jnp.minimum(v[i], v[j])


def _allreduce_rows(x, op):
    for shift in _HALVINGS[1:]:
        x = op(x, pltpu.roll(x, shift, 0))
    return x


def _sorted_top16(s):
    v = [s[SUBLANES * j:SUBLANES * (j + 1), :] for j in range(PEER_TOPK)]
    for i, j in _SORT16:
        _exchange(v, i, j)
    for shift in _HALVINGS[1:]:
        r = [pltpu.roll(x, shift, 0) for x in v]
        v = [jnp.maximum(v[i], r[PEER_TOPK - 1 - i]) for i in range(PEER_TOPK)]
        for stride in _HALVINGS:
            for i in range(PEER_TOPK):
                if i & stride == 0:
                    _exchange(v, i, i + stride)
    return v


def _rank_bits(x, v):
    b3 = v[7] > x
    b2 = jnp.where(b3, v[11], v[3]) > x
    b1 = jnp.where(b3, jnp.where(b2, v[13], v[9]), jnp.where(b2, v[5], v[1])) > x
    t = jnp.where(b3,
                  jnp.where(b2, jnp.where(b1, v[14], v[12]), jnp.where(b1, v[10], v[8])),
                  jnp.where(b2, jnp.where(b1, v[6], v[4]), jnp.where(b1, v[2], v[0])))
    return (b3, b2, b1, t > x), v[PEER_TOPK - 1] > x


def _select16(bits, vals):
    b3, b2, b1, b0 = bits
    lvl = [jnp.where(b0, vals[2 * i + 1], vals[2 * i]) for i in range(8)]
    lvl = [jnp.where(b1, lvl[2 * i + 1], lvl[2 * i]) for i in range(4)]
    lvl = [jnp.where(b2, lvl[2 * i + 1], lvl[2 * i]) for i in range(2)]
    return jnp.where(b3, lvl[1], lvl[0])


PAIR_ROW_FORM = 4


def _pair_counts(v1, v2):
    sub = lax.broadcasted_iota(jnp.int32, v1[0].shape, 0)
    ninf = -jnp.inf
    limit = lambda a: PEER_TOPK // (a + 1)

    def column(v, off):
        col = v[off]
        for b in range(1, SUBLANES):
            col = jnp.where(sub == b, v[off + b], col)
        return col

    def keep(c, off, lo, hi):
        lo, hi = max(lo, off), min(hi, off + SUBLANES)
        if lo >= hi:
            return None
        if (lo, hi) == (off, off + SUBLANES):
            return c
        if hi - lo == 1:
            return jnp.where(sub == lo - off, c, ninf)
        if lo == off:
            return jnp.where(sub < hi - off, c, ninf)
        assert hi == off + SUBLANES
        return jnp.where(sub >= lo - off, c, ninf)

    halves = (0, SUBLANES)
    v1c = [column(v1, off) for off in halves]
    v2c = [column(v2, off) for off in halves]
    rows = []
    for a in range(PAIR_ROW_FORM):
        parts = [keep(v1[a] + v2c[k], off, 0, limit(a)) for k, off in enumerate(halves)]
        rows.append([c for c in parts if c is not None])
    cols = []
    for b in range(limit(PAIR_ROW_FORM)):
        parts = [keep(v1c[k] + v2[b], off, PAIR_ROW_FORM, limit(b)) for k, off in enumerate(halves)]
        cols += [c for c in parts if c is not None]
    cur = [c for r in rows for c in r] + cols
    top = v1[0] + v2[0]
    z = jnp.zeros_like(top)
    for _ in range(PEER_TOPK):
        m = cur[0]
        for c in cur[1:]:
            m = jnp.maximum(m, c)
        m = _allreduce_rows(m, jnp.maximum)
        z = z + jnp.exp(m - top)
        cur = [jnp.where(c == m, ninf, c) for c in cur]
    tau = m
    count = lambda c: jnp.where(c >= tau, 1.0, 0.0)
    nsel = []
    for r in rows:
        f = count(r[0])
        for c in r[1:]:
            f = f + count(c)
        nsel.append(_allreduce_rows(f, jnp.add))
    for a in range(PAIR_ROW_FORM, PEER_TOPK):
        f = count(v1[a] + v2[0])
        for b in range(1, limit(a)):
            f = f + count(v1[a] + v2[b])
        nsel.append(f)
    total = nsel[0]
    for f in nsel[1:]:
        total = total + f
    return nsel, z, total != float(PEER_TOPK)


def _key_ranks(s, v, lookup=None):
    outs = []
    ranked = None
    for j in range(PEER_TOPK):
        x = s[SUBLANES * j:SUBLANES * (j + 1), :]
        bits, low = _rank_bits(x, v)
        if lookup is None:
            b3, b2, b1, b0 = bits
            val = (jnp.where(b3, 8.0, 0.0) + jnp.where(b2, 4.0, 0.0)
                   + jnp.where(b1, 2.0, 0.0) + jnp.where(b0, 1.0, 0.0) + jnp.where(low, 1.0, 0.0))
        else:
            val = jnp.where(low, 0.0, _select16(bits, lookup))
        outs.append(val)
        r = jnp.where(low, 0.0, 1.0)
        ranked = r if ranked is None else ranked + r
    tie = _allreduce_rows(ranked, jnp.add) != float(PEER_TOPK)
    for a in range(PEER_TOPK - 1):
        tie = tie | (v[a] == v[a + 1])
    return jnp.concatenate(outs, axis=0), tie


def _peer_select_kernel(x_ref, g_ref, wq_ref, k1_ref, k2_ref,
                        pos2_ref, m2_ref, nsel_ref, m1_ref, q_scr, s1_scr, s2_scr, *, tm):
    h = _rms(x_ref[...], g_ref[...]).astype(BF16)
    q = jnp.dot(h, wq_ref[...], preferred_element_type=F32)
    for j in range(2 * PEER_HEADS):
        q_scr[j] = q[:, j * PEER_KEYS:(j + 1) * PEER_KEYS].astype(BF16)
    k1 = k1_ref[...]
    k2 = k2_ref[...]

    def head(hh, carry):
        s1_scr[...] = lax.dot_general(k1, q_scr[2 * hh], _NT, preferred_element_type=F32)
        s2_scr[...] = lax.dot_general(k2, q_scr[2 * hh + 1], _NT, preferred_element_type=F32)
        tie = None
        for c in range(tm // LANES):
            sl = slice(c * LANES, (c + 1) * LANES)
            s1, s2 = s1_scr[:, sl], s2_scr[:, sl]
            v1 = _sorted_top16(s1)
            v2 = _sorted_top16(s2)
            nsel, z, t0 = _pair_counts(v1, v2)
            pos2, t2 = _key_ranks(s2, v2)
            nfull, t1 = _key_ranks(s1, v1, lookup=nsel)
            t = t0 | t1 | t2
            tie = t if tie is None else tie | t
            pos2_ref[hh, :, sl] = pos2.astype(BF16)
            m2_ref[hh, :, sl] = jnp.exp(s2 - v2[0][0:1, :]).astype(BF16)
            nsel_ref[hh, :, sl] = nfull
            m1_ref[hh, :, sl] = jnp.exp(s1 - v1[0][0:1, :]) * (1.0 / z[0:1, :])

        @pl.when(jnp.max(jnp.where(tie, 1.0, 0.0)) > 0.0)
        def _():
            for c in range(tm // LANES):
                sl = slice(c * LANES, (c + 1) * LANES)
                s1, s2 = s1_scr[:, sl], s2_scr[:, sl]
                vals1, pos1 = _top16(s1)
                vals2, pos2 = _top16(s2)
                nsel, z = _pair_select(vals1, vals2)
                nfull = jnp.zeros(s1.shape, jnp.int32)
                for a in range(PEER_TOPK):
                    nfull = jnp.where(pos1 == a, nsel[a:a + 1, :], nfull)
                pos2_ref[hh, :, sl] = pos2.astype(F32).astype(BF16)
                m2_ref[hh, :, sl] = jnp.exp(s2 - vals2[0]).astype(BF16)
                nsel_ref[hh, :, sl] = nfull.astype(F32)
                m1_ref[hh, :, sl] = jnp.exp(s1 - vals1[0]) * (1.0 / z)

        return carry

    lax.fori_loop(0, PEER_HEADS, head, 0)


PEER_SELECT_TILE = 256


def _peer_select(x, g, w_query, key1, key2):
    t, d = x.shape
    tm = PEER_SELECT_TILE
    nq = w_query.shape[1]
    head = pl.BlockSpec((PEER_HEADS, PEER_KEYS, tm), lambda i: (0, 0, i))
    keyspec = pl.BlockSpec((PEER_KEYS, PEER_KEYS), lambda i: (0, 0))
    shp = lambda dt: jax.ShapeDtypeStruct((PEER_HEADS, PEER_KEYS, t), dt)
    return pl.pallas_call(
        functools.partial(_peer_select_kernel, tm=tm),
        grid=(t // tm,),
        in_specs=[pl.BlockSpec((tm, d), lambda i: (i, 0)), pl.BlockSpec((1, d), lambda i: (0, 0)),
                  pl.BlockSpec((d, nq), lambda i: (0, 0)), keyspec, keyspec],
        out_specs=[head] * 4,
        out_shape=[shp(BF16), shp(BF16), shp(F32), shp(F32)],
        scratch_shapes=[pltpu.VMEM((2 * PEER_HEADS, tm, PEER_KEYS), BF16),
                        pltpu.VMEM((PEER_KEYS, tm), F32), pltpu.VMEM((PEER_KEYS, tm), F32)],
        compiler_params=_params("parallel"),
        name="peer_select",
    )(x, g.reshape(1, d), w_query, key1, key2)


def _zero_after(t):
    bits = pltpu.bitcast(t[0:2 * SUBLANES, :], jnp.uint32)
    return jnp.max(((bits >> 16) >> 16).astype(jnp.int32)).astype(F32).astype(BF16)


PEER_GATE_ROWS = 1


def _peer_dense_kernel(x_ref, g_ref, gf_ref, u_ref, vt_ref, pos2_ref, m2_ref, nsel_ref, m1_ref,
                       o_ref, h_scr, acc_scr, p2_scr, m2_scr, *, tm, te, final_norm):
    j = pl.program_id(1)

    @pl.when(j == 0)
    def _():
        h_scr[...] = _rms(x_ref[...], g_ref[...]).astype(BF16)
        acc_scr[...] = jnp.zeros(acc_scr.shape, F32)
        p2_scr[...] = pos2_ref[...]
        m2_scr[...] = m2_ref[...]

    n1 = te // PEER_KEYS
    first = pl.multiple_of(j * n1, n1)
    half = n1 // 2

    def activations(r0, r1, gate=None):
        hop = h_scr[...]
        if gate is not None:
            hop = hop + gate
        return lax.dot_general(u_ref[r0 * PEER_KEYS:r1 * PEER_KEYS, :], hop, _NT, preferred_element_type=F32)

    def weighted(a, base, r0, r1):
        blocks = []
        for r in range(r0, r1):
            tiles = []
            for ct in range(tm // LANES):
                sl = slice(ct * LANES, (ct + 1) * LANES)
                w = None
                for hh in range(PEER_HEADS):
                    nrow = nsel_ref[hh, pl.ds(first, n1), sl][r:r + 1, :].astype(BF16)
                    mrow = m1_ref[hh, pl.ds(first, n1), sl][r:r + 1, :].astype(BF16)
                    t = jnp.where(p2_scr[hh, :, sl] < nrow, m2_scr[hh, :, sl] * mrow, 0)
                    w = t if w is None else w + t
                at = a[(r - base) * PEER_KEYS:(r - base + 1) * PEER_KEYS, sl]
                tiles.append(_gelu(at).astype(BF16) * w)
            blocks.append(jnp.concatenate(tiles, axis=1))
        return blocks

    a_top = activations(0, half)
    z = weighted(a_top, 0, 0, PEER_GATE_ROWS)
    a_bot = activations(half, n1, gate=_zero_after(z[-1]))
    z = z + weighted(a_top, 0, PEER_GATE_ROWS, half) + weighted(a_bot, half, half, n1)
    acc_scr[...] += jnp.dot(vt_ref[...], jnp.concatenate(z, axis=0), preferred_element_type=F32)

    @pl.when(j == pl.num_programs(1) - 1)
    def _():
        out = x_ref[...] + acc_scr[...].T
        if final_norm:
            out = _rms(out, gf_ref[...])
        o_ref[...] = out


def _peer_dense(x, g, g_final, u, vt, sel, final_norm):
    t, d = x.shape
    tm = min(ROW_TILE, t)
    te = vt.shape[2]
    assert (te // PEER_KEYS) % SUBLANES == 0
    row = pl.BlockSpec((tm, d), lambda i, j: (i, 0))
    vec = pl.BlockSpec((1, d), lambda i, j: (0, 0))
    head = pl.BlockSpec((PEER_HEADS, PEER_KEYS, tm), lambda i, j: (0, 0, i))
    return pl.pallas_call(
        functools.partial(_peer_dense_kernel, tm=tm, te=te, final_norm=final_norm),
        grid=(t // tm, PEER_EXPERTS // te),
        in_specs=[row, vec, vec, pl.BlockSpec((te, d), lambda i, j: (j, 0)),
                  pl.BlockSpec((None, d, te), lambda i, j: (j, 0, 0)), head, head, head, head],
        out_specs=row,
        out_shape=jax.ShapeDtypeStruct((t, d), F32),
        scratch_shapes=[pltpu.VMEM((tm, d), BF16), pltpu.VMEM((d, tm), F32),
                        pltpu.VMEM((PEER_HEADS, PEER_KEYS, tm), BF16),
                        pltpu.VMEM((PEER_HEADS, PEER_KEYS, tm), BF16)],
        compiler_params=_params("parallel", "arbitrary"),
        name="peer_dense",
    )(x, g.reshape(1, d), g_final.reshape(1, d), u, vt, *sel)


def _peer(x, g, g_final, w_query, key1, key2, u, vt, final_norm):
    shp = x.shape
    x2 = x.reshape(-1, shp[-1])
    sel = _peer_select(x2, g, w_query, key1, key2)
    return _peer_dense(x2, g, g_final, u, vt, sel, final_norm).reshape(shp)


def _trunk(x, ssm0, conv0, mem_k, mem_v, w, sample):
    b, s, d = x.shape
    chunk = s if sample else S5_CHUNK
    if ssm0 is None:
        ssm0 = (jnp.zeros((b, SSM_GROUPS, SSM_STATE), F32),) * 2
    x, ssm_re, ssm_im = _s5_mixer(x, ssm0[0], ssm0[1], w['norm_mix'][0], w['s5_fold'][chunk],
                                  w['ssm_d'][0], w['ssm_w_glu'][0], chunk)
    attn = _attn_sample if sample else _attn_prompt
    for i in range(2):
        if i == 1:
            if sample:
                x_tm = x.transpose(1, 0, 2).reshape(s * b, d)
                buf_tm = conv0.transpose(1, 0, 2).reshape(2 * b, d)
                x_tm, nbuf = _conv_sample(x_tm, buf_tm, w['norm_mix'][1], w['conv_w_in'][0],
                                          w['conv_w'][0], w['conv_w_out'][0], s, b)
                x = x_tm.reshape(s, b, d).transpose(1, 0, 2)
                conv_out = nbuf.reshape(2, b, d).transpose(1, 0, 2)
            else:
                x, tail = _conv_prompt(x, w['norm_mix'][1], w['conv_w_in'][0], w['conv_w'][0],
                                       w['conv_w_out'][0])
                conv_out = tail[:, CONV_HEAD - 2:]
        x = attn(x, w['norm_mem'][i], mem_k, mem_v, i, w['mem_w_q'][i], w['mem_w_o'][i])
        x = _peer(x, w['norm_ffn'][i], w['norm_final'], w['peer_w_query'][i], w['peer_key1'][i],
                  w['peer_key2'][i], w['peer_u'][i], w['peer_vt'][i], final_norm=(i == 1))
    return x, ssm_re[None], ssm_im[None], conv_out[None]


def kernel(x_prompt, x_sample, mem_prompt, state_ssm_re, state_ssm_im, state_conv, cache_mem_k, cache_mem_v, norm_mix, norm_mem, norm_ffn, norm_final, ssm_a_re, ssm_a_im, ssm_log_dt, ssm_b_re, ssm_b_im, ssm_c_re, ssm_c_im, ssm_d, ssm_w_glu, conv_w_in, conv_w, conv_w_out, mem_w_q, mem_w_k, mem_w_v, mem_w_o, peer_w_query, peer_key1, peer_key2, peer_u, peer_v):
    bsz, seq, d = x_prompt.shape
    dec_b, dec_s, _ = x_sample.shape
    depth = mem_w_q.shape[0]
    fold = lambda chunk: _s5_fold(ssm_a_re[0], ssm_a_im[0], ssm_log_dt[0], ssm_b_re[0], ssm_b_im[0],
                                  ssm_c_re[0], ssm_c_im[0], chunk)
    w = dict(
        norm_mix=norm_mix, norm_mem=norm_mem, norm_ffn=norm_ffn, norm_final=norm_final,
        s5_fold={S5_CHUNK: fold(S5_CHUNK), dec_s: fold(dec_s)},
        ssm_d=ssm_d, ssm_w_glu=ssm_w_glu.astype(BF16),
        conv_w_in=conv_w_in.astype(BF16), conv_w=conv_w, conv_w_out=conv_w_out.astype(BF16),
        mem_w_q=mem_w_q.astype(BF16), mem_w_o=mem_w_o.astype(BF16),
        peer_w_query=peer_w_query.astype(BF16), peer_key1=peer_key1.astype(BF16),
        peer_key2=peer_key2.astype(BF16), peer_u=peer_u.astype(BF16),
        peer_vt=peer_v.astype(BF16).reshape(depth, PEER_EXPERTS // EXPERT_TILE, EXPERT_TILE, d)
        .transpose(0, 1, 3, 2),
    )
    w_kv = jnp.concatenate([mem_w_k, mem_w_v], axis=0).astype(BF16)
    kv = _mm_stacked(mem_prompt.reshape(bsz * MEM_TOKENS, d), w_kv).reshape(2 * depth, bsz, MEM_TOKENS, d)
    mem_k_p, mem_v_p = kv[:depth], kv[depth:]

    y_p, re_p, im_p, conv_p = _trunk(x_prompt, None, None, mem_k_p, mem_v_p, w, sample=False)
    y_s, re_s, im_s, conv_s = _trunk(x_sample, (state_ssm_re[0], state_ssm_im[0]), state_conv[0],
                                     cache_mem_k, cache_mem_v, w, sample=True)
    head_shape = (depth, bsz, MEM_TOKENS, MEM_HEADS, MEM_HEAD_DIM)
    return (y_p, y_s, re_p, im_p, conv_p, mem_k_p.reshape(head_shape), mem_v_p.reshape(head_shape),
            re_s, im_s, conv_s)
```

```python
import functools
import math

import jax
import jax.numpy as jnp
from jax import lax
from jax.experimental import pallas as pl
from jax.experimental.pallas import tpu as pltpu

F32 = jnp.float32
BF16 = jnp.bfloat16

D_MODEL = 1024
SSM_GROUP = 16
SSM_GROUPS = D_MODEL // SSM_GROUP
SSM_STATE = 64
SSM_BLOCK_GROUPS = 8
SSM_BLOCKS = SSM_GROUPS // SSM_BLOCK_GROUPS
MEM_TOKENS = 256
MEM_HEADS = 4
MEM_HEAD_DIM = D_MODEL // MEM_HEADS
PEER_HEADS = 8
PEER_KEYS = 128
PEER_EXPERTS = PEER_KEYS * PEER_KEYS
PEER_TOPK = 16
RMS_EPS = 1e-6

LANES = 128
SUBLANES = 8
ROW_TILE = 512
S5_CHUNK = 8
S5_COL_TILE = 512
EXPERT_TILE = 2048
VMEM_LIMIT = 48 * 1024 * 1024

_NT = (((1,), (1,)), ((), ()))


def _params(*sem):
    return pltpu.CompilerParams(dimension_semantics=sem, vmem_limit_bytes=VMEM_LIMIT)


def _rms(x, g):
    r = lax.rsqrt(jnp.mean(x * x, axis=-1, keepdims=True) + RMS_EPS)
    return (x * r) * g


def _gelu(x):
    return 0.5 * x * (1.0 + lax.erf(x * (1.0 / math.sqrt(2.0))))


def _sigmoid(x):
    return 1.0 / (1.0 + jnp.exp(-x))


def _norm_cast_kernel(x_ref, g_ref, o_ref):
    o_ref[...] = _rms(x_ref[...], g_ref[...]).astype(BF16)


def _norm_cast(x, g):
    t, d = x.shape
    tm = min(ROW_TILE, t)
    return pl.pallas_call(
        _norm_cast_kernel,
        grid=(t // tm,),
        in_specs=[pl.BlockSpec((tm, d), lambda i: (i, 0)), pl.BlockSpec((1, d), lambda i: (0, 0))],
        out_specs=pl.BlockSpec((tm, d), lambda i: (i, 0)),
        out_shape=jax.ShapeDtypeStruct((t, d), BF16),
        compiler_params=_params("parallel"),
        name="norm_cast",
    )(x, g.reshape(1, d))


def _mm_kernel(*refs, has_norm, has_res):
    x_ref, w_ref = refs[0], refs[1]
    pos = 2
    x = x_ref[...]
    if has_norm:
        x = _rms(x, refs[pos][...])
        pos += 1
    y = jnp.dot(x.astype(BF16), w_ref[...], preferred_element_type=F32)
    if has_res:
        y = y + refs[pos][...]
        pos += 1
    refs[pos][...] = y


def _mm(x, w, g=None, res=None, tm=ROW_TILE):
    t, k = x.shape
    n = w.shape[1]
    tm = min(tm, t)
    args = [x, w]
    specs = [pl.BlockSpec((tm, k), lambda i: (i, 0)), pl.BlockSpec((k, n), lambda i: (0, 0))]
    if g is not None:
        args.append(g.reshape(1, k))
        specs.append(pl.BlockSpec((1, k), lambda i: (0, 0)))
    if res is not None:
        args.append(res)
        specs.append(pl.BlockSpec((tm, n), lambda i: (i, 0)))
    return pl.pallas_call(
        functools.partial(_mm_kernel, has_norm=g is not None, has_res=res is not None),
        grid=(t // tm,),
        in_specs=specs,
        out_specs=pl.BlockSpec((tm, n), lambda i: (i, 0)),
        out_shape=jax.ShapeDtypeStruct((t, n), F32),
        compiler_params=_params("parallel"),
        name="mm_rows",
    )(*args)


def _mm_stacked_kernel(x_ref, w_ref, o_ref):
    o_ref[...] = jnp.dot(x_ref[...].astype(BF16), w_ref[...], preferred_element_type=F32)


def _mm_stacked(x, w, tm=ROW_TILE):
    t, k = x.shape
    n, _, c = w.shape
    tm = min(tm, t)
    return pl.pallas_call(
        _mm_stacked_kernel,
        grid=(n, t // tm),
        in_specs=[pl.BlockSpec((tm, k), lambda j, i: (i, 0)), pl.BlockSpec((None, k, c), lambda j, i: (j, 0, 0))],
        out_specs=pl.BlockSpec((None, tm, c), lambda j, i: (j, i, 0)),
        out_shape=jax.ShapeDtypeStruct((n, t, c), F32),
        compiler_params=_params("parallel", "parallel"),
        name="mm_stacked",
    )(x, w)


def _s5_fold(a_re, a_im, log_dt, b_re, b_im, c_re, c_im, chunk):
    hi = lax.Precision.HIGHEST
    dt = jnp.exp(log_dt)[:, None]
    mag = jnp.exp(a_re * dt)
    ang = a_im * dt
    lb_re = mag * jnp.cos(ang)
    lb_im = mag * jnp.sin(ang)
    den = a_re * a_re + a_im * a_im
    f_re = ((lb_re - 1.0) * a_re + lb_im * a_im) / den
    f_im = (lb_im * a_re - (lb_re - 1.0) * a_im) / den
    bb_re = f_re[..., None] * b_re - f_im[..., None] * b_im
    bb_im = f_re[..., None] * b_im + f_im[..., None] * b_re
    pw_re, pw_im = [jnp.ones_like(lb_re)], [jnp.zeros_like(lb_im)]
    for _ in range(chunk):
        pr, pi = pw_re[-1], pw_im[-1]
        pw_re.append(pr * lb_re - pi * lb_im)
        pw_im.append(pr * lb_im + pi * lb_re)
    pw_re, pw_im = jnp.stack(pw_re), jnp.stack(pw_im)
    cl_re = c_re[None] * pw_re[:, :, None, :] - c_im[None] * pw_im[:, :, None, :]
    cl_im = c_re[None] * pw_im[:, :, None, :] + c_im[None] * pw_re[:, :, None, :]
    kern = (jnp.einsum('kgdp,gpc->kgdc', cl_re[:chunk], bb_re, precision=hi)
            - jnp.einsum('kgdp,gpc->kgdc', cl_im[:chunk], bb_im, precision=hi))
    step = jnp.arange(chunk)
    rp_re, rp_im = pw_re[chunk - 1 - step], pw_im[chunk - 1 - step]
    p_re = rp_re[..., None] * bb_re[None] - rp_im[..., None] * bb_im[None]
    p_im = rp_re[..., None] * bb_im[None] + rp_im[..., None] * bb_re[None]
    p_re = p_re.transpose(1, 0, 3, 2)
    p_im = p_im.transpose(1, 0, 3, 2)
    n_re = cl_re[1:].transpose(1, 3, 0, 2)
    n_im = (-cl_im[1:]).transpose(1, 3, 0, 2)

    nb, q = SSM_BLOCKS, SSM_BLOCK_GROUPS
    eye = jnp.eye(q, dtype=BF16)
    w = chunk * q * SSM_GROUP
    sl = q * SSM_STATE
    kq = kern.transpose(1, 3, 0, 2).astype(BF16).reshape(nb, q, SSM_GROUP, chunk, 1, SSM_GROUP)
    kq = (kq * eye[None, :, None, None, :, None]).reshape(nb, q, SSM_GROUP, chunk, q * SSM_GROUP)
    kq = jnp.pad(kq, ((0, 0), (0, 0), (0, 0), (chunk - 1, 0), (0, 0)))
    m = kq.transpose(0, 3, 1, 2, 4).reshape(nb, 2 * chunk - 1, q * SSM_GROUP, q * SSM_GROUP)

    def fold_p(z):
        z = z.astype(BF16).reshape(nb, q, chunk, SSM_GROUP, 1, SSM_STATE).transpose(0, 2, 1, 3, 4, 5)
        return (z * eye[None, None, :, None, :, None]).reshape(nb, w, sl)

    def fold_n(z):
        z = z.astype(BF16).reshape(nb, q, SSM_STATE, chunk, 1, SSM_GROUP)
        return (z * eye[None, :, None, None, :, None]).reshape(nb, sl, w)

    lam_re = pw_re[chunk].reshape(nb, 1, sl)
    lam_im = pw_im[chunk].reshape(nb, 1, sl)
    return m, fold_p(p_re), fold_p(p_im), fold_n(n_re), fold_n(n_im), lam_re, lam_im


def _s5_core_kernel(x_ref, m_ref, pre_ref, pim_ref, nre_ref, nim_ref, lre_ref, lim_ref,
                    s0re_ref, s0im_ref, y_ref, sre_ref, sim_ref, stre, stim, m_scr, *, nchunks, rows, chunk):
    x = x_ref[0]
    col_steps = m_scr.shape[1] // LANES
    for t in range(chunk):
        for uu in range(col_steps):
            lag = chunk - 1 - t + pl.program_id(1) * col_steps + uu
            m_scr[t * LANES:(t + 1) * LANES, uu * LANES:(uu + 1) * LANES] = m_ref[0, lag]

    @pl.when(pl.program_id(1) == 0)
    def _():
        stre[...] = jnp.dot(x, pre_ref[0], preferred_element_type=F32)
        stim[...] = jnp.dot(x, pim_ref[0], preferred_element_type=F32)
        width = stre.shape[1]
        lr = jnp.broadcast_to(lre_ref[0], (rows, width))
        li = jnp.broadcast_to(lim_ref[0], (rows, width))

        def body(k, carry):
            sr, si = carry
            off = pl.multiple_of(k * rows, rows)
            qr = stre[pl.ds(off, rows), :]
            qi = stim[pl.ds(off, rows), :]
            stre[pl.ds(off, rows), :] = sr
            stim[pl.ds(off, rows), :] = si
            return lr * sr - li * si + qr, lr * si + li * sr + qi

        sr, si = lax.fori_loop(0, nchunks, body, (s0re_ref[0], s0im_ref[0]))
        sre_ref[0] = sr
        sim_ref[0] = si

    y = jnp.dot(x, m_scr[...], preferred_element_type=F32)
    y = y + jnp.dot(stre[...].astype(BF16), nre_ref[0], preferred_element_type=F32)
    y = y + jnp.dot(stim[...].astype(BF16), nim_ref[0], preferred_element_type=F32)
    y_ref[0] = y


def _s5_core(xb, folded, s0_re, s0_im, nchunks, rows):
    m, p_re, p_im, n_re, n_im, lam_re, lam_im = folded
    nblk, nr, w = xb.shape
    sl = SSM_BLOCK_GROUPS * SSM_STATE
    wc = min(w, S5_COL_TILE)
    chunk = w // LANES
    assert m.shape == (nblk, 2 * chunk - 1, LANES, LANES)
    fixed = lambda a, b: pl.BlockSpec((1, a, b), lambda i, j: (i, 0, 0))
    cols = lambda a: pl.BlockSpec((1, a, wc), lambda i, j: (i, 0, j))
    lags = pl.BlockSpec((1,) + m.shape[1:], lambda i, j: (i, 0, 0, 0))
    return pl.pallas_call(
        functools.partial(_s5_core_kernel, nchunks=nchunks, rows=rows, chunk=chunk),
        grid=(nblk, w // wc),
        in_specs=[fixed(nr, w), lags, fixed(w, sl), fixed(w, sl), cols(sl), cols(sl),
                  fixed(1, sl), fixed(1, sl), fixed(rows, sl), fixed(rows, sl)],
        out_specs=[cols(nr), fixed(rows, sl), fixed(rows, sl)],
        out_shape=[jax.ShapeDtypeStruct((nblk, nr, w), F32),
                   jax.ShapeDtypeStruct((nblk, rows, sl), F32),
                   jax.ShapeDtypeStruct((nblk, rows, sl), F32)],
        scratch_shapes=[pltpu.VMEM((nr, sl), F32)] * 2 + [pltpu.VMEM((w, wc), BF16)],
        compiler_params=_params("parallel", "arbitrary"),
        name="s5_core",
    )(xb, m, p_re, p_im, n_re, n_im, lam_re, lam_im, s0_re, s0_im)


def _s5_out_kernel(x_ref, y_ref, g_ref, d_ref, w_ref, o_ref):
    x = x_ref[...]
    h = _rms(x, g_ref[...])
    z = _gelu(y_ref[...] + d_ref[...] * h)
    gg = jnp.dot(z.astype(BF16), w_ref[...], preferred_element_type=F32)
    o_ref[...] = x + gg[:, :D_MODEL] * _sigmoid(gg[:, D_MODEL:])


def _s5_out(x, y, g, d, w_glu):
    t = x.shape[0]
    tm = min(ROW_TILE, t)
    row = pl.BlockSpec((tm, D_MODEL), lambda i: (i, 0))
    vec = pl.BlockSpec((1, D_MODEL), lambda i: (0, 0))
    return pl.pallas_call(
        _s5_out_kernel,
        grid=(t // tm,),
        in_specs=[row, row, vec, vec, pl.BlockSpec((D_MODEL, 2 * D_MODEL), lambda i: (0, 0))],
        out_specs=row,
        out_shape=jax.ShapeDtypeStruct((t, D_MODEL), F32),
        compiler_params=_params("parallel"),
        name="s5_out",
    )(x, y, g.reshape(1, -1), d.reshape(1, -1), w_glu)


def _s5_mixer(x, s0_re, s0_im, g, folded, d_skip, w_glu, chunk):
    b, s, d = x.shape
    nk = s // chunk
    x2 = x.reshape(b * s, d)
    h = _norm_cast(x2, g)
    xb = (h.reshape(b, nk, chunk, SSM_BLOCKS, LANES).transpose(3, 1, 0, 2, 4)
          .reshape(SSM_BLOCKS, nk * b, chunk * LANES))
    sl = SSM_BLOCK_GROUPS * SSM_STATE
    to_block = lambda z: z.reshape(b, SSM_BLOCKS, sl).transpose(1, 0, 2)
    y, sr, si = _s5_core(xb, folded, to_block(s0_re), to_block(s0_im), nk, b)
    y = y.reshape(SSM_BLOCKS, nk, b, chunk, LANES).transpose(2, 1, 3, 0, 4).reshape(b * s, d)
    from_block = lambda z: z.transpose(1, 0, 2).reshape(b, SSM_GROUPS, SSM_STATE)
    out = _s5_out(x2, y, g, d_skip, w_glu)
    return out.reshape(b, s, d), from_block(sr), from_block(si)


CONV_HEAD = 8


def _conv_prompt_kernel(x_ref, g_ref, win_ref, wc_ref, wout_ref, o_ref, tail_ref, vbuf, *, tm):
    @pl.when(pl.program_id(1) == 0)
    def _():
        vbuf[0:CONV_HEAD, :] = jnp.zeros((CONV_HEAD, D_MODEL), F32)

    x = x_ref[0]
    h = _rms(x, g_ref[...])
    p = jnp.dot(h.astype(BF16), win_ref[...], preferred_element_type=F32)
    bg = p[:, :D_MODEL]
    v = p[:, D_MODEL:2 * D_MODEL] * p[:, 2 * D_MODEL:]
    vbuf[CONV_HEAD:CONV_HEAD + tm, :] = v
    v2 = vbuf[CONV_HEAD - 2:CONV_HEAD - 2 + tm, :]
    v1 = vbuf[CONV_HEAD - 1:CONV_HEAD - 1 + tm, :]
    conv = wc_ref[0:1, :] * v2 + wc_ref[1:2, :] * v1 + wc_ref[2:3, :] * v
    out = jnp.dot((bg * conv).astype(BF16), wout_ref[...], preferred_element_type=F32)
    o_ref[0] = x + out
    tail = vbuf[tm:tm + CONV_HEAD, :]
    vbuf[0:CONV_HEAD, :] = tail
    tail_ref[0] = tail


def _conv_prompt(x, g, w_in, w_conv, w_out):
    b, s, d = x.shape
    tm = min(ROW_TILE, s)
    return pl.pallas_call(
        functools.partial(_conv_prompt_kernel, tm=tm),
        grid=(b, s // tm),
        in_specs=[pl.BlockSpec((1, tm, d), lambda i, j: (i, j, 0)),
                  pl.BlockSpec((1, d), lambda i, j: (0, 0)),
                  pl.BlockSpec((d, 3 * d), lambda i, j: (0, 0)),
                  pl.BlockSpec((3, d), lambda i, j: (0, 0)),
                  pl.BlockSpec((d, d), lambda i, j: (0, 0))],
        out_specs=[pl.BlockSpec((1, tm, d), lambda i, j: (i, j, 0)),
                   pl.BlockSpec((1, CONV_HEAD, d), lambda i, j: (i, 0, 0))],
        out_shape=[jax.ShapeDtypeStruct((b, s, d), F32),
                   jax.ShapeDtypeStruct((b, CONV_HEAD, d), F32)],
        scratch_shapes=[pltpu.VMEM((CONV_HEAD + tm, d), F32)],
        compiler_params=_params("parallel", "arbitrary"),
        name="conv_prompt",
    )(x, g.reshape(1, d), w_in, w_conv, w_out)


def _conv_sample_kernel(x_ref, buf_ref, g_ref, win_ref, wc_ref, wout_ref, o_ref, nbuf_ref, *, steps, nb):
    x = x_ref[...]
    h = _rms(x, g_ref[...])
    p = jnp.dot(h.astype(BF16), win_ref[...], preferred_element_type=F32)
    bg = p[:, :D_MODEL]
    v = p[:, D_MODEL:2 * D_MODEL] * p[:, 2 * D_MODEL:]
    vp = [buf_ref[0:nb, :], buf_ref[nb:2 * nb, :]] + [v[t * nb:(t + 1) * nb, :] for t in range(steps)]
    conv = jnp.concatenate(
        [wc_ref[0:1, :] * vp[t] + wc_ref[1:2, :] * vp[t + 1] + wc_ref[2:3, :] * vp[t + 2]
         for t in range(steps)], axis=0)
    out = jnp.dot((bg * conv).astype(BF16), wout_ref[...], preferred_element_type=F32)
    o_ref[...] = x + out
    nbuf_ref[0:nb, :] = vp[-2]
    nbuf_ref[nb:2 * nb, :] = vp[-1]


def _conv_sample(x_tm, buf_tm, g, w_in, w_conv, w_out, steps, nb):
    d = D_MODEL
    full = lambda r, c: pl.BlockSpec((r, c), lambda i: (0, 0))
    return pl.pallas_call(
        functools.partial(_conv_sample_kernel, steps=steps, nb=nb),
        grid=(1,),
        in_specs=[full(steps * nb, d), full(2 * nb, d), full(1, d), full(d, 3 * d), full(3, d), full(d, d)],
        out_specs=[full(steps * nb, d), full(2 * nb, d)],
        out_shape=[jax.ShapeDtypeStruct((steps * nb, d), F32), jax.ShapeDtypeStruct((2 * nb, d), F32)],
        compiler_params=_params("arbitrary"),
        name="conv_sample",
    )(x_tm, buf_tm, g.reshape(1, d), w_in, w_conv, w_out)


def _softmax_rows(s):
    e = jnp.exp(s - jnp.max(s, axis=-1, keepdims=True))
    return e / jnp.sum(e, axis=-1, keepdims=True)


def _attn_heads(q, k, v):
    k = k.astype(BF16)
    v = v.astype(BF16)
    outs = []
    for hd in range(MEM_HEADS):
        sl = slice(hd * MEM_HEAD_DIM, (hd + 1) * MEM_HEAD_DIM)
        s = lax.dot_general(q[:, sl].astype(BF16), k[:, sl], _NT, preferred_element_type=F32)
        p = _softmax_rows(s * (MEM_HEAD_DIM ** -0.5))
        outs.append(jnp.dot(p.astype(BF16), v[:, sl], preferred_element_type=F32))
    return jnp.concatenate(outs, axis=-1)


def _attn_prompt_kernel(x_ref, g_ref, k_ref, v_ref, wq_ref, wo_ref, o_ref):
    x = x_ref[0]
    h = _rms(x, g_ref[...])
    q = jnp.dot(h.astype(BF16), wq_ref[...], preferred_element_type=F32)
    o = _attn_heads(q, k_ref[0], v_ref[0])
    o_ref[0] = x + jnp.dot(o.astype(BF16), wo_ref[...], preferred_element_type=F32)


def _attn_prompt(x, g, mem_k, mem_v, layer, w_q, w_o):
    b, s, d = x.shape
    tm = min(ROW_TILE, s)
    mem = pl.BlockSpec((None, 1, MEM_TOKENS, d), lambda i, j: (layer, i, 0, 0))
    wgt = pl.BlockSpec((d, d), lambda i, j: (0, 0))
    row = pl.BlockSpec((1, tm, d), lambda i, j: (i, j, 0))
    return pl.pallas_call(
        _attn_prompt_kernel,
        grid=(b, s // tm),
        in_specs=[row, pl.BlockSpec((1, d), lambda i, j: (0, 0)), mem, mem, wgt, wgt],
        out_specs=row,
        out_shape=jax.ShapeDtypeStruct((b, s, d), F32),
        compiler_params=_params("parallel", "parallel"),
        name="attn_prompt",
    )(x, g.reshape(1, d), mem_k, mem_v, w_q, w_o)


SAMPLE_Q_ROWS = 8
SAMPLE_SEQ_BLOCK = 4


def _attn_sample_kernel(q_ref, k_ref, v_ref, o_ref):
    nrow = SAMPLE_Q_ROWS * MEM_HEADS
    ncol = MEM_TOKENS * MEM_HEADS
    rows = lax.broadcasted_iota(jnp.int32, (nrow, ncol), 0)
    cols = lax.broadcasted_iota(jnp.int32, (nrow, ncol), 1)
    same_head = (rows % MEM_HEADS) == (cols % MEM_HEADS)

    def body(i, carry):
        k = k_ref[i].reshape(ncol, MEM_HEAD_DIM).astype(BF16)
        v = v_ref[i].reshape(ncol, MEM_HEAD_DIM).astype(BF16)
        s = lax.dot_general(q_ref[i].astype(BF16), k, _NT, preferred_element_type=F32)
        p = _softmax_rows(jnp.where(same_head, s * (MEM_HEAD_DIM ** -0.5), -jnp.inf))
        o_ref[i] = jnp.dot(p.astype(BF16), v, preferred_element_type=F32)
        return carry

    lax.fori_loop(0, SAMPLE_SEQ_BLOCK, body, 0)


def _attn_sample(x, g, cache_k, cache_v, layer, w_q, w_o):
    b, s, d = x.shape
    x2 = x.reshape(b * s, d)
    q = _mm(x2, w_q, g=g).reshape(b, s, MEM_HEADS, MEM_HEAD_DIM)
    q = jnp.pad(q, ((0, 0), (0, SAMPLE_Q_ROWS - s), (0, 0), (0, 0)))
    q = q.reshape(b, SAMPLE_Q_ROWS * MEM_HEADS, MEM_HEAD_DIM)
    qblk = pl.BlockSpec((SAMPLE_SEQ_BLOCK, SAMPLE_Q_ROWS * MEM_HEADS, MEM_HEAD_DIM), lambda i: (i, 0, 0))
    mblk = pl.BlockSpec((None, SAMPLE_SEQ_BLOCK, MEM_TOKENS, MEM_HEADS, MEM_HEAD_DIM),
                        lambda i: (layer, i, 0, 0, 0))
    o = pl.pallas_call(
        _attn_sample_kernel,
        grid=(b // SAMPLE_SEQ_BLOCK,),
        in_specs=[qblk, mblk, mblk],
        out_specs=qblk,
        out_shape=jax.ShapeDtypeStruct((b, SAMPLE_Q_ROWS * MEM_HEADS, MEM_HEAD_DIM), F32),
        compiler_params=_params("parallel"),
        name="attn_sample",
    )(q, cache_k, cache_v)
    o = o.reshape(b, SAMPLE_Q_ROWS, d)[:, :s].reshape(b * s, d)
    return _mm(o, w_o, res=x2).reshape(b, s, d)


def _top16(s):
    iota = lax.broadcasted_iota(jnp.int32, s.shape, 0)
    pos = jnp.full(s.shape, PEER_TOPK, jnp.int32)
    vals = []
    for i in range(PEER_TOPK):
        m = jnp.max(s, axis=0, keepdims=True)
        idx = jnp.min(jnp.where(s == m, iota, s.shape[0]), axis=0, keepdims=True)
        hit = iota == idx
        pos = jnp.where(hit, i, pos)
        s = jnp.where(hit, -jnp.inf, s)
        vals.append(m)
    return vals, pos


def _pair_select(vals1, vals2):
    lanes = vals1[0].shape[1]
    iota16 = lax.broadcasted_iota(jnp.int32, (PEER_TOPK, lanes), 0)
    v2 = jnp.zeros((PEER_TOPK, lanes), F32)
    for b in range(PEER_TOPK):
        v2 = jnp.where(iota16 == b, vals2[b], v2)
    iota8 = iota16[:SUBLANES]
    cand = [vals1[0] + v2]
    flat = [iota16]
    for a in range(1, PEER_TOPK):
        cand.append(jnp.where(iota8 < PEER_TOPK // (a + 1), vals1[a] + v2[:SUBLANES], -jnp.inf))
        flat.append(iota8 + a * PEER_TOPK)
    nsel = jnp.zeros((PEER_TOPK, lanes), jnp.int32)
    z = jnp.zeros((1, lanes), F32)
    top = vals1[0] + vals2[0]
    nflat = PEER_TOPK * PEER_TOPK
    for _ in range(PEER_TOPK):
        mx = jnp.maximum(cand[0][:SUBLANES], cand[0][SUBLANES:])
        for a in range(1, PEER_TOPK):
            mx = jnp.maximum(mx, cand[a])
        m = jnp.max(mx, axis=0, keepdims=True)
        w0 = jnp.where(cand[0] == m, flat[0], nflat)
        mi = jnp.minimum(w0[:SUBLANES], w0[SUBLANES:])
        for a in range(1, PEER_TOPK):
            mi = jnp.minimum(mi, jnp.where(cand[a] == m, flat[a], nflat))
        idx = jnp.min(mi, axis=0, keepdims=True)
        cand = [jnp.where(flat[a] == idx, -jnp.inf, cand[a]) for a in range(PEER_TOPK)]
        nsel = nsel + (iota16 == (idx >> (PEER_TOPK.bit_length() - 1))).astype(jnp.int32)
        z = z + jnp.exp(m - top)
    return nsel, z


def _sort_network(n):
    pairs = []

    def merge(lo, cnt, r):
        m = r * 2
        if m < cnt:
            merge(lo, cnt, m)
            merge(lo + r, cnt, m)
            for i in range(lo + r, lo + cnt - r, m):
                pairs.append((i, i + r))
        else:
            pairs.append((lo, lo + r))

    def sort(lo, cnt):
        if cnt > 1:
            m = cnt // 2
            sort(lo, m)
            sort(lo + m, m)
            merge(lo, cnt, 1)

    sort(0, n)
    return tuple(pairs)


_SORT16 = _sort_network(PEER_TOPK)
_HALVINGS = tuple(PEER_TOPK >> k for k in range(1, PEER_TOPK.bit_length()))
assert _HALVINGS[0] == SUBLANES


def _exchange(v, i, j):
    v[i], v[j] = jnp.maximum(v[i], v[j]), jnp.minimum(v[i], v[j])


def _allreduce_rows(x, op):
    for shift in _HALVINGS[1:]:
        x = op(x, pltpu.roll(x, shift, 0))
    return x


def _sorted_top16(s):
    v = [s[SUBLANES * j:SUBLANES * (j + 1), :] for j in range(PEER_TOPK)]
    for i, j in _SORT16:
        _exchange(v, i, j)
    for shift in _HALVINGS[1:]:
        r = [pltpu.roll(x, shift, 0) for x in v]
        v = [jnp.maximum(v[i], r[PEER_TOPK - 1 - i]) for i in range(PEER_TOPK)]
        for stride in _HALVINGS:
            for i in range(PEER_TOPK):
                if i & stride == 0:
                    _exchange(v, i, i + stride)
    return v


def _rank_bits(x, v):
    b3 = v[7] > x
    b2 = jnp.where(b3, v[11], v[3]) > x
    b1 = jnp.where(b3, jnp.where(b2, v[13], v[9]), jnp.where(b2, v[5], v[1])) > x
    t = jnp.where(b3,
                  jnp.where(b2, jnp.where(b1, v[14], v[12]), jnp.where(b1, v[10], v[8])),
                  jnp.where(b2, jnp.where(b1, v[6], v[4]), jnp.where(b1, v[2], v[0])))
    return (b3, b2, b1, t > x), v[PEER_TOPK - 1] > x


def _select16(bits, vals):
    b3, b2, b1, b0 = bits
    lvl = [jnp.where(b0, vals[2 * i + 1], vals[2 * i]) for i in range(8)]
    lvl = [jnp.where(b1, lvl[2 * i + 1], lvl[2 * i]) for i in range(4)]
    lvl = [jnp.where(b2, lvl[2 * i + 1], lvl[2 * i]) for i in range(2)]
    return jnp.where(b3, lvl[1], lvl[0])


PAIR_ROW_FORM = 4


def _pair_counts(v1, v2):
    sub = lax.broadcasted_iota(jnp.int32, v1[0].shape, 0)
    ninf = -jnp.inf
    limit = lambda a: PEER_TOPK // (a + 1)

    def column(v, off):
        col = v[off]
        for b in range(1, SUBLANES):
            col = jnp.where(sub == b, v[off + b], col)
        return col

    def keep(c, off, lo, hi):
        lo, hi = max(lo, off), min(hi, off + SUBLANES)
        if lo >= hi:
            return None
        if (lo, hi) == (off, off + SUBLANES):
            return c
        if hi - lo == 1:
            return jnp.where(sub == lo - off, c, ninf)
        if lo == off:
            return jnp.where(sub < hi - off, c, ninf)
        assert hi == off + SUBLANES
        return jnp.where(sub >= lo - off, c, ninf)

    halves = (0, SUBLANES)
    v1c = [column(v1, off) for off in halves]
    v2c = [column(v2, off) for off in halves]
    rows = []
    for a in range(PAIR_ROW_FORM):
        parts = [keep(v1[a] + v2c[k], off, 0, limit(a)) for k, off in enumerate(halves)]
        rows.append([c for c in parts if c is not None])
    cols = []
    for b in range(limit(PAIR_ROW_FORM)):
        parts = [keep(v1c[k] + v2[b], off, PAIR_ROW_FORM, limit(b)) for k, off in enumerate(halves)]
        cols += [c for c in parts if c is not None]
    cur = [c for r in rows for c in r] + cols
    top = v1[0] + v2[0]
    z = jnp.zeros_like(top)
    for _ in range(PEER_TOPK):
        m = cur[0]
        for c in cur[1:]:
            m = jnp.maximum(m, c)
        m = _allreduce_rows(m, jnp.maximum)
        z = z + jnp.exp(m - top)
        cur = [jnp.where(c == m, ninf, c) for c in cur]
    tau = m
    count = lambda c: jnp.where(c >= tau, 1.0, 0.0)
    nsel = []
    for r in rows:
        f = count(r[0])
        for c in r[1:]:
            f = f + count(c)
        nsel.append(_allreduce_rows(f, jnp.add))
    for a in range(PAIR_ROW_FORM, PEER_TOPK):
        f = count(v1[a] + v2[0])
        for b in range(1, limit(a)):
            f = f + count(v1[a] + v2[b])
        nsel.append(f)
    total = nsel[0]
    for f in nsel[1:]:
        total = total + f
    return nsel, z, total != float(PEER_TOPK)


def _key_ranks(s, v, lookup=None):
    outs = []
    ranked = None
    for j in range(PEER_TOPK):
        x = s[SUBLANES * j:SUBLANES * (j + 1), :]
        bits, low = _rank_bits(x, v)
        if lookup is None:
            b3, b2, b1, b0 = bits
            val = (jnp.where(b3, 8.0, 0.0) + jnp.where(b2, 4.0, 0.0)
                   + jnp.where(b1, 2.0, 0.0) + jnp.where(b0, 1.0, 0.0) + jnp.where(low, 1.0, 0.0))
        else:
            val = jnp.where(low, 0.0, _select16(bits, lookup))
        outs.append(val)
        r = jnp.where(low, 0.0, 1.0)
        ranked = r if ranked is None else ranked + r
    tie = _allreduce_rows(ranked, jnp.add) != float(PEER_TOPK)
    for a in range(PEER_TOPK - 1):
        tie = tie | (v[a] == v[a + 1])
    return jnp.concatenate(outs, axis=0), tie


def _peer_select_kernel(x_ref, g_ref, wq_ref, k1_ref, k2_ref,
                        pos2_ref, m2_ref, nsel_ref, m1_ref, q_scr, s1_scr, s2_scr, *, tm):
    h = _rms(x_ref[...], g_ref[...]).astype(BF16)
    q = jnp.dot(h, wq_ref[...], preferred_element_type=F32)
    for j in range(2 * PEER_HEADS):
        q_scr[j] = q[:, j * PEER_KEYS:(j + 1) * PEER_KEYS].astype(BF16)
    k1 = k1_ref[...]
    k2 = k2_ref[...]

    def head(hh, carry):
        s1_scr[...] = lax.dot_general(k1, q_scr[2 * hh], _NT, preferred_element_type=F32)
        s2_scr[...] = lax.dot_general(k2, q_scr[2 * hh + 1], _NT, preferred_element_type=F32)
        tie = None
        for c in range(tm // LANES):
            sl = slice(c * LANES, (c + 1) * LANES)
            s1, s2 = s1_scr[:, sl], s2_scr[:, sl]
            v1 = _sorted_top16(s1)
            v2 = _sorted_top16(s2)
            nsel, z, t0 = _pair_counts(v1, v2)
            pos2, t2 = _key_ranks(s2, v2)
            nfull, t1 = _key_ranks(s1, v1, lookup=nsel)
            t = t0 | t1 | t2
            tie = t if tie is None else tie | t
            pos2_ref[hh, :, sl] = pos2.astype(BF16)
            m2_ref[hh, :, sl] = jnp.exp(s2 - v2[0][0:1, :]).astype(BF16)
            nsel_ref[hh, :, sl] = nfull
            m1_ref[hh, :, sl] = jnp.exp(s1 - v1[0][0:1, :]) * (1.0 / z[0:1, :])

        @pl.when(jnp.max(jnp.where(tie, 1.0, 0.0)) > 0.0)
        def _():
            for c in range(tm // LANES):
                sl = slice(c * LANES, (c + 1) * LANES)
                s1, s2 = s1_scr[:, sl], s2_scr[:, sl]
                vals1, pos1 = _top16(s1)
                vals2, pos2 = _top16(s2)
                nsel, z = _pair_select(vals1, vals2)
                nfull = jnp.zeros(s1.shape, jnp.int32)
                for a in range(PEER_TOPK):
                    nfull = jnp.where(pos1 == a, nsel[a:a + 1, :], nfull)
                pos2_ref[hh, :, sl] = pos2.astype(F32).astype(BF16)
                m2_ref[hh, :, sl] = jnp.exp(s2 - vals2[0]).astype(BF16)
                nsel_ref[hh, :, sl] = nfull.astype(F32)
                m1_ref[hh, :, sl] = jnp.exp(s1 - vals1[0]) * (1.0 / z)

        return carry

    lax.fori_loop(0, PEER_HEADS, head, 0)


PEER_SELECT_TILE = 256


def _peer_select(x, g, w_query, key1, key2):
    t, d = x.shape
    tm = PEER_SELECT_TILE
    nq = w_query.shape[1]
    head = pl.BlockSpec((PEER_HEADS, PEER_KEYS, tm), lambda i: (0, 0, i))
    keyspec = pl.BlockSpec((PEER_KEYS, PEER_KEYS), lambda i: (0, 0))
    shp = lambda dt: jax.ShapeDtypeStruct((PEER_HEADS, PEER_KEYS, t), dt)
    return pl.pallas_call(
        functools.partial(_peer_select_kernel, tm=tm),
        grid=(t // tm,),
        in_specs=[pl.BlockSpec((tm, d), lambda i: (i, 0)), pl.BlockSpec((1, d), lambda i: (0, 0)),
                  pl.BlockSpec((d, nq), lambda i: (0, 0)), keyspec, keyspec],
        out_specs=[head] * 4,
        out_shape=[shp(BF16), shp(BF16), shp(F32), shp(F32)],
        scratch_shapes=[pltpu.VMEM((2 * PEER_HEADS, tm, PEER_KEYS), BF16),
                        pltpu.VMEM((PEER_KEYS, tm), F32), pltpu.VMEM((PEER_KEYS, tm), F32)],
        compiler_params=_params("parallel"),
        name="peer_select",
    )(x, g.reshape(1, d), w_query, key1, key2)


def _zero_after(t):
    bits = pltpu.bitcast(t[0:2 * SUBLANES, :], jnp.uint32)
    return jnp.max(((bits >> 16) >> 16).astype(jnp.int32)).astype(F32).astype(BF16)


PEER_GATE_ROWS = 1


def _peer_dense_kernel(x_ref, g_ref, gf_ref, u_ref, vt_ref, pos2_ref, m2_ref, nsel_ref, m1_ref,
                       o_ref, h_scr, acc_scr, p2_scr, m2_scr, *, tm, te, final_norm):
    j = pl.program_id(1)

    @pl.when(j == 0)
    def _():
        h_scr[...] = _rms(x_ref[...], g_ref[...]).astype(BF16)
        acc_scr[...] = jnp.zeros(acc_scr.shape, F32)
        p2_scr[...] = pos2_ref[...]
        m2_scr[...] = m2_ref[...]

    n1 = te // PEER_KEYS
    first = pl.multiple_of(j * n1, n1)
    half = n1 // 2

    def activations(r0, r1, gate=None):
        hop = h_scr[...]
        if gate is not None:
            hop = jnp.concatenate([hop[0:2 * SUBLANES, :] + gate, hop[2 * SUBLANES:, :]], axis=0)
        return lax.dot_general(u_ref[r0 * PEER_KEYS:r1 * PEER_KEYS, :], hop, _NT, preferred_element_type=F32)

    def weighted(a, base, r0, r1):
        blocks = []
        for r in range(r0, r1):
            tiles = []
            for ct in range(tm // LANES):
                sl = slice(ct * LANES, (ct + 1) * LANES)
                w = None
                for hh in range(PEER_HEADS):
                    nrow = nsel_ref[hh, pl.ds(first, n1), sl][r:r + 1, :].astype(BF16)
                    mrow = m1_ref[hh, pl.ds(first, n1), sl][r:r + 1, :].astype(BF16)
                    t = jnp.where(p2_scr[hh, :, sl] < nrow, m2_scr[hh, :, sl] * mrow, 0)
                    w = t if w is None else w + t
                at = a[(r - base) * PEER_KEYS:(r - base + 1) * PEER_KEYS, sl]
                tiles.append(_gelu(at).astype(BF16) * w)
            blocks.append(jnp.concatenate(tiles, axis=1))
        return blocks

    a_top = activations(0, half)
    z = weighted(a_top, 0, 0, PEER_GATE_ROWS)
    a_bot = activations(half, n1, gate=_zero_after(z[-1]))
    z = z + weighted(a_top, 0, PEER_GATE_ROWS, half) + weighted(a_bot, half, half, n1)
    acc_scr[...] += jnp.dot(vt_ref[...], jnp.concatenate(z, axis=0), preferred_element_type=F32)

    @pl.when(j == pl.num_programs(1) - 1)
    def _():
        out = x_ref[...] + acc_scr[...].T
        if final_norm:
            out = _rms(out, gf_ref[...])
        o_ref[...] = out


def _peer_dense(x, g, g_final, u, vt, sel, final_norm):
    t, d = x.shape
    tm = min(ROW_TILE, t)
    te = vt.shape[2]
    assert (te // PEER_KEYS) % SUBLANES == 0
    row = pl.BlockSpec((tm, d), lambda i, j: (i, 0))
    vec = pl.BlockSpec((1, d), lambda i, j: (0, 0))
    head = pl.BlockSpec((PEER_HEADS, PEER_KEYS, tm), lambda i, j: (0, 0, i))
    return pl.pallas_call(
        functools.partial(_peer_dense_kernel, tm=tm, te=te, final_norm=final_norm),
        grid=(t // tm, PEER_EXPERTS // te),
        in_specs=[row, vec, vec, pl.BlockSpec((te, d), lambda i, j: (j, 0)),
                  pl.BlockSpec((None, d, te), lambda i, j: (j, 0, 0)), head, head, head, head],
        out_specs=row,
        out_shape=jax.ShapeDtypeStruct((t, d), F32),
        scratch_shapes=[pltpu.VMEM((tm, d), BF16), pltpu.VMEM((d, tm), F32),
                        pltpu.VMEM((PEER_HEADS, PEER_KEYS, tm), BF16),
                        pltpu.VMEM((PEER_HEADS, PEER_KEYS, tm), BF16)],
        compiler_params=_params("parallel", "arbitrary"),
        name="peer_dense",
    )(x, g.reshape(1, d), g_final.reshape(1, d), u, vt, *sel)


def _peer(x, g, g_final, w_query, key1, key2, u, vt, final_norm):
    shp = x.shape
    x2 = x.reshape(-1, shp[-1])
    sel = _peer_select(x2, g, w_query, key1, key2)
    return _peer_dense(x2, g, g_final, u, vt, sel, final_norm).reshape(shp)


def _trunk(x, ssm0, conv0, mem_k, mem_v, w, sample):
    b, s, d = x.shape
    chunk = s if sample else S5_CHUNK
    if ssm0 is None:
        ssm0 = (jnp.zeros((b, SSM_GROUPS, SSM_STATE), F32),) * 2
    x, ssm_re, ssm_im = _s5_mixer(x, ssm0[0], ssm0[1], w['norm_mix'][0], w['s5_fold'][chunk],
                                  w['ssm_d'][0], w['ssm_w_glu'][0], chunk)
    attn = _attn_sample if sample else _attn_prompt
    for i in range(2):
        if i == 1:
            if sample:
                x_tm = x.transpose(1, 0, 2).reshape(s * b, d)
                buf_tm = conv0.transpose(1, 0, 2).reshape(2 * b, d)
                x_tm, nbuf = _conv_sample(x_tm, buf_tm, w['norm_mix'][1], w['conv_w_in'][0],
                                          w['conv_w'][0], w['conv_w_out'][0], s, b)
                x = x_tm.reshape(s, b, d).transpose(1, 0, 2)
                conv_out = nbuf.reshape(2, b, d).transpose(1, 0, 2)
            else:
                x, tail = _conv_prompt(x, w['norm_mix'][1], w['conv_w_in'][0], w['conv_w'][0],
                                       w['conv_w_out'][0])
                conv_out = tail[:, CONV_HEAD - 2:]
        x = attn(x, w['norm_mem'][i], mem_k, mem_v, i, w['mem_w_q'][i], w['mem_w_o'][i])
        x = _peer(x, w['norm_ffn'][i], w['norm_final'], w['peer_w_query'][i], w['peer_key1'][i],
                  w['peer_key2'][i], w['peer_u'][i], w['peer_vt'][i], final_norm=(i == 1))
    return x, ssm_re[None], ssm_im[None], conv_out[None]


def kernel(x_prompt, x_sample, mem_prompt, state_ssm_re, state_ssm_im, state_conv, cache_mem_k, cache_mem_v, norm_mix, norm_mem, norm_ffn, norm_final, ssm_a_re, ssm_a_im, ssm_log_dt, ssm_b_re, ssm_b_im, ssm_c_re, ssm_c_im, ssm_d, ssm_w_glu, conv_w_in, conv_w, conv_w_out, mem_w_q, mem_w_k, mem_w_v, mem_w_o, peer_w_query, peer_key1, peer_key2, peer_u, peer_v):
    bsz, seq, d = x_prompt.shape
    dec_b, dec_s, _ = x_sample.shape
    depth = mem_w_q.shape[0]
    fold = lambda chunk: _s5_fold(ssm_a_re[0], ssm_a_im[0], ssm_log_dt[0], ssm_b_re[0], ssm_b_im[0],
                                  ssm_c_re[0], ssm_c_im[0], chunk)
    w = dict(
        norm_mix=norm_mix, norm_mem=norm_mem, norm_ffn=norm_ffn, norm_final=norm_final,
        s5_fold={S5_CHUNK: fold(S5_CHUNK), dec_s: fold(dec_s)},
        ssm_d=ssm_d, ssm_w_glu=ssm_w_glu.astype(BF16),
        conv_w_in=conv_w_in.astype(BF16), conv_w=conv_w, conv_w_out=conv_w_out.astype(BF16),
        mem_w_q=mem_w_q.astype(BF16), mem_w_o=mem_w_o.astype(BF16),
        peer_w_query=peer_w_query.astype(BF16), peer_key1=peer_key1.astype(BF16),
        peer_key2=peer_key2.astype(BF16), peer_u=peer_u.astype(BF16),
        peer_vt=peer_v.astype(BF16).reshape(depth, PEER_EXPERTS // EXPERT_TILE, EXPERT_TILE, d)
        .transpose(0, 1, 3, 2),
    )
    w_kv = jnp.concatenate([mem_w_k, mem_w_v], axis=0).astype(BF16)
    kv = _mm_stacked(mem_prompt.reshape(bsz * MEM_TOKENS, d), w_kv).reshape(2 * depth, bsz, MEM_TOKENS, d)
    mem_k_p, mem_v_p = kv[:depth], kv[depth:]

    y_p, re_p, im_p, conv_p = _trunk(x_prompt, None, None, mem_k_p, mem_v_p, w, sample=False)
    y_s, re_s, im_s, conv_s = _trunk(x_sample, (state_ssm_re[0], state_ssm_im[0]), state_conv[0],
                                     cache_mem_k, cache_mem_v, w, sample=True)
    head_shape = (depth, bsz, MEM_TOKENS, MEM_HEADS, MEM_HEAD_DIM)
    return (y_p, y_s, re_p, im_p, conv_p, mem_k_p.reshape(head_shape), mem_v_p.reshape(head_shape),
            re_s, im_s, conv_s)
```
